```python
import jax, jax.numpy as jnp
from jax import lax
import numpy as np

D_MODEL = 2048
BATCH = 1
SEQ = 8192
DEPTH = 1
DEC_BATCH = 32
DEC_SEQ = 4
PAST_LEN = 8192
PAGE_SIZE = 128

HA = 12
DHA = 64
DA = HA * DHA
LORA_W = 64
LORA_A = 64
LNX_EPS = 64e-5
HB = 6
DHB = 128
DB = HB * DHB
HI = 8
DI = 64
TOPK_MAX = 256
Q_BLOCK = 128
N_MEM = 256
HC = 4
DHC = 128
DC = HC * DHC
N_BRANCH = 3
ROPE_THETA = 10000.0
EPS = 1e-6
SHIFT_W = 3 * DA + LORA_W + LORA_A
SPLIT_SIZES = (SHIFT_W, DA, DB, DB, DB, HI * DI, HI, DI, DB, DC, DC, N_BRANCH * D_MODEL)
N_IN = SHIFT_W + DA + 4 * DB + HI * DI + HI + DI + 2 * DC + N_BRANCH * D_MODEL

kernel_name = 'hybrid_rwkv7_dsa_memory_decode_step'


def rmsnorm(x, g):
    xf = x.astype(jnp.float32)
    y = xf * lax.rsqrt(jnp.mean(xf * xf, axis=-1, keepdims=True) + EPS)
    return (y * g).astype(x.dtype)


def split_cols(p):
    idx = [int(i) for i in np.cumsum(SPLIT_SIZES)[:-1]]
    return jnp.split(p, idx, axis=-1)


def rope(x, pos):
    d = x.shape[-1]
    inv = ROPE_THETA ** (-jnp.arange(0, d, 2, dtype=jnp.float32) / d)
    ang = pos.astype(jnp.float32)[:, None] * inv[None, :]
    cos = jnp.cos(ang)[:, None, :]
    sin = jnp.sin(ang)[:, None, :]
    x1, x2 = jnp.split(x.astype(jnp.float32), 2, axis=-1)
    return jnp.concatenate([x1 * cos - x2 * sin, x1 * sin + x2 * cos], axis=-1).astype(x.dtype)


def rwkv7_mix(cur, prev, s0, mu, w0, w_up, a0, a_up, k_k, k_a, r_k, lnx_g, lnx_b):
    B, T, _ = cur.shape
    f32 = jnp.float32
    m = cur + mu * (prev - cur)
    r, k, v, wd, ad = jnp.split(m, [DA, 2 * DA, 3 * DA, 3 * DA + LORA_W], axis=-1)
    logw = -jax.nn.softplus(-(w0 + jnp.tanh(wd) @ w_up)) - 0.5
    decay = jnp.exp(-jnp.exp(logw.astype(f32)))
    a = jax.nn.sigmoid(a0 + ad @ a_up)
    heads = lambda t: t.reshape(B, T, HA, DHA).astype(f32)
    kk = heads(k * k_k)
    kk = kk * lax.rsqrt(jnp.sum(kk * kk, axis=-1, keepdims=True) + 1e-12)
    k = k * (1 + (a - 1) * k_a)
    r_h, k_h, v_h, a_h, w_h = heads(r), heads(k), heads(v), heads(a), heads(decay)

    def step(S, inp):
        r_t, w_t, k_t, v_t, kk_t, a_t = inp
        sa = jnp.einsum('bhvk,bhk->bhv', S, -kk_t)
        S = (S * w_t[:, :, None, :] + sa[..., None] * (kk_t * a_t)[:, :, None, :]
             + v_t[..., None] * k_t[:, :, None, :])
        return S, jnp.einsum('bhvk,bhk->bhv', S, r_t)

    xs = tuple(jnp.moveaxis(t, 1, 0) for t in (r_h, w_h, k_h, v_h, kk, a_h))
    s_fin, y = lax.scan(step, s0.astype(f32), xs)
    y = jnp.moveaxis(y, 0, 1)
    mean = jnp.mean(y, axis=-1, keepdims=True)
    var = jnp.mean(jnp.square(y - mean), axis=-1, keepdims=True)
    y = ((y - mean) * lax.rsqrt(var + LNX_EPS)).reshape(B, T, DA) * lnx_g + lnx_b
    bonus = (jnp.sum(r_h * k_h * r_k, axis=-1, keepdims=True) * v_h).reshape(B, T, DA)
    return (y + bonus).astype(cur.dtype), s_fin.astype(s0.dtype)


def attn_heads(qb, kb, vb, qi, ki, pos):
    B, T = qb.shape[:2]
    q = rope(qb.reshape(B, T, HB, DHB), pos)
    k = rope(kb.reshape(B, T, HB, DHB), pos)
    v = vb.reshape(B, T, HB, DHB)
    qi = rope(qi.reshape(B, T, HI, DI), pos)
    ki = rope(ki[:, :, None, :], pos)[:, :, 0]
    return q, k, v, qi, ki


def index_topk(qi, wi, ki, qpos, topk):
    s = jnp.einsum('bqhd,bsd->bqhs', qi, ki, preferred_element_type=jnp.float32)
    score = jnp.einsum('bqh,bqhs->bqs', wi.astype(jnp.float32), jax.nn.relu(s)) * (HI ** -0.5 * DI ** -0.5)
    kpos = jnp.arange(ki.shape[1])
    score = jnp.where(kpos[None, None, :] <= qpos[None, :, None], score, -jnp.inf)
    _, idx = lax.top_k(score, topk)
    valid = idx <= qpos[None, :, None]
    return idx, valid


def sparse_attend(q, ks, vs, valid):
    B, Q = q.shape[:2]
    s = jnp.einsum('bqhd,bqkhd->bqhk', q, ks, preferred_element_type=jnp.float32) * (DHB ** -0.5)
    s = jnp.where(valid[:, :, None, :], s, -jnp.inf)
    p = jax.nn.softmax(s, axis=-1).astype(vs.dtype)
    return jnp.einsum('bqhk,bqkhd->bqhd', p, vs).reshape(B, Q, DB)


def gather_rows(rows, idx):
    return jax.vmap(lambda r, i: r[i])(rows, idx)


def dsa_prompt(q, k, v, qi, wi, ki):
    B, T = q.shape[:2]
    topk = min(TOPK_MAX, T // 4)
    nb = T // Q_BLOCK
    to_blocks = lambda t: jnp.moveaxis(t.reshape(B, nb, Q_BLOCK, *t.shape[2:]), 1, 0)
    qpos_blocks = jnp.arange(T).reshape(nb, Q_BLOCK)

    def block(args):
        qb, qib, wib, qpos = args
        idx, valid = index_topk(qib, wib, ki, qpos, topk)
        return sparse_attend(qb, gather_rows(k, idx), gather_rows(v, idx), valid)

    out = lax.map(block, (to_blocks(q), to_blocks(qi), to_blocks(wi), qpos_blocks))
    return jnp.moveaxis(out, 0, 1).reshape(B, T, DB)


def dsa_sample(q, k_new, v_new, qi, wi, ki_new, cache_k, cache_v, cache_kidx, page_table):
    Bd, T = q.shape[:2]
    past = page_table.shape[1] * PAGE_SIZE
    topk = min(TOPK_MAX, (past + T) // 4)
    ki_past = cache_kidx[page_table].reshape(Bd, past, DI)
    ki_all = jnp.concatenate([ki_past, ki_new.astype(ki_past.dtype)], axis=1)
    qpos = past + jnp.arange(T)
    idx, valid = index_topk(qi, wi, ki_all, qpos, topk)
    is_past = idx < past
    pidx = jnp.minimum(idx, past - 1)
    phys = jax.vmap(lambda pt, i: pt[i])(page_table, pidx // PAGE_SIZE)
    slot = pidx % PAGE_SIZE
    nidx = jnp.clip(idx - past, 0, T - 1)
    sel = lambda pool, new: jnp.where(is_past[..., None, None], pool[phys, slot],
                                      gather_rows(new, nidx).astype(pool.dtype))
    return sparse_attend(q, sel(cache_k, k_new), sel(cache_v, v_new), valid)


def memory_kv(mem, g, w):
    B = mem.shape[0]
    mk, mv = jnp.split(rmsnorm(mem, g) @ w, 2, axis=-1)
    return mk.reshape(B, N_MEM, HC, DHC), mv.reshape(B, N_MEM, HC, DHC)


def cross_attend(qc, mk, mv):
    B, T = qc.shape[:2]
    q = qc.reshape(B, T, HC, DHC)
    s = jnp.einsum('bthd,bmhd->bhtm', q, mk, preferred_element_type=jnp.float32) * (DHC ** -0.5)
    p = jax.nn.softmax(s, axis=-1).astype(mv.dtype)
    return jnp.einsum('bhtm,bmhd->bthd', p, mv).reshape(B, T, DC)


def merge_branches(x, ya, za, yb, zb, yc, zc, gates, b_gate, w_br_a, w_br_b, w_br_c, w_out):
    ba = (ya * jax.nn.silu(za)) @ w_br_a
    bb = (yb * jax.nn.silu(zb)) @ w_br_b
    bc = (yc * jax.nn.silu(zc)) @ w_br_c
    ga, gb, gc = jnp.split(jax.nn.sigmoid(gates + b_gate), N_BRANCH, axis=-1)
    return x + (ga * ba + gb * bb + gc * bc) @ w_out


def prompt_layer(x, mem, norm_g, w_in, rw, mem_norm_g, w_mem_kv, br):
    B, T, _ = x.shape
    sh, za, qb, kb, vb, qi, wi, ki, zb, qc, zc, gates = split_cols(rmsnorm(x, norm_g) @ w_in)
    prev = jnp.concatenate([jnp.zeros_like(sh[:, :1]), sh[:, :-1]], axis=1)
    ya, wkv = rwkv7_mix(sh, prev, jnp.zeros((B, HA, DHA, DHA), x.dtype), *rw)
    q, k, v, qi, ki = attn_heads(qb, kb, vb, qi, ki, jnp.arange(T))
    yb = dsa_prompt(q, k, v, qi, wi, ki)
    mk, mv = memory_kv(mem, mem_norm_g, w_mem_kv)
    yc = cross_attend(qc, mk, mv)
    y = merge_branches(x, ya, za, yb, zb, yc, zc, gates, *br)
    return y, k, v, ki, mk, mv, wkv, sh[:, -1]


def sample_layer(x, cache_k, cache_v, cache_kidx, mem_k, mem_v, wkv0, shift0, page_table, norm_g, w_in, rw, br):
    T = x.shape[1]
    past = page_table.shape[1] * PAGE_SIZE
    sh, za, qb, kb, vb, qi, wi, ki, zb, qc, zc, gates = split_cols(rmsnorm(x, norm_g) @ w_in)
    prev = jnp.concatenate([shift0[:, None, :].astype(sh.dtype), sh[:, :-1]], axis=1)
    ya, wkv = rwkv7_mix(sh, prev, wkv0, *rw)
    q, k, v, qi, ki = attn_heads(qb, kb, vb, qi, ki, past + jnp.arange(T))
    yb = dsa_sample(q, k, v, qi, wi, ki, cache_k, cache_v, cache_kidx, page_table)
    yc = cross_attend(qc, mem_k, mem_v)
    y = merge_branches(x, ya, za, yb, zb, yc, zc, gates, *br)
    return y, k, v, ki, wkv, sh[:, -1]


def setup_inputs(seed: int = 0) -> dict:
    key = jax.random.key(seed)
    keys = iter(jax.random.split(key, 40))
    f32 = jnp.float32
    nrm = lambda shape, scale: jax.random.normal(next(keys), shape, f32) * scale
    L = DEPTH
    n_pages = PAST_LEN // PAGE_SIZE
    n_used = DEC_BATCH * n_pages
    n_pool = n_used + (n_used + 3) // 4
    page_table = jax.random.permutation(next(keys), n_pool)[:n_used].reshape(DEC_BATCH, n_pages).astype(jnp.int32)
    return {
        'x_prompt': nrm((BATCH, SEQ, D_MODEL), 1.0),
        'x_sample': nrm((DEC_BATCH, DEC_SEQ, D_MODEL), 1.0),
        'mem_prompt': nrm((BATCH, N_MEM, D_MODEL), 1.0),
        'cache_k': nrm((L, n_pool, PAGE_SIZE, HB, DHB), 1.0),
        'cache_v': nrm((L, n_pool, PAGE_SIZE, HB, DHB), 1.0),
        'cache_kidx': nrm((L, n_pool, PAGE_SIZE, DI), 1.0),
        'cache_mem_k': nrm((L, DEC_BATCH, N_MEM, HC, DHC), 1.0),
        'cache_mem_v': nrm((L, DEC_BATCH, N_MEM, HC, DHC), 1.0),
        'state_wkv': nrm((L, DEC_BATCH, HA, DHA, DHA), 0.5),
        'state_shift': nrm((L, DEC_BATCH, SHIFT_W), 1.0),
        'page_table': page_table,
        'norm_g': 1.0 + nrm((L, D_MODEL), 0.02),
        'w_in': nrm((L, D_MODEL, N_IN), D_MODEL ** -0.5),
        'rwkv_mu': jax.random.uniform(next(keys), (L, SHIFT_W), f32),
        'rwkv_w0': jax.random.uniform(next(keys), (L, DA), f32, -4.0, 1.0),
        'rwkv_w_up': nrm((L, LORA_W, DA), 0.1),
        'rwkv_a0': nrm((L, DA), 0.1),
        'rwkv_a_up': nrm((L, LORA_A, DA), 0.1),
        'rwkv_k_k': 0.85 + nrm((L, DA), 0.02),
        'rwkv_k_a': 1.0 + nrm((L, DA), 0.02),
        'rwkv_r_k': nrm((L, HA, DHA), 0.1),
        'rwkv_lnx_g': 1.0 + nrm((L, DA), 0.02),
        'rwkv_lnx_b': nrm((L, DA), 0.01),
        'mem_norm_g': 1.0 + nrm((L, D_MODEL), 0.02),
        'w_mem_kv': nrm((L, D_MODEL, 2 * DC), D_MODEL ** -0.5),
        'b_gate': nrm((L, N_BRANCH * D_MODEL), 0.01),
        'w_br_a': nrm((L, DA, D_MODEL), DA ** -0.5),
        'w_br_b': nrm((L, DB, D_MODEL), DB ** -0.5),
        'w_br_c': nrm((L, DC, D_MODEL), DC ** -0.5),
        'w_out': nrm((L, D_MODEL, D_MODEL), D_MODEL ** -0.5),
        'final_norm_g': 1.0 + nrm((D_MODEL,), 0.02),
    }


def reference(x_prompt, x_sample, mem_prompt, cache_k, cache_v, cache_kidx, cache_mem_k, cache_mem_v,
              state_wkv, state_shift, page_table, norm_g, w_in, rwkv_mu, rwkv_w0, rwkv_w_up, rwkv_a0,
              rwkv_a_up, rwkv_k_k, rwkv_k_a, rwkv_r_k, rwkv_lnx_g, rwkv_lnx_b, mem_norm_g, w_mem_kv,
              b_gate, w_br_a, w_br_b, w_br_c, w_out, final_norm_g):
    xp, xs = x_prompt, x_sample
    acc_p = [[] for _ in range(7)]
    acc_s = [[] for _ in range(5)]
    for l in range(DEPTH):
        rw = (rwkv_mu[l], rwkv_w0[l], rwkv_w_up[l], rwkv_a0[l], rwkv_a_up[l], rwkv_k_k[l],
              rwkv_k_a[l], rwkv_r_k[l], rwkv_lnx_g[l], rwkv_lnx_b[l])
        br = (b_gate[l], w_br_a[l], w_br_b[l], w_br_c[l], w_out[l])
        xp, *new_p = prompt_layer(xp, mem_prompt, norm_g[l], w_in[l], rw, mem_norm_g[l], w_mem_kv[l], br)
        xs, *new_s = sample_layer(xs, cache_k[l], cache_v[l], cache_kidx[l], cache_mem_k[l], cache_mem_v[l],
                                  state_wkv[l], state_shift[l], page_table, norm_g[l], w_in[l], rw, br)
        for a, t in zip(acc_p, new_p):
            a.append(t)
        for a, t in zip(acc_s, new_s):
            a.append(t)
    y_prompt = rmsnorm(xp, final_norm_g)
    y_sample = rmsnorm(xs, final_norm_g)
    return (y_prompt, y_sample,
            jnp.stack(acc_p[0]), jnp.stack(acc_p[1]), jnp.stack(acc_p[2]), jnp.stack(acc_p[3]),
            jnp.stack(acc_p[4]), jnp.stack(acc_p[5]), jnp.stack(acc_p[6]),
            jnp.stack(acc_s[0]), jnp.stack(acc_s[1]), jnp.stack(acc_s[2]), jnp.stack(acc_s[3]),
            jnp.stack(acc_s[4]))
```

```python
import functools

import numpy as np
import jax
import jax.numpy as jnp
from jax import lax
from jax.experimental import pallas as pl
from jax.experimental.pallas import tpu as pltpu

F32, BF16, I32 = jnp.float32, jnp.bfloat16, jnp.int32

D_MODEL = 2048
HA, DHA = 12, 64
DA = HA * DHA
LORA = 64
LNX_EPS = 64e-5
HB, DHB = 6, 128
DB = HB * DHB
HI, DI = 8, 64
TOPK_MAX = 256
N_MEM = 256
HC, DHC = 4, 128
DC = HC * DHC
ROPE_THETA = 10000.0
EPS = 1e-6
PAGE = 128
SHIFT_W = 3 * DA + 2 * LORA
SPLIT_SIZES = (SHIFT_W, DA, DB, DB, DB, HI * DI, HI, DI, DB, DC, DC, 3 * D_MODEL)

LANES = 128
SUBLANES = 8
N_PAIR = HA // 2

C_RKV, C_ZA, C_QB, C_KB, C_VB, C_ZB = 0, 2304, 3072, 3840, 4608, 5376
C_QI, C_QC, C_ZC, C_WDAD, C_KI, C_WI, C_GATES = 6144, 6656, 7168, 7680, 7808, 7936, 8192
N_P = C_GATES + 3 * D_MODEL

NEG_BIAS = -1e30
KEY_NEG_INF = -2139095041
INT_MIN = -2147483648
VMEM_LIMIT = 56 * 1024 * 1024


def _cparams(*sem):
    return pltpu.CompilerParams(dimension_semantics=sem, vmem_limit_bytes=VMEM_LIMIT)


def _dot(a, b):
    return jnp.dot(a, b, preferred_element_type=F32)


def _dot_nt(a, b):
    return lax.dot_general(a, b, (((1,), (1,)), ((), ())), preferred_element_type=F32)


def _split2(x):
    hi = x.astype(BF16)
    lo = (x - hi.astype(F32)).astype(BF16)
    return hi, lo


def _dot_ones(x, ones_bf):
    hi = x.astype(BF16)
    r1 = x - hi.astype(F32)
    mid = r1.astype(BF16)
    lo = (r1 - mid.astype(F32)).astype(BF16)
    return _dot(hi, ones_bf) + _dot(mid, ones_bf) + _dot(lo, ones_bf)


def _dot3(a, b):
    a_hi, a_lo = _split2(a)
    b_hi, b_lo = _split2(b)
    return _dot(a_hi, b_hi) + _dot(a_hi, b_lo) + _dot(a_lo, b_hi)


def _sigmoid(x):
    return 1.0 / (1.0 + jnp.exp(-x))


def _silu(x):
    return x * _sigmoid(x)


def _proj_kernel(x_ref, g_ref, w_ref, o_ref, xn_ref):
    @pl.when(pl.program_id(1) == 0)
    def _():
        x = x_ref[...]
        ms = jnp.mean(x * x, axis=-1, keepdims=True)
        xn_ref[...] = (x * lax.rsqrt(ms + EPS) * g_ref[...]).astype(BF16)

    o_ref[...] = _dot(xn_ref[...], w_ref[...])


def _proj(x2d, g, w_bf, tm, tn):
    m, d = x2d.shape
    n = w_bf.shape[1]
    return pl.pallas_call(
        _proj_kernel,
        grid=(m // tm, n // tn),
        in_specs=[pl.BlockSpec((tm, d), lambda i, j: (i, 0)),
                  pl.BlockSpec((1, d), lambda i, j: (0, 0)),
                  pl.BlockSpec((d, tn), lambda i, j: (0, j))],
        out_specs=pl.BlockSpec((tm, tn), lambda i, j: (i, j)),
        out_shape=jax.ShapeDtypeStruct((m, n), F32),
        scratch_shapes=[pltpu.VMEM((tm, d), BF16)],
        compiler_params=_cparams("parallel", "arbitrary"),
    )(x2d, g.reshape(1, d), w_bf)


def _permute_w_in(w):
    o = [int(v) for v in np.cumsum((0,) + SPLIT_SIZES)]
    za0, qb0, kb0, vb0, qi0, wi0, ki0, zb0, qc0, zc0, g0 = o[1:12]
    z = lambda n: jnp.zeros((w.shape[0], n), w.dtype)
    parts = [w[:, 0:3 * DA], w[:, za0:za0 + DA], w[:, qb0:qb0 + DB], w[:, kb0:kb0 + DB], w[:, vb0:vb0 + DB],
             w[:, zb0:zb0 + DB], w[:, qi0:qi0 + HI * DI], w[:, qc0:qc0 + DC], w[:, zc0:zc0 + DC],
             w[:, 3 * DA:SHIFT_W], w[:, ki0:ki0 + DI], z(LANES - DI), w[:, wi0:wi0 + HI], z(LANES - HI), z(LANES),
             w[:, g0:g0 + 3 * D_MODEL]]
    out = jnp.concatenate(parts, axis=1).astype(BF16)
    assert out.shape[1] == N_P
    return out


def _prep_kernel(carry_mode, period, rkv_ref, wdad_ref, pinit_ref, mu_ref, w0_ref, wup_ref, a0_ref, aup_ref,
                 kk_ref, ka_ref, ones_ref, r_o, w_o, k_o, v_o, kn_o, b_o, carry_ref):
    tm = rkv_ref.shape[0]
    row = lax.broadcasted_iota(I32, (tm, 1), 0)
    cur = jnp.concatenate([rkv_ref[...], wdad_ref[...]], axis=1)
    rolled = pltpu.roll(cur, 1, 0)
    if carry_mode:
        @pl.when(pl.program_id(0) == 0)
        def _():
            carry_ref[...] = pinit_ref[...]
        prev = jnp.where(row == 0, carry_ref[...], rolled)
        carry_ref[...] = cur[tm - 1:tm, :]
    else:
        prev = jnp.where((row & (period - 1)) == 0, pinit_ref[...], rolled)
    m = cur + mu_ref[...] * (prev - cur)
    r = m[:, 0:DA]
    k = m[:, DA:2 * DA]
    v = m[:, 2 * DA:3 * DA]
    wd = m[:, 3 * DA:3 * DA + LORA]
    ad = m[:, 3 * DA + LORA:SHIFT_W]
    lw = w0_ref[...] + _dot3(jnp.tanh(wd), wup_ref[...])
    z = -lw
    softplus = jnp.maximum(z, 0.0) + jnp.log(1.0 + jnp.exp(-jnp.abs(z)))
    logw = -softplus - 0.5
    decay = jnp.exp(-jnp.exp(logw))
    a = _sigmoid(a0_ref[...] + _dot3(ad, aup_ref[...]))
    kk = k * kk_ref[...]
    ss = _dot_ones(kk * kk, ones_ref[...])
    kn = kk * lax.rsqrt(ss + 1e-12)
    r_o[...] = r
    w_o[...] = decay
    k_o[...] = k * (1.0 + (a - 1.0) * ka_ref[...])
    v_o[...] = v
    kn_o[...] = kn
    b_o[...] = kn * a


def _rwkv_prep(p, pinit, rw, ones_da, tm, carry_mode, period=1):
    t = p.shape[0]
    mu, w0, w_up, a0, a_up, k_k, k_a = rw[:7]
    row = lambda v: v.reshape(1, -1)
    full = lambda shape: pl.BlockSpec(shape, lambda i: (0, 0))
    pin_spec = full((1, SHIFT_W)) if carry_mode else pl.BlockSpec((tm, SHIFT_W), lambda i: (i, 0))
    out = jax.ShapeDtypeStruct((t, DA), F32)
    return pl.pallas_call(
        functools.partial(_prep_kernel, carry_mode, period),
        grid=(t // tm,),
        in_specs=[pl.BlockSpec((tm, 3 * DA), lambda i: (i, 0)),
                  pl.BlockSpec((tm, LANES), lambda i: (i, C_WDAD // LANES)),
                  pin_spec, full((1, SHIFT_W)), full((1, DA)), full((LORA, DA)), full((1, DA)), full((LORA, DA)),
                  full((1, DA)), full((1, DA)), full((DA, DA))],
        out_specs=[pl.BlockSpec((tm, DA), lambda i: (i, 0))] * 6,
        out_shape=[out] * 6,
        scratch_shapes=[pltpu.VMEM((1, SHIFT_W), F32)],
        compiler_params=_cparams("arbitrary"),
    )(p, p, pinit, row(mu), row(w0), w_up, row(a0), a_up, row(k_k), row(k_a), ones_da)


def _seg_sum_bcast(x, lo_lanes):
    lo = jnp.sum(jnp.where(lo_lanes, x, 0.0), axis=1, keepdims=True)
    hi = jnp.sum(jnp.where(lo_lanes, 0.0, x), axis=1, keepdims=True)
    return jnp.where(lo_lanes, lo, hi)


def _scan_kernel(n_tok, r_ref, w_ref, k_ref, v_ref, kn_ref, b_ref, s0_ref, rk_ref, g_ref, beta_ref, ones_ref,
                 y_o, s_o, s_ref, y_ref):
    c = pl.program_id(1)

    @pl.when(c == 0)
    def _():
        s_ref[...] = s0_ref[...]

    lane = lax.broadcasted_iota(I32, (DHA, LANES), 1)
    sub = lax.broadcasted_iota(I32, (DHA, LANES), 0)
    lo_lanes = lane < DHA
    diag = (lane & (DHA - 1)) == sub
    def group(base, n):
        rows8 = pl.ds(base, SUBLANES)
        for p in range(N_PAIR):
            sl = slice(p * LANES, (p + 1) * LANES)
            kn8, v8, w8, b8, k8, r8 = (ref[rows8, sl] for ref in (kn_ref, v_ref, w_ref, b_ref, k_ref, r_ref))
            s = s_ref[p]
            y_rows = []
            for u in range(n):
                row = slice(u, u + 1)
                sa = _seg_sum_bcast(s * kn8[row], lo_lanes)
                vb = _seg_sum_bcast(jnp.where(diag, v8[row], 0.0), lo_lanes)
                s = s * w8[row] - sa * b8[row] + vb * k8[row]
                yb = _seg_sum_bcast(s * r8[row], lo_lanes)
                y_rows.append(jnp.sum(jnp.where(diag, yb, 0.0), axis=0, keepdims=True))
            y_rows += [jnp.zeros((1, LANES), F32)] * (SUBLANES - n)
            y_ref[rows8, sl] = jnp.concatenate(y_rows, axis=0)
            s_ref[p] = s

    n_full = n_tok // SUBLANES

    def full_group(g, carry):
        group(pl.multiple_of(g * SUBLANES, SUBLANES), SUBLANES)
        return carry

    lax.fori_loop(0, n_full, full_group, 0)
    if n_tok % SUBLANES:
        group(n_full * SUBLANES, n_tok % SUBLANES)

    ones = ones_ref[...]
    y = y_ref[...]
    mean = _dot_ones(y, ones) * (1.0 / DHA)
    yc = y - mean
    var = _dot_ones(yc * yc, ones) * (1.0 / DHA)
    yn = yc * lax.rsqrt(var + LNX_EPS) * g_ref[...] + beta_ref[...]
    bonus = _dot_ones(r_ref[...] * k_ref[...] * rk_ref[...], ones) * v_ref[...]
    y_o[...] = yn + bonus

    @pl.when(c == pl.num_programs(1) - 1)
    def _():
        s_o[...] = s_ref[...]


def _pack_state(s):
    b = s.shape[0]
    return s.reshape(b, N_PAIR, 2, DHA, DHA).transpose(0, 1, 3, 2, 4).reshape(b, N_PAIR, DHA, LANES)


def _unpack_state(s):
    b = s.shape[0]
    return s.reshape(b, N_PAIR, DHA, 2, DHA).transpose(0, 1, 3, 2, 4).reshape(b, HA, DHA, DHA)


def _rwkv_scan(seqs, s0, rw, ones_da, tc, n_tok):
    bsz, t, _ = seqs[0].shape
    assert tc == -(-n_tok // SUBLANES) * SUBLANES
    r_k, lnx_g, lnx_b = rw[7:10]
    row = lambda v: v.reshape(1, DA)
    seq_spec = pl.BlockSpec((None, tc, DA), lambda b, c: (b, c, 0))
    st_spec = pl.BlockSpec((None, N_PAIR, DHA, LANES), lambda b, c: (b, 0, 0, 0))
    full = lambda shape: pl.BlockSpec(shape, lambda b, c: (0, 0))
    y, s_fin = pl.pallas_call(
        functools.partial(_scan_kernel, n_tok),
        grid=(bsz, t // tc),
        in_specs=[seq_spec] * 6 + [st_spec, full((1, DA)), full((1, DA)), full((1, DA)), full((DA, DA))],
        out_specs=[seq_spec, st_spec],
        out_shape=[jax.ShapeDtypeStruct((bsz, t, DA), F32), jax.ShapeDtypeStruct((bsz, N_PAIR, DHA, LANES), F32)],
        scratch_shapes=[pltpu.VMEM((N_PAIR, DHA, LANES), F32), pltpu.VMEM((tc, DA), F32)],
        compiler_params=_cparams("parallel", "arbitrary"),
    )(*seqs, _pack_state(s0), row(r_k), row(lnx_g), row(lnx_b), ones_da)
    return y, _unpack_state(s_fin)


def _rope_kernel(qb_ref, kb_ref, vb_ref, qi_ref, ki_ref, c128_ref, s128_ref, c64_ref, s64_ref,
                 q_o, k_o, kbf_o, vbf_o, qi_o, ki_o, kihi_o, kilo_o):
    c128, s128, c64, s64 = c128_ref[...], s128_ref[...], c64_ref[...], s64_ref[...]
    lane = lax.broadcasted_iota(I32, c64.shape, 1)
    first_half = (lane & (DI - 1)) < DI // 2

    def rope128(x):
        return x * c128 + pltpu.roll(x, DHB // 2, 1) * s128

    def rope64(x):
        partner = jnp.where(first_half, pltpu.roll(x, LANES - DI // 2, 1), pltpu.roll(x, DI // 2, 1))
        return x * c64 + partner * s64

    for h in range(HB):
        sl = slice(h * DHB, (h + 1) * DHB)
        q_o[:, sl] = rope128(qb_ref[:, sl]).astype(BF16)
        kr = rope128(kb_ref[:, sl])
        k_o[:, sl] = kr
        kbf_o[:, sl] = kr.astype(BF16)
    vbf_o[...] = vb_ref[...].astype(BF16)
    for h2 in range(HI * DI // LANES):
        sl = slice(h2 * LANES, (h2 + 1) * LANES)
        qi_o[:, sl] = rope64(qi_ref[:, sl])
    ki = rope64(ki_ref[...])[:, :DI]
    ki_o[...] = ki
    hi, lo = _split2(ki)
    kihi_o[...] = hi
    kilo_o[...] = lo


def _rope_tables(pos, d):
    inv = ROPE_THETA ** (-jnp.arange(0, d, 2, dtype=F32) / d)
    ang = pos.astype(F32)[:, None] * inv[None, :]
    cos, sin = jnp.cos(ang), jnp.sin(ang)
    reps = LANES // d
    c = jnp.tile(jnp.concatenate([cos, cos], axis=1), (1, reps))
    s = jnp.tile(jnp.concatenate([-sin, sin], axis=1), (1, reps))
    return c, s


def _rope(p, pos, tm):
    t = p.shape[0]
    c128, s128 = _rope_tables(pos, DHB)
    c64, s64 = _rope_tables(pos, DI)
    col = lambda w, c0: pl.BlockSpec((tm, w), lambda i: (i, c0 // w))
    rows = lambda w: pl.BlockSpec((tm, w), lambda i: (i, 0))
    sds = lambda w, dt: jax.ShapeDtypeStruct((t, w), dt)
    return pl.pallas_call(
        _rope_kernel,
        grid=(t // tm,),
        in_specs=[col(DB, C_QB), col(DB, C_KB), col(DB, C_VB), col(HI * DI, C_QI), col(LANES, C_KI),
                  rows(LANES), rows(LANES), rows(LANES), rows(LANES)],
        out_specs=[rows(DB), rows(DB), rows(DB), rows(DB), rows(HI * DI), rows(DI), rows(DI), rows(DI)],
        out_shape=[sds(DB, BF16), sds(DB, F32), sds(DB, BF16), sds(DB, BF16), sds(HI * DI, F32), sds(DI, F32),
                   sds(DI, BF16), sds(DI, BF16)],
        compiler_params=_cparams("parallel"),
    )(p, p, p, p, p, c128, s128, c64, s64)


def _sort_key(score):
    score = jnp.where(score == 0.0, 0.0, score)
    b = lax.bitcast_convert_type(score, I32)
    return b ^ ((b >> 31) & 0x7FFFFFFF)


def _index_scores(qi_hi, qi_lo, wi, k_hi, k_lo):
    acc = None
    for h in range(HI):
        sl = slice(h * DI, (h + 1) * DI)
        s = _dot_nt(qi_hi[:, sl], k_hi) + _dot_nt(qi_hi[:, sl], k_lo) + _dot_nt(qi_lo[:, sl], k_hi)
        term = wi[:, h:h + 1] * jnp.maximum(s, 0.0)
        acc = term if acc is None else acc + term
    return acc * (HI ** -0.5 * DI ** -0.5)


def _count(skey_ref, nc, ck, pred):
    rows = skey_ref.shape[0]

    def body(c, acc):
        base = pl.multiple_of(c * ck, ck)
        for u in range(ck // LANES):
            blk = skey_ref[:, pl.ds(base + u * LANES, LANES)]
            acc = acc + jnp.where(pred(blk, base + u * LANES), 1, 0)
        return acc

    acc = lax.fori_loop(0, nc, body, jnp.zeros((rows, LANES), I32))
    return jnp.sum(acc, axis=1, keepdims=True)


def _select_bias(skey_ref, bias_ref, cut_ref, nc, nc_total, ck, topk, idx_bits):
    rows = skey_ref.shape[0]
    lane = lax.broadcasted_iota(I32, (rows, LANES), 1)

    def bit_body(it, ans_u):
        cand_u = ans_u | lax.shift_left(jnp.int32(1), 31 - it)
        cand_s = cand_u ^ INT_MIN
        cnt = _count(skey_ref, nc, ck, lambda blk, _: blk >= cand_s)
        return jnp.where(cnt >= topk, cand_u, ans_u)

    thr = lax.fori_loop(0, 32, bit_body, jnp.zeros((rows, 1), I32)) ^ INT_MIN
    n_gt = _count(skey_ref, nc, ck, lambda blk, _: blk > thr)
    n_ge = _count(skey_ref, nc, ck, lambda blk, _: blk >= thr)
    need = topk - n_gt
    tied = ((n_ge - n_gt) > need) & (thr > KEY_NEG_INF)
    cut_ref[...] = jnp.full((rows, 1), 2 ** 30, I32)

    @pl.when(jnp.max(tied.astype(I32)) > 0)
    def _():
        def tie_body(it, ans):
            cand = ans | lax.shift_left(jnp.int32(1), idx_bits - 1 - it)
            g = _count(skey_ref, nc, ck, lambda blk, base: (blk == thr) & ((lane + base) < cand))
            return jnp.where(g < need, cand, ans)

        cut = lax.fori_loop(0, idx_bits, tie_body, jnp.zeros((rows, 1), I32))
        cut_ref[...] = jnp.where(tied, cut, 2 ** 30)

    cut = cut_ref[...]

    def write(c, carry):
        base = pl.multiple_of(c * ck, ck)
        for u in range(ck // LANES):
            off = base + u * LANES
            blk = skey_ref[:, pl.ds(off, LANES)]
            sel = ((blk > thr) | ((blk == thr) & ((lane + off) <= cut))) & (blk > KEY_NEG_INF)
            bias_ref[:, pl.ds(off, LANES)] = jnp.where(sel, 0.0, NEG_BIAS).astype(bias_ref.dtype)
        return carry

    lax.fori_loop(0, nc, write, 0)

    def fill(c, carry):
        base = pl.multiple_of(c * ck, ck)
        bias_ref[:, pl.ds(base, ck)] = jnp.full((rows, ck), NEG_BIAS, bias_ref.dtype)
        return carry

    lax.fori_loop(nc, nc_total, fill, 0)


def _prompt_select_kernel(topk, ck, idx_bits, qi_ref, wi_ref, kihi_ref, kilo_ref, bias_o, skey_ref, cut_ref):
    i = pl.program_id(0)
    tq = qi_ref.shape[0]
    t = kihi_ref.shape[0]
    nc = ((i + 1) * tq + ck - 1) // ck
    qi_hi, qi_lo = _split2(qi_ref[...])
    wi = wi_ref[...]
    qpos = i * tq + lax.broadcasted_iota(I32, (tq, ck), 0)
    kiota = lax.broadcasted_iota(I32, (tq, ck), 1)

    def score_chunk(c, carry):
        base = pl.multiple_of(c * ck, ck)
        sc = _index_scores(qi_hi, qi_lo, wi, kihi_ref[pl.ds(base, ck), :], kilo_ref[pl.ds(base, ck), :])
        sc = jnp.where(kiota + base <= qpos, sc, -jnp.inf)
        skey_ref[:, pl.ds(base, ck)] = _sort_key(sc)
        return carry

    lax.fori_loop(0, nc, score_chunk, 0)
    _select_bias(skey_ref, bias_o, cut_ref, nc, t // ck, ck, topk, idx_bits)


def _prompt_select(qi, p, ki_hi, ki_lo, tq, ck):
    t = qi.shape[0]
    topk = min(TOPK_MAX, t // 4)
    idx_bits = int(np.ceil(np.log2(t))) + 1
    return pl.pallas_call(
        functools.partial(_prompt_select_kernel, topk, ck, idx_bits),
        grid=(t // tq,),
        in_specs=[pl.BlockSpec((tq, HI * DI), lambda i: (i, 0)),
                  pl.BlockSpec((tq, LANES), lambda i: (i, C_WI // LANES)),
                  pl.BlockSpec((t, DI), lambda i: (0, 0)),
                  pl.BlockSpec((t, DI), lambda i: (0, 0))],
        out_specs=pl.BlockSpec((tq, t), lambda i: (i, 0)),
        out_shape=jax.ShapeDtypeStruct((t, t), BF16),
        scratch_shapes=[pltpu.VMEM((tq, t), I32), pltpu.VMEM((tq, 1), I32)],
        compiler_params=_cparams("parallel"),
    )(qi, p, ki_hi, ki_lo)


def _sample_score_kernel(n_pages, n_new, pt_ref, qi_ref, wi_ref, page_ref, knew_ref, skey_o):
    j = pl.program_id(1)
    kc = jnp.where(j == n_pages, knew_ref[...], page_ref[...])
    k_hi, k_lo = _split2(kc)
    qi_hi, qi_lo = _split2(qi_ref[...])
    sc = _index_scores(qi_hi, qi_lo, wi_ref[...], k_hi, k_lo)
    rows = sc.shape[0]
    trow = lax.broadcasted_iota(I32, (rows, LANES), 0)
    lane = lax.broadcasted_iota(I32, (rows, LANES), 1)
    valid = (trow < n_new) & ((j < n_pages) | (lane <= trow))
    skey_o[...] = _sort_key(jnp.where(valid, sc, -jnp.inf))


def _sample_scores(qi8, wi8, cache_kidx, knew_pad, page_table, n_new):
    bsz, n_pages = page_table.shape
    rows = qi8.shape[1]
    grid_spec = pltpu.PrefetchScalarGridSpec(
        num_scalar_prefetch=1,
        grid=(bsz, n_pages + 1),
        in_specs=[pl.BlockSpec((None, rows, HI * DI), lambda b, j, pt: (b, 0, 0)),
                  pl.BlockSpec((None, rows, LANES), lambda b, j, pt: (b, 0, 0)),
                  pl.BlockSpec((None, PAGE, DI), lambda b, j, pt: (pt[b, jnp.minimum(j, n_pages - 1)], 0, 0)),
                  pl.BlockSpec((None, PAGE, DI), lambda b, j, pt: (b, 0, 0))],
        out_specs=pl.BlockSpec((None, rows, LANES), lambda b, j, pt: (b, 0, j)),
    )
    return pl.pallas_call(
        functools.partial(_sample_score_kernel, n_pages, n_new),
        grid_spec=grid_spec,
        out_shape=jax.ShapeDtypeStruct((bsz, rows, (n_pages + 1) * PAGE), I32),
        compiler_params=_cparams("parallel", "arbitrary"),
    )(page_table, qi8, wi8, cache_kidx, knew_pad)


def _sample_select_kernel(topk, ck, idx_bits, skey_in, bias_o, skey_ref, cut_ref):
    skey_ref[...] = skey_in[...]
    nc = skey_in.shape[1] // ck
    _select_bias(skey_ref, bias_o, cut_ref, nc, nc, ck, topk, idx_bits)


def _sample_select(skey, topk, tr):
    rows, n = skey.shape
    idx_bits = int(np.ceil(np.log2(n))) + 1
    return pl.pallas_call(
        functools.partial(_sample_select_kernel, topk, PAGE, idx_bits),
        grid=(rows // tr,),
        in_specs=[pl.BlockSpec((tr, n), lambda i: (i, 0))],
        out_specs=pl.BlockSpec((tr, n), lambda i: (i, 0)),
        out_shape=jax.ShapeDtypeStruct((rows, n), F32),
        scratch_shapes=[pltpu.VMEM((tr, n), I32), pltpu.VMEM((tr, 1), I32)],
        compiler_params=_cparams("parallel"),
    )(skey)


def _prompt_attn_kernel(ck, q_ref, k_ref, v_ref, bias_ref, o_ref, m_ref, l_ref, acc_ref):
    i = pl.program_id(0)
    tq = q_ref.shape[0]
    nc = ((i + 1) * tq + ck - 1) // ck
    m_ref[...] = jnp.full(m_ref.shape, NEG_BIAS, F32)
    l_ref[...] = jnp.zeros(l_ref.shape, F32)
    acc_ref[...] = jnp.zeros(acc_ref.shape, F32)
    scale = DHB ** -0.5

    def chunk(c, carry):
        base = pl.multiple_of(c * ck, ck)
        bias = bias_ref[:, pl.ds(base, ck)].astype(F32)
        for h in range(HB):
            sl = slice(h * DHB, (h + 1) * DHB)
            s = _dot_nt(q_ref[:, sl], k_ref[pl.ds(base, ck), sl]) * scale + bias
            m_old = m_ref[h]
            m_new = jnp.maximum(m_old, jnp.max(s, axis=1, keepdims=True))
            alpha = jnp.exp(m_old - m_new)
            p = jnp.exp(s - m_new)
            l_ref[h] = alpha * l_ref[h] + jnp.sum(p, axis=1, keepdims=True)
            acc_ref[h] = alpha * acc_ref[h] + _dot(p.astype(BF16), v_ref[pl.ds(base, ck), sl])
            m_ref[h] = m_new
        return carry

    lax.fori_loop(0, nc, chunk, 0)
    for h in range(HB):
        o_ref[:, h * DHB:(h + 1) * DHB] = acc_ref[h] / l_ref[h]


def _prompt_attn(q_bf, k_bf, v_bf, bias, tq, ck):
    t = q_bf.shape[0]
    resident = lambda: pl.BlockSpec((t, DB), lambda i: (0, 0), pipeline_mode=pl.Buffered(1))
    return pl.pallas_call(
        functools.partial(_prompt_attn_kernel, ck),
        grid=(t // tq,),
        in_specs=[pl.BlockSpec((tq, DB), lambda i: (i, 0)), resident(), resident(),
                  pl.BlockSpec((tq, t), lambda i: (i, 0))],
        out_specs=pl.BlockSpec((tq, DB), lambda i: (i, 0)),
        out_shape=jax.ShapeDtypeStruct((t, DB), F32),
        scratch_shapes=[pltpu.VMEM((HB, tq, 1), F32), pltpu.VMEM((HB, tq, 1), F32), pltpu.VMEM((HB, tq, DHB), F32)],
        compiler_params=_cparams("parallel"),
    )(q_bf, k_bf, v_bf, bias)


def _sample_attn_kernel(n_pages, rq, pt_ref, qbd_ref, kpage_ref, vpage_ref, knew_ref, vnew_ref, bias_ref, o_ref,
                        m_ref, l_ref, acc_ref):
    j = pl.program_id(1)
    nr = HB * rq

    @pl.when(j == 0)
    def _():
        m_ref[...] = jnp.full(m_ref.shape, NEG_BIAS, F32)
        l_ref[...] = jnp.zeros(l_ref.shape, F32)
        acc_ref[...] = jnp.zeros(acc_ref.shape, F32)

    is_new = j == n_pages
    kp = jnp.where(is_new, knew_ref[...], kpage_ref[...]).astype(BF16)
    vp = jnp.where(is_new, vnew_ref[...], vpage_ref[...]).astype(BF16)
    st = _dot(kp, qbd_ref[...])
    s = st.T[:nr] * (DHB ** -0.5) + jnp.concatenate([bias_ref[...]] * HB, axis=0)
    m_old = m_ref[...]
    m_new = jnp.maximum(m_old, jnp.max(s, axis=1, keepdims=True))
    alpha = jnp.exp(m_old - m_new)
    p = jnp.exp(s - m_new)
    l_ref[...] = alpha * l_ref[...] + jnp.sum(p, axis=1, keepdims=True)
    acc_ref[...] = alpha * acc_ref[...] + _dot(p.astype(BF16), vp)
    m_ref[...] = m_new

    @pl.when(is_new)
    def _():
        for h in range(HB):
            rs = slice(h * rq, (h + 1) * rq)
            cs = slice(h * DHB, (h + 1) * DHB)
            o_ref[:, cs] = acc_ref[rs, cs] / l_ref[rs, :]


def _sample_attn(qbd, cache_k, cache_v, knew_pad, vnew_pad, bias, page_table, rq):
    bsz, n_pages = page_table.shape
    page_map = lambda b, j, pt: (pt[b, jnp.minimum(j, n_pages - 1)], 0, 0)
    per_b = lambda b, j, pt: (b, 0, 0)
    grid_spec = pltpu.PrefetchScalarGridSpec(
        num_scalar_prefetch=1,
        grid=(bsz, n_pages + 1),
        in_specs=[pl.BlockSpec((None, DB, LANES), per_b),
                  pl.BlockSpec((None, PAGE, DB), page_map),
                  pl.BlockSpec((None, PAGE, DB), page_map),
                  pl.BlockSpec((None, PAGE, DB), per_b),
                  pl.BlockSpec((None, PAGE, DB), per_b),
                  pl.BlockSpec((None, rq, LANES), lambda b, j, pt: (b, 0, j))],
        out_specs=pl.BlockSpec((None, rq, DB), per_b),
        scratch_shapes=[pltpu.VMEM((HB * rq, 1), F32), pltpu.VMEM((HB * rq, 1), F32), pltpu.VMEM((HB * rq, DB), F32)],
    )
    return pl.pallas_call(
        functools.partial(_sample_attn_kernel, n_pages, rq),
        grid_spec=grid_spec,
        out_shape=jax.ShapeDtypeStruct((bsz, rq, DB), F32),
        compiler_params=_cparams("parallel", "arbitrary"),
    )(page_table, qbd, cache_k, cache_v, knew_pad, vnew_pad, bias)


def _cross_kernel(q_ref, mk_ref, mv_ref, o_ref):
    scale = DHC ** -0.5
    for h in range(HC):
        sl = slice(h * DHC, (h + 1) * DHC)
        s = _dot_nt(q_ref[:, sl].astype(BF16), mk_ref[:, sl].astype(BF16)) * scale
        m = jnp.max(s, axis=1, keepdims=True)
        p = jnp.exp(s - m)
        l = jnp.sum(p, axis=1, keepdims=True)
        o_ref[:, sl] = _dot(p.astype(BF16), mv_ref[:, sl].astype(BF16)) / l


def _cross_attn(q_arr, q_col, mk, mv, tm):
    bsz, t, _ = q_arr.shape
    return pl.pallas_call(
        _cross_kernel,
        grid=(bsz, t // tm),
        in_specs=[pl.BlockSpec((None, tm, DC), lambda b, i: (b, i, q_col // DC)),
                  pl.BlockSpec((None, N_MEM, DC), lambda b, i: (b, 0, 0)),
                  pl.BlockSpec((None, N_MEM, DC), lambda b, i: (b, 0, 0))],
        out_specs=pl.BlockSpec((None, tm, DC), lambda b, i: (b, i, 0)),
        out_shape=jax.ShapeDtypeStruct((bsz, t, DC), F32),
        compiler_params=_cparams("parallel", "parallel"),
    )(q_arr, mk, mv)


def _merge_kernel(x_ref, ya_ref, za_ref, yb_ref, zb_ref, yc_ref, zc_ref, ga_ref, gb_ref, gc_ref,
                  bga_ref, bgb_ref, bgc_ref, wa_ref, wb_ref, wc_ref, wo_ref, fg_ref, o_ref):
    ba = _dot((ya_ref[...] * _silu(za_ref[...])).astype(BF16), wa_ref[...])
    bb = _dot((yb_ref[...] * _silu(zb_ref[...])).astype(BF16), wb_ref[...])
    bc = _dot((yc_ref[...] * _silu(zc_ref[...])).astype(BF16), wc_ref[...])
    mix = (_sigmoid(ga_ref[...] + bga_ref[...]) * ba + _sigmoid(gb_ref[...] + bgb_ref[...]) * bb
           + _sigmoid(gc_ref[...] + bgc_ref[...]) * bc)
    y = x_ref[...] + _dot(mix.astype(BF16), wo_ref[...])
    ms = jnp.mean(y * y, axis=-1, keepdims=True)
    o_ref[...] = y * lax.rsqrt(ms + EPS) * fg_ref[...]


def _merge(x2d, p, ya, yb, yc, b_gate, wa, wb, wc, wo, fg, tm):
    t, d = x2d.shape
    rows = lambda w: pl.BlockSpec((tm, w), lambda i: (i, 0))
    col = lambda w, c0: pl.BlockSpec((tm, w), lambda i: (i, c0 // w))
    const = lambda shape, c=0: pl.BlockSpec(shape, lambda i: (0, c), pipeline_mode=pl.Buffered(1))
    bg = b_gate.reshape(1, 3 * d)
    return pl.pallas_call(
        _merge_kernel,
        grid=(t // tm,),
        in_specs=[rows(d), rows(DA), col(DA, C_ZA), rows(DB), col(DB, C_ZB), rows(DC), col(DC, C_ZC),
                  col(d, C_GATES), col(d, C_GATES + d), col(d, C_GATES + 2 * d),
                  const((1, d), 0), const((1, d), 1), const((1, d), 2),
                  const((DA, d)), const((DB, d)), const((DC, d)), const((d, d)), const((1, d))],
        out_specs=rows(d),
        out_shape=jax.ShapeDtypeStruct((t, d), F32),
        compiler_params=_cparams("parallel"),
    )(x2d, ya, p, yb, p, yc, p, p, p, p, bg, bg, bg, wa, wb, wc, wo, fg.reshape(1, d))


def _block_ones(n, seg):
    idx = np.arange(n) // seg
    return jnp.asarray(idx[:, None] == idx[None, :], dtype=BF16)


def _shift_row(p_row):
    return jnp.concatenate([p_row[:3 * DA], p_row[C_WDAD:C_WDAD + 2 * LORA]])


def kernel(x_prompt, x_sample, mem_prompt, cache_k, cache_v, cache_kidx, cache_mem_k, cache_mem_v, state_wkv, state_shift, page_table, norm_g, w_in, rwkv_mu, rwkv_w0, rwkv_w_up, rwkv_a0, rwkv_a_up, rwkv_k_k, rwkv_k_a, rwkv_r_k, rwkv_lnx_g, rwkv_lnx_b, mem_norm_g, w_mem_kv, b_gate, w_br_a, w_br_b, w_br_c, w_out, final_norm_g):
    depth = w_in.shape[0]
    assert depth == 1 and x_prompt.shape[0] == 1
    t = x_prompt.shape[1]
    bs, ts, d = x_sample.shape
    n_pages = page_table.shape[1]
    past = n_pages * PAGE
    rq = 8
    assert ts <= rq and (ts & (ts - 1)) == 0

    rw = (rwkv_mu[0], rwkv_w0[0], rwkv_w_up[0], rwkv_a0[0], rwkv_a_up[0], rwkv_k_k[0], rwkv_k_a[0],
          rwkv_r_k[0].reshape(DA), rwkv_lnx_g[0], rwkv_lnx_b[0])
    ones_da = _block_ones(DA, DHA)
    w_p = _permute_w_in(w_in[0])
    wa, wb, wc, wo = (w[0].astype(BF16) for w in (w_br_a, w_br_b, w_br_c, w_out))

    xp2 = x_prompt.reshape(t, d)
    xs2 = x_sample.reshape(bs * ts, d)
    tm_p = min(512, t)
    p_p = _proj(xp2, norm_g[0], w_p, tm_p, 1024)
    p_s = _proj(xs2, norm_g[0], w_p, bs * ts, 1024)
    mkv = _proj(mem_prompt.reshape(N_MEM, d), mem_norm_g[0], w_mem_kv[0].astype(BF16), N_MEM, 2 * DC)
    mk_p, mv_p = mkv[:, :DC], mkv[:, DC:]

    seq_p = _rwkv_prep(p_p, jnp.zeros((1, SHIFT_W), F32), rw, ones_da, min(256, t), True)
    ya_p, wkv_p = _rwkv_scan([a[None] for a in seq_p], jnp.zeros((1, HA, DHA, DHA), F32), rw, ones_da,
                             min(256, t), min(256, t))
    pinit_s = jnp.repeat(state_shift[0], ts, axis=0)
    seq_s = _rwkv_prep(p_s, pinit_s, rw, ones_da, bs * ts, False, ts)
    pad_t = lambda a: jnp.pad(a.reshape(bs, ts, -1), ((0, 0), (0, rq - ts), (0, 0)))
    ya_s, wkv_s = _rwkv_scan([pad_t(a) for a in seq_s], state_wkv[0], rw, ones_da, rq, ts)
    ya_s = ya_s[:, :ts].reshape(bs * ts, DA)

    q_p, k_p, kbf_p, vbf_p, qi_p, ki_p, kihi_p, kilo_p = _rope(p_p, jnp.arange(t), min(512, t))
    pos_s = jnp.tile(past + jnp.arange(ts), bs)
    q_s, k_s, _, _, qi_s, ki_s, _, _ = _rope(p_s, pos_s, bs * ts)

    bias_p = _prompt_select(qi_p, p_p, kihi_p, kilo_p, min(128, t), 512)
    yb_p = _prompt_attn(q_p, kbf_p, vbf_p, bias_p, min(256, t), 256)

    topk_s = min(TOPK_MAX, (past + ts) // 4)
    pad_rows = lambda a, n: jnp.pad(a, ((0, 0), (0, n - a.shape[1]), (0, 0)))
    qi8 = pad_rows(qi_s.reshape(bs, ts, HI * DI), rq)
    wi8 = pad_rows(p_s[:, C_WI:C_WI + LANES].reshape(bs, ts, LANES), rq)
    kinew = pad_rows(ki_s.reshape(bs, ts, DI), PAGE)
    skey_s = _sample_scores(qi8, wi8, cache_kidx[0], kinew, page_table, ts)
    bias_s = _sample_select(skey_s.reshape(bs * rq, past + PAGE), topk_s, min(128, bs * rq))
    bias_s = bias_s.reshape(bs, rq, past + PAGE)
    v_s = p_s[:, C_VB:C_VB + DB]
    knew = pad_rows(k_s.reshape(bs, ts, DB), PAGE)
    vnew = pad_rows(v_s.reshape(bs, ts, DB), PAGE)
    qh = pad_rows(q_s.reshape(bs, ts, DB), rq).reshape(bs, rq, HB, DHB).transpose(0, 2, 3, 1)
    head_eq = jnp.eye(HB, dtype=bool)[None, :, None, :, None]
    qbd = jnp.where(head_eq, qh[:, :, :, None, :], jnp.zeros((), BF16)).reshape(bs, DB, HB * rq)
    qbd = jnp.pad(qbd, ((0, 0), (0, 0), (0, LANES - HB * rq)))
    yb_s = _sample_attn(qbd, cache_k[0].reshape(-1, PAGE, DB), cache_v[0].reshape(-1, PAGE, DB), knew, vnew,
                        bias_s, page_table, rq)
    yb_s = yb_s[:, :ts].reshape(bs * ts, DB)

    yc_p = _cross_attn(p_p[None], C_QC, mk_p[None], mv_p[None], min(512, t))[0]
    qc8 = pad_rows(p_s[:, C_QC:C_QC + DC].reshape(bs, ts, DC), rq)
    yc_s = _cross_attn(qc8, 0, cache_mem_k[0].reshape(bs, N_MEM, DC), cache_mem_v[0].reshape(bs, N_MEM, DC), rq)
    yc_s = yc_s[:, :ts].reshape(bs * ts, DC)

    y_p = _merge(xp2, p_p, ya_p[0], yb_p, yc_p, b_gate[0], wa, wb, wc, wo, final_norm_g, min(128, t))
    y_s = _merge(xs2, p_s, ya_s, yb_s, yc_s, b_gate[0], wa, wb, wc, wo, final_norm_g, bs * ts)

    p_s3 = p_s.reshape(bs, ts, N_P)
    shift_s = jnp.concatenate([p_s3[:, -1, :3 * DA], p_s3[:, -1, C_WDAD:C_WDAD + 2 * LORA]], axis=-1)
    return (y_p.reshape(1, t, d), y_s.reshape(bs, ts, d),
            k_p.reshape(1, 1, t, HB, DHB), p_p[:, C_VB:C_VB + DB].reshape(1, 1, t, HB, DHB),
            ki_p.reshape(1, 1, t, DI),
            mk_p.reshape(1, 1, N_MEM, HC, DHC), mv_p.reshape(1, 1, N_MEM, HC, DHC),
            wkv_p.reshape(1, 1, HA, DHA, DHA), _shift_row(p_p[t - 1]).reshape(1, 1, SHIFT_W),
            k_s.reshape(1, bs, ts, HB, DHB), v_s.reshape(1, bs, ts, HB, DHB), ki_s.reshape(1, bs, ts, DI),
            wkv_s.reshape(1, bs, HA, DHA, DHA), shift_s.reshape(1, bs, SHIFT_W))
```

```python
import functools

import numpy as np
import jax
import jax.numpy as jnp
from jax import lax
from jax.experimental import pallas as pl
from jax.experimental.pallas import tpu as pltpu

F32, BF16, I32 = jnp.float32, jnp.bfloat16, jnp.int32

D_MODEL = 2048
HA, DHA = 12, 64
DA = HA * DHA
LORA = 64
LNX_EPS = 64e-5
HB, DHB = 6, 128
DB = HB * DHB
HI, DI = 8, 64
TOPK_MAX = 256
N_MEM = 256
HC, DHC = 4, 128
DC = HC * DHC
ROPE_THETA = 10000.0
EPS = 1e-6
PAGE = 128
SHIFT_W = 3 * DA + 2 * LORA
SPLIT_SIZES = (SHIFT_W, DA, DB, DB, DB, HI * DI, HI, DI, DB, DC, DC, 3 * D_MODEL)

LANES = 128
SUBLANES = 8
N_PAIR = HA // 2

C_RKV, C_ZA, C_QB, C_KB, C_VB, C_ZB = 0, 2304, 3072, 3840, 4608, 5376
C_QI, C_QC, C_ZC, C_WDAD, C_KI, C_WI, C_GATES = 6144, 6656, 7168, 7680, 7808, 7936, 8192
N_P = C_GATES + 3 * D_MODEL

NEG_BIAS = -1e30
KEY_NEG_INF = -2139095041
INT_MIN = -2147483648
VMEM_LIMIT = 56 * 1024 * 1024


def _cparams(*sem):
    return pltpu.CompilerParams(dimension_semantics=sem, vmem_limit_bytes=VMEM_LIMIT)


def _dot(a, b):
    return jnp.dot(a, b, preferred_element_type=F32)


def _dot_nt(a, b):
    return lax.dot_general(a, b, (((1,), (1,)), ((), ())), preferred_element_type=F32)


def _split2(x):
    hi = x.astype(BF16)
    lo = (x - hi.astype(F32)).astype(BF16)
    return hi, lo


def _dot_ones(x, ones_bf):
    hi = x.astype(BF16)
    r1 = x - hi.astype(F32)
    mid = r1.astype(BF16)
    lo = (r1 - mid.astype(F32)).astype(BF16)
    return _dot(hi, ones_bf) + _dot(mid, ones_bf) + _dot(lo, ones_bf)


def _dot3(a, b):
    a_hi, a_lo = _split2(a)
    b_hi, b_lo = _split2(b)
    return _dot(a_hi, b_hi) + _dot(a_hi, b_lo) + _dot(a_lo, b_hi)


def _sigmoid(x):
    return 1.0 / (1.0 + jnp.exp(-x))


def _silu(x):
    return x * _sigmoid(x)


def _proj_kernel(x_ref, g_ref, w_ref, o_ref, xn_ref):
    @pl.when(pl.program_id(1) == 0)
    def _():
        x = x_ref[...]
        ms = jnp.mean(x * x, axis=-1, keepdims=True)
        xn_ref[...] = (x * lax.rsqrt(ms + EPS) * g_ref[...]).astype(BF16)

    o_ref[...] = _dot(xn_ref[...], w_ref[...])


def _proj(x2d, g, w_bf, tm, tn):
    m, d = x2d.shape
    n = w_bf.shape[1]
    return pl.pallas_call(
        _proj_kernel,
        grid=(m // tm, n // tn),
        in_specs=[pl.BlockSpec((tm, d), lambda i, j: (i, 0)),
                  pl.BlockSpec((1, d), lambda i, j: (0, 0)),
                  pl.BlockSpec((d, tn), lambda i, j: (0, j))],
        out_specs=pl.BlockSpec((tm, tn), lambda i, j: (i, j)),
        out_shape=jax.ShapeDtypeStruct((m, n), F32),
        scratch_shapes=[pltpu.VMEM((tm, d), BF16)],
        compiler_params=_cparams("parallel", "arbitrary"),
    )(x2d, g.reshape(1, d), w_bf)


def _permute_w_in(w):
    o = [int(v) for v in np.cumsum((0,) + SPLIT_SIZES)]
    za0, qb0, kb0, vb0, qi0, wi0, ki0, zb0, qc0, zc0, g0 = o[1:12]
    z = lambda n: jnp.zeros((w.shape[0], n), w.dtype)
    parts = [w[:, 0:3 * DA], w[:, za0:za0 + DA], w[:, qb0:qb0 + DB], w[:, kb0:kb0 + DB], w[:, vb0:vb0 + DB],
             w[:, zb0:zb0 + DB], w[:, qi0:qi0 + HI * DI], w[:, qc0:qc0 + DC], w[:, zc0:zc0 + DC],
             w[:, 3 * DA:SHIFT_W], w[:, ki0:ki0 + DI], z(LANES - DI), w[:, wi0:wi0 + HI], z(LANES - HI), z(LANES),
             w[:, g0:g0 + 3 * D_MODEL]]
    out = jnp.concatenate(parts, axis=1).astype(BF16)
    assert out.shape[1] == N_P
    return out


def _prep_kernel(carry_mode, period, rkv_ref, wdad_ref, pinit_ref, mu_ref, w0_ref, wup_ref, a0_ref, aup_ref,
                 kk_ref, ka_ref, ones_ref, r_o, w_o, k_o, v_o, kn_o, b_o, carry_ref):
    tm = rkv_ref.shape[0]
    row = lax.broadcasted_iota(I32, (tm, 1), 0)
    cur = jnp.concatenate([rkv_ref[...], wdad_ref[...]], axis=1)
    rolled = pltpu.roll(cur, 1, 0)
    if carry_mode:
        @pl.when(pl.program_id(0) == 0)
        def _():
            carry_ref[...] = pinit_ref[...]
        prev = jnp.where(row == 0, carry_ref[...], rolled)
        carry_ref[...] = cur[tm - 1:tm, :]
    else:
        prev = jnp.where((row & (period - 1)) == 0, pinit_ref[...], rolled)
    m = cur + mu_ref[...] * (prev - cur)
    r = m[:, 0:DA]
    k = m[:, DA:2 * DA]
    v = m[:, 2 * DA:3 * DA]
    wd = m[:, 3 * DA:3 * DA + LORA]
    ad = m[:, 3 * DA + LORA:SHIFT_W]
    lw = w0_ref[...] + _dot3(jnp.tanh(wd), wup_ref[...])
    z = -lw
    softplus = jnp.maximum(z, 0.0) + jnp.log(1.0 + jnp.exp(-jnp.abs(z)))
    logw = -softplus - 0.5
    decay = jnp.exp(-jnp.exp(logw))
    a = _sigmoid(a0_ref[...] + _dot3(ad, aup_ref[...]))
    kk = k * kk_ref[...]
    ss = _dot_ones(kk * kk, ones_ref[...])
    kn = kk * lax.rsqrt(ss + 1e-12)
    r_o[...] = r
    w_o[...] = decay
    k_o[...] = k * (1.0 + (a - 1.0) * ka_ref[...])
    v_o[...] = v
    kn_o[...] = kn
    b_o[...] = kn * a


def _rwkv_prep(p, pinit, rw, ones_da, tm, carry_mode, period=1):
    t = p.shape[0]
    mu, w0, w_up, a0, a_up, k_k, k_a = rw[:7]
    row = lambda v: v.reshape(1, -1)
    full = lambda shape: pl.BlockSpec(shape, lambda i: (0, 0))
    pin_spec = full((1, SHIFT_W)) if carry_mode else pl.BlockSpec((tm, SHIFT_W), lambda i: (i, 0))
    out = jax.ShapeDtypeStruct((t, DA), F32)
    return pl.pallas_call(
        functools.partial(_prep_kernel, carry_mode, period),
        grid=(t // tm,),
        in_specs=[pl.BlockSpec((tm, 3 * DA), lambda i: (i, 0)),
                  pl.BlockSpec((tm, LANES), lambda i: (i, C_WDAD // LANES)),
                  pin_spec, full((1, SHIFT_W)), full((1, DA)), full((LORA, DA)), full((1, DA)), full((LORA, DA)),
                  full((1, DA)), full((1, DA)), full((DA, DA))],
        out_specs=[pl.BlockSpec((tm, DA), lambda i: (i, 0))] * 6,
        out_shape=[out] * 6,
        scratch_shapes=[pltpu.VMEM((1, SHIFT_W), F32)],
        compiler_params=_cparams("arbitrary"),
    )(p, p, pinit, row(mu), row(w0), w_up, row(a0), a_up, row(k_k), row(k_a), ones_da)


def _seg_sum_bcast(x, ones2_ref):
    hi = x.astype(BF16)
    lo = (x - hi.astype(F32)).astype(BF16)
    return _dot(jnp.concatenate([hi, lo], axis=1), ones2_ref[...])


def _seg_sum_lanes(x, lo_lanes):
    lo = jnp.sum(jnp.where(lo_lanes, x, 0.0), axis=1, keepdims=True)
    hi = jnp.sum(jnp.where(lo_lanes, 0.0, x), axis=1, keepdims=True)
    return jnp.where(lo_lanes, lo, hi)


def _scan_kernel(n_tok, r_ref, w_ref, k_ref, v_ref, kn_ref, b_ref, s0_ref, rk_ref, g_ref, beta_ref, ones_ref,
                 ones2_ref, y_o, s_o, s_ref, y_ref):
    c = pl.program_id(1)

    @pl.when(c == 0)
    def _():
        s_ref[...] = s0_ref[...]

    lane = lax.broadcasted_iota(I32, (DHA, LANES), 1)
    sub = lax.broadcasted_iota(I32, (DHA, LANES), 0)
    diag = (lane & (DHA - 1)) == sub
    lo_lanes = lane < DHA
    pairs = range(N_PAIR)

    def group(base, n):
        rows8 = pl.ds(base, SUBLANES)
        tile = lambda ref, p: ref[rows8, p * LANES:(p + 1) * LANES]
        y_rows = [[] for _ in pairs]
        for u in range(n):
            row = slice(u, u + 1)
            vb = [_seg_sum_bcast(jnp.where(diag, tile(v_ref, p)[row], 0.0), ones2_ref) for p in pairs]
            sa = [_seg_sum_lanes(s_ref[p] * tile(kn_ref, p)[row], lo_lanes) for p in pairs]
            for p in pairs:
                s_ref[p] = (s_ref[p] * tile(w_ref, p)[row] - sa[p] * tile(b_ref, p)[row]
                            + vb[p] * tile(k_ref, p)[row])
            yb = [_seg_sum_bcast(s_ref[p] * tile(r_ref, p)[row], ones2_ref) for p in pairs]
            for p in pairs:
                y_rows[p].append(jnp.sum(jnp.where(diag, yb[p], 0.0), axis=0, keepdims=True))
        for p in pairs:
            rows = y_rows[p] + [jnp.zeros((1, LANES), F32)] * (SUBLANES - n)
            y_ref[rows8, p * LANES:(p + 1) * LANES] = jnp.concatenate(rows, axis=0)

    n_full = n_tok // SUBLANES

    def full_group(g, carry):
        group(pl.multiple_of(g * SUBLANES, SUBLANES), SUBLANES)
        return carry

    lax.fori_loop(0, n_full, full_group, 0)
    if n_tok % SUBLANES:
        group(n_full * SUBLANES, n_tok % SUBLANES)

    ones = ones_ref[...]
    y = y_ref[...]
    mean = _dot_ones(y, ones) * (1.0 / DHA)
    yc = y - mean
    var = _dot_ones(yc * yc, ones) * (1.0 / DHA)
    yn = yc * lax.rsqrt(var + LNX_EPS) * g_ref[...] + beta_ref[...]
    bonus = _dot_ones(r_ref[...] * k_ref[...] * rk_ref[...], ones) * v_ref[...]
    y_o[...] = yn + bonus

    @pl.when(c == pl.num_programs(1) - 1)
    def _():
        s_o[...] = s_ref[...]


def _pack_state(s):
    b = s.shape[0]
    return s.reshape(b, N_PAIR, 2, DHA, DHA).transpose(0, 1, 3, 2, 4).reshape(b, N_PAIR, DHA, LANES)


def _unpack_state(s):
    b = s.shape[0]
    return s.reshape(b, N_PAIR, DHA, 2, DHA).transpose(0, 1, 3, 2, 4).reshape(b, HA, DHA, DHA)


def _rwkv_scan(seqs, s0, rw, ones_da, tc, n_tok):
    bsz, t, _ = seqs[0].shape
    assert tc == -(-n_tok // SUBLANES) * SUBLANES
    r_k, lnx_g, lnx_b = rw[7:10]
    row = lambda v: v.reshape(1, DA)
    seq_spec = pl.BlockSpec((None, tc, DA), lambda b, c: (b, c, 0))
    st_spec = pl.BlockSpec((None, N_PAIR, DHA, LANES), lambda b, c: (b, 0, 0, 0))
    full = lambda shape: pl.BlockSpec(shape, lambda b, c: (0, 0))
    y, s_fin = pl.pallas_call(
        functools.partial(_scan_kernel, n_tok),
        grid=(bsz, t // tc),
        in_specs=[seq_spec] * 6 + [st_spec, full((1, DA)), full((1, DA)), full((1, DA)), full((DA, DA)),
                                   full((2 * LANES, LANES))],
        out_specs=[seq_spec, st_spec],
        out_shape=[jax.ShapeDtypeStruct((bsz, t, DA), F32), jax.ShapeDtypeStruct((bsz, N_PAIR, DHA, LANES), F32)],
        scratch_shapes=[pltpu.VMEM((N_PAIR, DHA, LANES), F32), pltpu.VMEM((tc, DA), F32)],
        compiler_params=_cparams("parallel", "arbitrary"),
    )(*seqs, _pack_state(s0), row(r_k), row(lnx_g), row(lnx_b), ones_da,
      jnp.concatenate([ones_da[:LANES, :LANES]] * 2, axis=0))
    return y, _unpack_state(s_fin)


def _rope_kernel(qb_ref, kb_ref, vb_ref, qi_ref, ki_ref, c128_ref, s128_ref, c64_ref, s64_ref,
                 q_o, k_o, kbf_o, vbf_o, qi_o, ki_o, kihi_o, kilo_o):
    c128, s128, c64, s64 = c128_ref[...], s128_ref[...], c64_ref[...], s64_ref[...]
    lane = lax.broadcasted_iota(I32, c64.shape, 1)
    first_half = (lane & (DI - 1)) < DI // 2

    def rope128(x):
        return x * c128 + pltpu.roll(x, DHB // 2, 1) * s128

    def rope64(x):
        partner = jnp.where(first_half, pltpu.roll(x, LANES - DI // 2, 1), pltpu.roll(x, DI // 2, 1))
        return x * c64 + partner * s64

    for h in range(HB):
        sl = slice(h * DHB, (h + 1) * DHB)
        q_o[:, sl] = rope128(qb_ref[:, sl]).astype(BF16)
        kr = rope128(kb_ref[:, sl])
        k_o[:, sl] = kr
        kbf_o[:, sl] = kr.astype(BF16)
    vbf_o[...] = vb_ref[...].astype(BF16)
    for h2 in range(HI * DI // LANES):
        sl = slice(h2 * LANES, (h2 + 1) * LANES)
        qi_o[:, sl] = rope64(qi_ref[:, sl])
    ki = rope64(ki_ref[...])[:, :DI]
    ki_o[...] = ki
    hi, lo = _split2(ki)
    kihi_o[...] = hi
    kilo_o[...] = lo


def _rope_tables(pos, d):
    inv = ROPE_THETA ** (-jnp.arange(0, d, 2, dtype=F32) / d)
    ang = pos.astype(F32)[:, None] * inv[None, :]
    cos, sin = jnp.cos(ang), jnp.sin(ang)
    reps = LANES // d
    c = jnp.tile(jnp.concatenate([cos, cos], axis=1), (1, reps))
    s = jnp.tile(jnp.concatenate([-sin, sin], axis=1), (1, reps))
    return c, s


def _rope(p, pos, tm):
    t = p.shape[0]
    c128, s128 = _rope_tables(pos, DHB)
    c64, s64 = _rope_tables(pos, DI)
    col = lambda w, c0: pl.BlockSpec((tm, w), lambda i: (i, c0 // w))
    rows = lambda w: pl.BlockSpec((tm, w), lambda i: (i, 0))
    sds = lambda w, dt: jax.ShapeDtypeStruct((t, w), dt)
    return pl.pallas_call(
        _rope_kernel,
        grid=(t // tm,),
        in_specs=[col(DB, C_QB), col(DB, C_KB), col(DB, C_VB), col(HI * DI, C_QI), col(LANES, C_KI),
                  rows(LANES), rows(LANES), rows(LANES), rows(LANES)],
        out_specs=[rows(DB), rows(DB), rows(DB), rows(DB), rows(HI * DI), rows(DI), rows(DI), rows(DI)],
        out_shape=[sds(DB, BF16), sds(DB, F32), sds(DB, BF16), sds(DB, BF16), sds(HI * DI, F32), sds(DI, F32),
                   sds(DI, BF16), sds(DI, BF16)],
        compiler_params=_cparams("parallel"),
    )(p, p, p, p, p, c128, s128, c64, s64)


def _sort_key(score):
    score = jnp.where(score == 0.0, 0.0, score)
    b = lax.bitcast_convert_type(score, I32)
    return b ^ ((b >> 31) & 0x7FFFFFFF)


def _index_scores(qi_hi, qi_lo, wi, k_hi, k_lo):
    acc = None
    for h in range(HI):
        sl = slice(h * DI, (h + 1) * DI)
        s = _dot_nt(qi_hi[:, sl], k_hi) + _dot_nt(qi_hi[:, sl], k_lo) + _dot_nt(qi_lo[:, sl], k_hi)
        term = wi[:, h:h + 1] * jnp.maximum(s, 0.0)
        acc = term if acc is None else acc + term
    return acc * (HI ** -0.5 * DI ** -0.5)


def _count(skey_ref, nc, ck, pred):
    rows = skey_ref.shape[0]

    def body(c, acc):
        base = pl.multiple_of(c * ck, ck)
        for u in range(ck // LANES):
            blk = skey_ref[:, pl.ds(base + u * LANES, LANES)]
            acc = acc + jnp.where(pred(blk, base + u * LANES), 1, 0)
        return acc

    acc = lax.fori_loop(0, nc, body, jnp.zeros((rows, LANES), I32))
    return jnp.sum(acc, axis=1, keepdims=True)


def _select_bias(skey_ref, bias_ref, cut_ref, nc, nc_total, ck, topk, idx_bits):
    rows = skey_ref.shape[0]
    lane = lax.broadcasted_iota(I32, (rows, LANES), 1)

    def bit_body(it, ans_u):
        cand_u = ans_u | lax.shift_left(jnp.int32(1), 31 - it)
        cand_s = cand_u ^ INT_MIN
        cnt = _count(skey_ref, nc, ck, lambda blk, _: blk >= cand_s)
        return jnp.where(cnt >= topk, cand_u, ans_u)

    thr = lax.fori_loop(0, 32, bit_body, jnp.zeros((rows, 1), I32)) ^ INT_MIN
    n_gt = _count(skey_ref, nc, ck, lambda blk, _: blk > thr)
    n_ge = _count(skey_ref, nc, ck, lambda blk, _: blk >= thr)
    need = topk - n_gt
    tied = ((n_ge - n_gt) > need) & (thr > KEY_NEG_INF)
    cut_ref[...] = jnp.full((rows, 1), 2 ** 30, I32)

    @pl.when(jnp.max(tied.astype(I32)) > 0)
    def _():
        def tie_body(it, ans):
            cand = ans | lax.shift_left(jnp.int32(1), idx_bits - 1 - it)
            g = _count(skey_ref, nc, ck, lambda blk, base: (blk == thr) & ((lane + base) < cand))
            return jnp.where(g < need, cand, ans)

        cut = lax.fori_loop(0, idx_bits, tie_body, jnp.zeros((rows, 1), I32))
        cut_ref[...] = jnp.where(tied, cut, 2 ** 30)

    cut = cut_ref[...]

    def write(c, carry):
        base = pl.multiple_of(c * ck, ck)
        for u in range(ck // LANES):
            off = base + u * LANES
            blk = skey_ref[:, pl.ds(off, LANES)]
            sel = ((blk > thr) | ((blk == thr) & ((lane + off) <= cut))) & (blk > KEY_NEG_INF)
            bias_ref[:, pl.ds(off, LANES)] = jnp.where(sel, 0.0, NEG_BIAS).astype(bias_ref.dtype)
        return carry

    lax.fori_loop(0, nc, write, 0)

    def fill(c, carry):
        base = pl.multiple_of(c * ck, ck)
        bias_ref[:, pl.ds(base, ck)] = jnp.full((rows, ck), NEG_BIAS, bias_ref.dtype)
        return carry

    lax.fori_loop(nc, nc_total, fill, 0)


def _prompt_select_kernel(topk, ck, idx_bits, qi_ref, wi_ref, kihi_ref, kilo_ref, bias_o, skey_ref, cut_ref):
    i = pl.program_id(0)
    tq = qi_ref.shape[0]
    t = kihi_ref.shape[0]
    nc = ((i + 1) * tq + ck - 1) // ck
    qi_hi, qi_lo = _split2(qi_ref[...])
    wi = wi_ref[...]
    qpos = i * tq + lax.broadcasted_iota(I32, (tq, ck), 0)
    kiota = lax.broadcasted_iota(I32, (tq, ck), 1)

    def score_chunk(c, carry):
        base = pl.multiple_of(c * ck, ck)
        sc = _index_scores(qi_hi, qi_lo, wi, kihi_ref[pl.ds(base, ck), :], kilo_ref[pl.ds(base, ck), :])
        sc = jnp.where(kiota + base <= qpos, sc, -jnp.inf)
        skey_ref[:, pl.ds(base, ck)] = _sort_key(sc)
        return carry

    lax.fori_loop(0, nc, score_chunk, 0)
    _select_bias(skey_ref, bias_o, cut_ref, nc, t // ck, ck, topk, idx_bits)


def _prompt_select(qi, p, ki_hi, ki_lo, tq, ck):
    t = qi.shape[0]
    topk = min(TOPK_MAX, t // 4)
    idx_bits = int(np.ceil(np.log2(t))) + 1
    return pl.pallas_call(
        functools.partial(_prompt_select_kernel, topk, ck, idx_bits),
        grid=(t // tq,),
        in_specs=[pl.BlockSpec((tq, HI * DI), lambda i: (i, 0)),
                  pl.BlockSpec((tq, LANES), lambda i: (i, C_WI // LANES)),
                  pl.BlockSpec((t, DI), lambda i: (0, 0)),
                  pl.BlockSpec((t, DI), lambda i: (0, 0))],
        out_specs=pl.BlockSpec((tq, t), lambda i: (i, 0)),
        out_shape=jax.ShapeDtypeStruct((t, t), BF16),
        scratch_shapes=[pltpu.VMEM((tq, t), I32), pltpu.VMEM((tq, 1), I32)],
        compiler_params=_cparams("parallel"),
    )(qi, p, ki_hi, ki_lo)


def _sample_score_kernel(pps, rq, n_new, pt_ref, qit_ref, wif_ref, *refs):
    page_refs, knew_ref, skey_o, skey_new_o = refs[:pps], refs[pps], refs[pps + 1], refs[pps + 2]
    j = pl.program_id(1)
    q_hi, q_lo = _split2(qit_ref[...])
    wif = wif_ref[...]

    def scores(kc):
        k_hi, k_lo = _split2(kc)
        st = _dot(k_hi, q_hi) + _dot(k_lo, q_hi) + _dot(k_hi, q_lo)
        at = (jnp.maximum(st, 0.0) * wif).T
        sc = at[0:rq]
        for h in range(1, HI):
            sc = sc + at[h * rq:(h + 1) * rq]
        return sc * (HI ** -0.5 * DI ** -0.5)

    sc = scores(jnp.concatenate([r[...] for r in page_refs], axis=0))
    trow = lax.broadcasted_iota(I32, sc.shape, 0)
    skey_o[...] = _sort_key(jnp.where(trow < n_new, sc, -jnp.inf))

    @pl.when(j == pl.num_programs(1) - 1)
    def _():
        scn = scores(knew_ref[...])
        tr = lax.broadcasted_iota(I32, scn.shape, 0)
        ln = lax.broadcasted_iota(I32, scn.shape, 1)
        skey_new_o[...] = _sort_key(jnp.where((tr < n_new) & (ln <= tr), scn, -jnp.inf))


def _sample_scores(qit, wif, cache_kidx, knew_pad, page_table, n_new, rq, pps):
    bsz, n_pages = page_table.shape
    per_b = lambda b, j, pt: (b, 0, 0)
    page_spec = lambda u: pl.BlockSpec((None, PAGE, DI), lambda b, j, pt: (pt[b, j * pps + u], 0, 0))
    grid_spec = pltpu.PrefetchScalarGridSpec(
        num_scalar_prefetch=1,
        grid=(bsz, n_pages // pps),
        in_specs=[pl.BlockSpec((None, DI, LANES), per_b), pl.BlockSpec((None, 1, LANES), per_b)]
                 + [page_spec(u) for u in range(pps)] + [pl.BlockSpec((None, PAGE, DI), per_b)],
        out_specs=[pl.BlockSpec((None, rq, pps * PAGE), lambda b, j, pt: (b, 0, j)),
                   pl.BlockSpec((None, rq, PAGE), per_b)],
    )
    return pl.pallas_call(
        functools.partial(_sample_score_kernel, pps, rq, n_new),
        grid_spec=grid_spec,
        out_shape=[jax.ShapeDtypeStruct((bsz, rq, n_pages * PAGE), I32), jax.ShapeDtypeStruct((bsz, rq, PAGE), I32)],
        compiler_params=_cparams("parallel", "arbitrary"),
    )(page_table, qit, wif, *([cache_kidx] * pps), knew_pad)


def _sample_select_kernel(topk, ck, idx_bits, skey_past, skey_new, bias_o, skey_ref, cut_ref):
    past = skey_past.shape[1]
    skey_ref[:, :past] = skey_past[...]
    skey_ref[:, past:] = skey_new[...]
    nc = skey_ref.shape[1] // ck
    _select_bias(skey_ref, bias_o, cut_ref, nc, nc, ck, topk, idx_bits)


def _sample_select(skey_past, skey_new, topk, tr):
    rows, past = skey_past.shape
    n = past + skey_new.shape[1]
    idx_bits = int(np.ceil(np.log2(n))) + 1
    return pl.pallas_call(
        functools.partial(_sample_select_kernel, topk, PAGE, idx_bits),
        grid=(rows // tr,),
        in_specs=[pl.BlockSpec((tr, past), lambda i: (i, 0)), pl.BlockSpec((tr, n - past), lambda i: (i, 0))],
        out_specs=pl.BlockSpec((tr, n), lambda i: (i, 0)),
        out_shape=jax.ShapeDtypeStruct((rows, n), F32),
        scratch_shapes=[pltpu.VMEM((tr, n), I32), pltpu.VMEM((tr, 1), I32)],
        compiler_params=_cparams("parallel"),
    )(skey_past, skey_new)


def _prompt_attn_kernel(ck, q_ref, k_ref, v_ref, bias_ref, o_ref, m_ref, l_ref, acc_ref):
    i = pl.program_id(0)
    tq = q_ref.shape[0]
    nc = ((i + 1) * tq + ck - 1) // ck
    m_ref[...] = jnp.full(m_ref.shape, NEG_BIAS, F32)
    l_ref[...] = jnp.zeros(l_ref.shape, F32)
    acc_ref[...] = jnp.zeros(acc_ref.shape, F32)
    scale = DHB ** -0.5

    def chunk(c, carry):
        base = pl.multiple_of(c * ck, ck)
        bias = bias_ref[:, pl.ds(base, ck)].astype(F32)
        for h in range(HB):
            sl = slice(h * DHB, (h + 1) * DHB)
            s = _dot_nt(q_ref[:, sl], k_ref[pl.ds(base, ck), sl]) * scale + bias
            m_old = m_ref[h]
            m_new = jnp.maximum(m_old, jnp.max(s, axis=1, keepdims=True))
            alpha = jnp.exp(m_old - m_new)
            p = jnp.exp(s - m_new)
            l_ref[h] = alpha * l_ref[h] + jnp.sum(p, axis=1, keepdims=True)
            acc_ref[h] = alpha * acc_ref[h] + _dot(p.astype(BF16), v_ref[pl.ds(base, ck), sl])
            m_ref[h] = m_new
        return carry

    lax.fori_loop(0, nc, chunk, 0)
    for h in range(HB):
        o_ref[:, h * DHB:(h + 1) * DHB] = acc_ref[h] / l_ref[h]


def _prompt_attn(q_bf, k_bf, v_bf, bias, tq, ck):
    t = q_bf.shape[0]
    resident = lambda: pl.BlockSpec((t, DB), lambda i: (0, 0), pipeline_mode=pl.Buffered(1))
    return pl.pallas_call(
        functools.partial(_prompt_attn_kernel, ck),
        grid=(t // tq,),
        in_specs=[pl.BlockSpec((tq, DB), lambda i: (i, 0)), resident(), resident(),
                  pl.BlockSpec((tq, t), lambda i: (i, 0))],
        out_specs=pl.BlockSpec((tq, DB), lambda i: (i, 0)),
        out_shape=jax.ShapeDtypeStruct((t, DB), F32),
        scratch_shapes=[pltpu.VMEM((HB, tq, 1), F32), pltpu.VMEM((HB, tq, 1), F32), pltpu.VMEM((HB, tq, DHB), F32)],
        compiler_params=_cparams("parallel"),
    )(q_bf, k_bf, v_bf, bias)


def _sample_attn_kernel(pps, rq, pt_ref, qbd_ref, *refs):
    kpages, vpages = refs[:pps], refs[pps:2 * pps]
    knew_ref, vnew_ref, bias_ref, bias_new_ref, o_ref, m_ref, l_ref, acc_ref = refs[2 * pps:]
    j = pl.program_id(1)
    nr = HB * rq

    @pl.when(j == 0)
    def _():
        m_ref[...] = jnp.full(m_ref.shape, NEG_BIAS, F32)
        l_ref[...] = jnp.zeros(l_ref.shape, F32)
        acc_ref[...] = jnp.zeros(acc_ref.shape, F32)

    def accumulate(kp, vp, bias):
        st = _dot(kp, qbd_ref[...])
        s = st.T[:nr] * (DHB ** -0.5) + jnp.concatenate([bias] * HB, axis=0)
        m_old = m_ref[...]
        m_new = jnp.maximum(m_old, jnp.max(s, axis=1, keepdims=True))
        alpha = jnp.exp(m_old - m_new)
        p = jnp.exp(s - m_new)
        l_ref[...] = alpha * l_ref[...] + jnp.sum(p, axis=1, keepdims=True)
        acc_ref[...] = alpha * acc_ref[...] + _dot(p.astype(BF16), vp)
        m_ref[...] = m_new

    def gather(pages):
        out = []
        for r in pages:
            hkd = pltpu.einshape("khd->hkd", r[...])
            out.append(jnp.concatenate([hkd[h].astype(BF16) for h in range(HB)], axis=1))
        return jnp.concatenate(out, axis=0)
    accumulate(gather(kpages), gather(vpages), bias_ref[...])

    @pl.when(j == pl.num_programs(1) - 1)
    def _():
        accumulate(knew_ref[...].astype(BF16), vnew_ref[...].astype(BF16), bias_new_ref[...])
        for h in range(HB):
            rs = slice(h * rq, (h + 1) * rq)
            cs = slice(h * DHB, (h + 1) * DHB)
            o_ref[:, cs] = acc_ref[rs, cs] / l_ref[rs, :]


def _sample_attn(qbd, cache_k, cache_v, knew_pad, vnew_pad, bias, page_table, rq, pps):
    bsz, n_pages = page_table.shape
    page_spec = lambda u: pl.BlockSpec((None, PAGE, HB, DHB), lambda b, j, pt: (pt[b, j * pps + u], 0, 0, 0))
    per_b = lambda b, j, pt: (b, 0, 0)
    grid_spec = pltpu.PrefetchScalarGridSpec(
        num_scalar_prefetch=1,
        grid=(bsz, n_pages // pps),
        in_specs=[pl.BlockSpec((None, DB, LANES), per_b)]
                 + [page_spec(u) for u in range(pps)] * 2
                 + [pl.BlockSpec((None, PAGE, DB), per_b), pl.BlockSpec((None, PAGE, DB), per_b),
                    pl.BlockSpec((None, rq, pps * PAGE), lambda b, j, pt: (b, 0, j)),
                    pl.BlockSpec((None, rq, PAGE), lambda b, j, pt: (b, 0, n_pages))],
        out_specs=pl.BlockSpec((None, rq, DB), per_b),
        scratch_shapes=[pltpu.VMEM((HB * rq, 1), F32), pltpu.VMEM((HB * rq, 1), F32), pltpu.VMEM((HB * rq, DB), F32)],
    )
    return pl.pallas_call(
        functools.partial(_sample_attn_kernel, pps, rq),
        grid_spec=grid_spec,
        out_shape=jax.ShapeDtypeStruct((bsz, rq, DB), F32),
        compiler_params=_cparams("parallel", "arbitrary"),
    )(page_table, qbd, *([cache_k] * pps), *([cache_v] * pps), knew_pad, vnew_pad, bias, bias)


def _cross_kernel(q_ref, mk_ref, mv_ref, o_ref):
    scale = DHC ** -0.5
    mk = pltpu.einshape("mhd->hmd", mk_ref[...]).astype(BF16)
    mv = pltpu.einshape("mhd->hmd", mv_ref[...]).astype(BF16)
    for h in range(HC):
        sl = slice(h * DHC, (h + 1) * DHC)
        s = _dot_nt(q_ref[:, sl].astype(BF16), mk[h]) * scale
        m = jnp.max(s, axis=1, keepdims=True)
        p = jnp.exp(s - m)
        l = jnp.sum(p, axis=1, keepdims=True)
        o_ref[:, sl] = _dot(p.astype(BF16), mv[h]) / l


def _cross_attn(q_arr, q_col, mk, mv, tm):
    bsz, t, _ = q_arr.shape
    return pl.pallas_call(
        _cross_kernel,
        grid=(bsz, t // tm),
        in_specs=[pl.BlockSpec((None, tm, DC), lambda b, i: (b, i, q_col // DC)),
                  pl.BlockSpec((None, N_MEM, HC, DHC), lambda b, i: (b, 0, 0, 0)),
                  pl.BlockSpec((None, N_MEM, HC, DHC), lambda b, i: (b, 0, 0, 0))],
        out_specs=pl.BlockSpec((None, tm, DC), lambda b, i: (b, i, 0)),
        out_shape=jax.ShapeDtypeStruct((bsz, t, DC), F32),
        compiler_params=_cparams("parallel", "parallel"),
    )(q_arr, mk, mv)


def _merge_kernel(x_ref, ya_ref, za_ref, yb_ref, zb_ref, yc_ref, zc_ref, ga_ref, gb_ref, gc_ref,
                  bga_ref, bgb_ref, bgc_ref, wa_ref, wb_ref, wc_ref, wo_ref, fg_ref, o_ref):
    ba = _dot((ya_ref[...] * _silu(za_ref[...])).astype(BF16), wa_ref[...])
    bb = _dot((yb_ref[...] * _silu(zb_ref[...])).astype(BF16), wb_ref[...])
    bc = _dot((yc_ref[...] * _silu(zc_ref[...])).astype(BF16), wc_ref[...])
    mix = (_sigmoid(ga_ref[...] + bga_ref[...]) * ba + _sigmoid(gb_ref[...] + bgb_ref[...]) * bb
           + _sigmoid(gc_ref[...] + bgc_ref[...]) * bc)
    y = x_ref[...] + _dot(mix.astype(BF16), wo_ref[...])
    ms = jnp.mean(y * y, axis=-1, keepdims=True)
    o_ref[...] = y * lax.rsqrt(ms + EPS) * fg_ref[...]


def _merge(x2d, p, ya, yb, yc, b_gate, wa, wb, wc, wo, fg, tm):
    t, d = x2d.shape
    rows = lambda w: pl.BlockSpec((tm, w), lambda i: (i, 0))
    col = lambda w, c0: pl.BlockSpec((tm, w), lambda i: (i, c0 // w))
    const = lambda shape, c=0: pl.BlockSpec(shape, lambda i: (0, c), pipeline_mode=pl.Buffered(1))
    bg = b_gate.reshape(1, 3 * d)
    return pl.pallas_call(
        _merge_kernel,
        grid=(t // tm,),
        in_specs=[rows(d), rows(DA), col(DA, C_ZA), rows(DB), col(DB, C_ZB), rows(DC), col(DC, C_ZC),
                  col(d, C_GATES), col(d, C_GATES + d), col(d, C_GATES + 2 * d),
                  const((1, d), 0), const((1, d), 1), const((1, d), 2),
                  const((DA, d)), const((DB, d)), const((DC, d)), const((d, d)), const((1, d))],
        out_specs=rows(d),
        out_shape=jax.ShapeDtypeStruct((t, d), F32),
        compiler_params=_cparams("parallel"),
    )(x2d, ya, p, yb, p, yc, p, p, p, p, bg, bg, bg, wa, wb, wc, wo, fg.reshape(1, d))


def _block_ones(n, seg):
    idx = np.arange(n) // seg
    return jnp.asarray(idx[:, None] == idx[None, :], dtype=BF16)


def _shift_row(p_row):
    return jnp.concatenate([p_row[:3 * DA], p_row[C_WDAD:C_WDAD + 2 * LORA]])


def kernel(x_prompt, x_sample, mem_prompt, cache_k, cache_v, cache_kidx, cache_mem_k, cache_mem_v, state_wkv, state_shift, page_table, norm_g, w_in, rwkv_mu, rwkv_w0, rwkv_w_up, rwkv_a0, rwkv_a_up, rwkv_k_k, rwkv_k_a, rwkv_r_k, rwkv_lnx_g, rwkv_lnx_b, mem_norm_g, w_mem_kv, b_gate, w_br_a, w_br_b, w_br_c, w_out, final_norm_g):
    depth = w_in.shape[0]
    assert depth == 1 and x_prompt.shape[0] == 1
    t = x_prompt.shape[1]
    bs, ts, d = x_sample.shape
    n_pages = page_table.shape[1]
    past = n_pages * PAGE
    rq = 8
    assert ts <= rq and (ts & (ts - 1)) == 0

    rw = (rwkv_mu[0], rwkv_w0[0], rwkv_w_up[0], rwkv_a0[0], rwkv_a_up[0], rwkv_k_k[0], rwkv_k_a[0],
          rwkv_r_k[0].reshape(DA), rwkv_lnx_g[0], rwkv_lnx_b[0])
    ones_da = _block_ones(DA, DHA)
    w_p = _permute_w_in(w_in[0])
    wa, wb, wc, wo = (w[0].astype(BF16) for w in (w_br_a, w_br_b, w_br_c, w_out))

    xp2 = x_prompt.reshape(t, d)
    xs2 = x_sample.reshape(bs * ts, d)
    tm_p = min(512, t)
    p_p = _proj(xp2, norm_g[0], w_p, tm_p, 1024)
    p_s = _proj(xs2, norm_g[0], w_p, bs * ts, 1024)
    mkv = _proj(mem_prompt.reshape(N_MEM, d), mem_norm_g[0], w_mem_kv[0].astype(BF16), N_MEM, 2 * DC)
    mk_p = mkv[:, :DC].reshape(1, N_MEM, HC, DHC)
    mv_p = mkv[:, DC:].reshape(1, N_MEM, HC, DHC)

    seq_p = _rwkv_prep(p_p, jnp.zeros((1, SHIFT_W), F32), rw, ones_da, min(256, t), True)
    ya_p, wkv_p = _rwkv_scan([a[None] for a in seq_p], jnp.zeros((1, HA, DHA, DHA), F32), rw, ones_da,
                             min(256, t), min(256, t))
    pinit_s = jnp.repeat(state_shift[0], ts, axis=0)
    seq_s = _rwkv_prep(p_s, pinit_s, rw, ones_da, bs * ts, False, ts)
    pad_t = lambda a: jnp.pad(a.reshape(bs, ts, -1), ((0, 0), (0, rq - ts), (0, 0)))
    ya_s, wkv_s = _rwkv_scan([pad_t(a) for a in seq_s], state_wkv[0], rw, ones_da, rq, ts)
    ya_s = ya_s[:, :ts].reshape(bs * ts, DA)

    q_p, k_p, kbf_p, vbf_p, qi_p, ki_p, kihi_p, kilo_p = _rope(p_p, jnp.arange(t), min(512, t))
    pos_s = jnp.tile(past + jnp.arange(ts), bs)
    q_s, k_s, _, _, qi_s, ki_s, _, _ = _rope(p_s, pos_s, bs * ts)

    bias_p = _prompt_select(qi_p, p_p, kihi_p, kilo_p, min(128, t), 512)
    yb_p = _prompt_attn(q_p, kbf_p, vbf_p, bias_p, min(256, t), 256)

    topk_s = min(TOPK_MAX, (past + ts) // 4)
    pad_rows = lambda a, n: jnp.pad(a, ((0, 0), (0, n - a.shape[1]), (0, 0)))
    pps = 8 if n_pages % 8 == 0 else 1
    pad_lanes = lambda a: jnp.pad(a, ((0, 0), (0, 0), (0, LANES - a.shape[2])))
    qit = pad_rows(qi_s.reshape(bs, ts, HI * DI), rq).reshape(bs, rq, HI, DI).transpose(0, 3, 2, 1)
    qit = pad_lanes(qit.reshape(bs, DI, HI * rq))
    wif = pad_rows(p_s[:, C_WI:C_WI + HI].reshape(bs, ts, HI), rq).transpose(0, 2, 1).reshape(bs, 1, HI * rq)
    wif = pad_lanes(wif)
    kinew = pad_rows(ki_s.reshape(bs, ts, DI), PAGE)
    skey_past, skey_new = _sample_scores(qit, wif, cache_kidx[0], kinew, page_table, ts, rq, pps)
    bias_s = _sample_select(skey_past.reshape(bs * rq, past), skey_new.reshape(bs * rq, PAGE), topk_s,
                            min(128, bs * rq))
    bias_s = bias_s.reshape(bs, rq, past + PAGE)
    v_s = p_s[:, C_VB:C_VB + DB]
    knew = pad_rows(k_s.reshape(bs, ts, DB), PAGE)
    vnew = pad_rows(v_s.reshape(bs, ts, DB), PAGE)
    qh = pad_rows(q_s.reshape(bs, ts, DB), rq).reshape(bs, rq, HB, DHB).transpose(0, 2, 3, 1)
    head_eq = jnp.eye(HB, dtype=bool)[None, :, None, :, None]
    qbd = jnp.where(head_eq, qh[:, :, :, None, :], jnp.zeros((), BF16)).reshape(bs, DB, HB * rq)
    qbd = jnp.pad(qbd, ((0, 0), (0, 0), (0, LANES - HB * rq)))
    yb_s = _sample_attn(qbd, cache_k[0], cache_v[0], knew, vnew, bias_s, page_table, rq, pps)
    yb_s = yb_s[:, :ts].reshape(bs * ts, DB)

    yc_p = _cross_attn(p_p[None], C_QC, mk_p, mv_p, min(512, t))[0]
    qc8 = pad_rows(p_s[:, C_QC:C_QC + DC].reshape(bs, ts, DC), rq)
    yc_s = _cross_attn(qc8, 0, cache_mem_k[0], cache_mem_v[0], rq)
    yc_s = yc_s[:, :ts].reshape(bs * ts, DC)

    y_p = _merge(xp2, p_p, ya_p[0], yb_p, yc_p, b_gate[0], wa, wb, wc, wo, final_norm_g, min(128, t))
    y_s = _merge(xs2, p_s, ya_s, yb_s, yc_s, b_gate[0], wa, wb, wc, wo, final_norm_g, bs * ts)

    p_s3 = p_s.reshape(bs, ts, N_P)
    shift_s = jnp.concatenate([p_s3[:, -1, :3 * DA], p_s3[:, -1, C_WDAD:C_WDAD + 2 * LORA]], axis=-1)
    return (y_p.reshape(1, t, d), y_s.reshape(bs, ts, d),
            k_p.reshape(1, 1, t, HB, DHB), p_p[:, C_VB:C_VB + DB].reshape(1, 1, t, HB, DHB),
            ki_p.reshape(1, 1, t, DI),
            mk_p.reshape(1, 1, N_MEM, HC, DHC), mv_p.reshape(1, 1, N_MEM, HC, DHC),
            wkv_p.reshape(1, 1, HA, DHA, DHA), _shift_row(p_p[t - 1]).reshape(1, 1, SHIFT_W),
            k_s.reshape(1, bs, ts, HB, DHB), v_s.reshape(1, bs, ts, HB, DHB), ki_s.reshape(1, bs, ts, DI),
            wkv_s.reshape(1, bs, HA, DHA, DHA), shift_s.reshape(1, bs, SHIFT_W))
```

```python
import functools

import numpy as np
import jax
import jax.numpy as jnp
from jax import lax
from jax.experimental import pallas as pl
from jax.experimental.pallas import tpu as pltpu

F32, BF16, I32 = jnp.float32, jnp.bfloat16, jnp.int32

D_MODEL = 2048
HA, DHA = 12, 64
DA = HA * DHA
LORA = 64
LNX_EPS = 64e-5
HB, DHB = 6, 128
DB = HB * DHB
HI, DI = 8, 64
TOPK_MAX = 256
N_MEM = 256
HC, DHC = 4, 128
DC = HC * DHC
ROPE_THETA = 10000.0
EPS = 1e-6
PAGE = 128
SHIFT_W = 3 * DA + 2 * LORA
SPLIT_SIZES = (SHIFT_W, DA, DB, DB, DB, HI * DI, HI, DI, DB, DC, DC, 3 * D_MODEL)

LANES = 128
SUBLANES = 8
N_PAIR = HA // 2

C_RKV, C_ZA, C_QB, C_KB, C_VB, C_ZB = 0, 2304, 3072, 3840, 4608, 5376
C_QI, C_QC, C_ZC, C_WDAD, C_KI, C_WI, C_GATES = 6144, 6656, 7168, 7680, 7808, 7936, 8192
N_P = C_GATES + 3 * D_MODEL

NEG_BIAS = -1e30
ORD_NEG_INF = 0x007FFFFF
INT_MIN = -2147483648
VMEM_LIMIT = 56 * 1024 * 1024


def _cparams(*sem):
    return pltpu.CompilerParams(dimension_semantics=sem, vmem_limit_bytes=VMEM_LIMIT)


def _dot(a, b):
    return jnp.dot(a, b, preferred_element_type=F32)


def _dot_nt(a, b):
    return lax.dot_general(a, b, (((1,), (1,)), ((), ())), preferred_element_type=F32)


def _split2(x):
    hi = x.astype(BF16)
    lo = (x - hi.astype(F32)).astype(BF16)
    return hi, lo


def _dot_ones(x, ones_bf):
    hi = x.astype(BF16)
    r1 = x - hi.astype(F32)
    mid = r1.astype(BF16)
    lo = (r1 - mid.astype(F32)).astype(BF16)
    return _dot(hi, ones_bf) + _dot(mid, ones_bf) + _dot(lo, ones_bf)


def _dot3(a, b):
    a_hi, a_lo = _split2(a)
    b_hi, b_lo = _split2(b)
    return _dot(a_hi, b_hi) + _dot(a_hi, b_lo) + _dot(a_lo, b_hi)


def _sigmoid(x):
    return 1.0 / (1.0 + jnp.exp(-x))


def _silu(x):
    return x * _sigmoid(x)


def _proj_kernel(x_ref, g_ref, w_ref, o_ref, xn_ref):
    @pl.when(pl.program_id(1) == 0)
    def _():
        x = x_ref[...]
        ms = jnp.mean(x * x, axis=-1, keepdims=True)
        xn_ref[...] = (x * lax.rsqrt(ms + EPS) * g_ref[...]).astype(BF16)

    o_ref[...] = _dot(xn_ref[...], w_ref[...])


def _proj(x2d, g, w_bf, tm, tn):
    m, d = x2d.shape
    n = w_bf.shape[1]
    return pl.pallas_call(
        _proj_kernel,
        grid=(m // tm, n // tn),
        in_specs=[pl.BlockSpec((tm, d), lambda i, j: (i, 0)),
                  pl.BlockSpec((1, d), lambda i, j: (0, 0)),
                  pl.BlockSpec((d, tn), lambda i, j: (0, j))],
        out_specs=pl.BlockSpec((tm, tn), lambda i, j: (i, j)),
        out_shape=jax.ShapeDtypeStruct((m, n), F32),
        scratch_shapes=[pltpu.VMEM((tm, d), BF16)],
        compiler_params=_cparams("parallel", "arbitrary"),
    )(x2d, g.reshape(1, d), w_bf)


def _permute_w_in(w):
    o = [int(v) for v in np.cumsum((0,) + SPLIT_SIZES)]
    za0, qb0, kb0, vb0, qi0, wi0, ki0, zb0, qc0, zc0, g0 = o[1:12]
    z = lambda n: jnp.zeros((w.shape[0], n), w.dtype)
    parts = [w[:, 0:3 * DA], w[:, za0:za0 + DA], w[:, qb0:qb0 + DB], w[:, kb0:kb0 + DB], w[:, vb0:vb0 + DB],
             w[:, zb0:zb0 + DB], w[:, qi0:qi0 + HI * DI], w[:, qc0:qc0 + DC], w[:, zc0:zc0 + DC],
             w[:, 3 * DA:SHIFT_W], w[:, ki0:ki0 + DI], z(LANES - DI), w[:, wi0:wi0 + HI], z(LANES - HI), z(LANES),
             w[:, g0:g0 + 3 * D_MODEL]]
    out = jnp.concatenate(parts, axis=1).astype(BF16)
    assert out.shape[1] == N_P
    return out


def _prep_kernel(carry_mode, period, rkv_ref, wdad_ref, pinit_ref, mu_ref, w0_ref, wup_ref, a0_ref, aup_ref,
                 kk_ref, ka_ref, ones_ref, r_o, w_o, k_o, v_o, kn_o, b_o, carry_ref):
    tm = rkv_ref.shape[0]
    row = lax.broadcasted_iota(I32, (tm, 1), 0)
    cur = jnp.concatenate([rkv_ref[...], wdad_ref[...]], axis=1)
    rolled = pltpu.roll(cur, 1, 0)
    if carry_mode:
        @pl.when(pl.program_id(0) == 0)
        def _():
            carry_ref[...] = pinit_ref[...]
        prev = jnp.where(row == 0, carry_ref[...], rolled)
        carry_ref[...] = cur[tm - 1:tm, :]
    else:
        prev = jnp.where((row & (period - 1)) == 0, pinit_ref[...], rolled)
    m = cur + mu_ref[...] * (prev - cur)
    r = m[:, 0:DA]
    k = m[:, DA:2 * DA]
    v = m[:, 2 * DA:3 * DA]
    wd = m[:, 3 * DA:3 * DA + LORA]
    ad = m[:, 3 * DA + LORA:SHIFT_W]
    lw = w0_ref[...] + _dot3(jnp.tanh(wd), wup_ref[...])
    z = -lw
    softplus = jnp.maximum(z, 0.0) + jnp.log(1.0 + jnp.exp(-jnp.abs(z)))
    logw = -softplus - 0.5
    decay = jnp.exp(-jnp.exp(logw))
    a = _sigmoid(a0_ref[...] + _dot3(ad, aup_ref[...]))
    kk = k * kk_ref[...]
    ss = _dot_ones(kk * kk, ones_ref[...])
    kn = kk * lax.rsqrt(ss + 1e-12)
    r_o[...] = r
    w_o[...] = decay
    k_o[...] = k * (1.0 + (a - 1.0) * ka_ref[...])
    v_o[...] = v
    kn_o[...] = kn
    b_o[...] = kn * a


def _rwkv_prep(p, pinit, rw, ones_da, tm, carry_mode, period=1):
    t = p.shape[0]
    mu, w0, w_up, a0, a_up, k_k, k_a = rw[:7]
    row = lambda v: v.reshape(1, -1)
    full = lambda shape: pl.BlockSpec(shape, lambda i: (0, 0))
    pin_spec = full((1, SHIFT_W)) if carry_mode else pl.BlockSpec((tm, SHIFT_W), lambda i: (i, 0))
    out = jax.ShapeDtypeStruct((t, DA), F32)
    return pl.pallas_call(
        functools.partial(_prep_kernel, carry_mode, period),
        grid=(t // tm,),
        in_specs=[pl.BlockSpec((tm, 3 * DA), lambda i: (i, 0)),
                  pl.BlockSpec((tm, LANES), lambda i: (i, C_WDAD // LANES)),
                  pin_spec, full((1, SHIFT_W)), full((1, DA)), full((LORA, DA)), full((1, DA)), full((LORA, DA)),
                  full((1, DA)), full((1, DA)), full((DA, DA))],
        out_specs=[pl.BlockSpec((tm, DA), lambda i: (i, 0))] * 6,
        out_shape=[out] * 6,
        scratch_shapes=[pltpu.VMEM((1, SHIFT_W), F32)],
        compiler_params=_cparams("arbitrary"),
    )(p, p, pinit, row(mu), row(w0), w_up, row(a0), a_up, row(k_k), row(k_a), ones_da)


def _seg_sum_bcast(x, ones2_ref):
    hi = x.astype(BF16)
    lo = (x - hi.astype(F32)).astype(BF16)
    return _dot(jnp.concatenate([hi, lo], axis=1), ones2_ref[...])


def _scan_kernel(n_tok, r_ref, w_ref, k_ref, v_ref, kn_ref, b_ref, s0_ref, rk_ref, g_ref, beta_ref, ones_ref,
                 ones2_ref, y_o, s_o, s_ref, y_ref):
    c = pl.program_id(1)

    @pl.when(c == 0)
    def _():
        s_ref[...] = s0_ref[...]

    lane = lax.broadcasted_iota(I32, (DHA, LANES), 1)
    sub = lax.broadcasted_iota(I32, (DHA, LANES), 0)
    diag = (lane & (DHA - 1)) == sub
    diag_bf = jnp.where(diag, 1.0, 0.0).astype(BF16)
    lo_lanes = lane < DHA
    lo_row = lax.broadcasted_iota(I32, (SUBLANES, LANES), 1) < DHA
    pairs = range(N_PAIR)

    def on_diag(row):
        packed = jnp.broadcast_to(row, (2 * SUBLANES, LANES)).astype(BF16)
        return jnp.concatenate([packed] * (DHA // (2 * SUBLANES)), axis=0) * diag_bf

    def group(base, n):
        rows8 = pl.ds(base, SUBLANES)
        tile = lambda ref, p: ref[rows8, p * LANES:(p + 1) * LANES]
        v_hi = [tile(v_ref, p).astype(BF16).astype(F32) for p in pairs]
        v_lo = [tile(v_ref, p) - v_hi[p] for p in pairs]
        kn_a = [jnp.where(lo_row, tile(kn_ref, p), 0.0) for p in pairs]
        kn_b = [jnp.where(lo_row, 0.0, tile(kn_ref, p)) for p in pairs]
        y_rows = [[] for _ in pairs]
        for u in range(n):
            row = slice(u, u + 1)
            vb = [_dot(jnp.concatenate([on_diag(v_hi[p][row]), on_diag(v_lo[p][row])], axis=1), ones2_ref[...])
                  for p in pairs]
            sa = []
            for p in pairs:
                s = s_ref[p]
                sa.append(jnp.where(lo_lanes, jnp.sum(s * kn_a[p][row], axis=1, keepdims=True),
                                    jnp.sum(s * kn_b[p][row], axis=1, keepdims=True)))
            for p in pairs:
                s_ref[p] = (s_ref[p] * tile(w_ref, p)[row] - sa[p] * tile(b_ref, p)[row]
                            + vb[p] * tile(k_ref, p)[row])
            yb = [_seg_sum_bcast(s_ref[p] * tile(r_ref, p)[row], ones2_ref) for p in pairs]
            for p in pairs:
                y_rows[p].append(jnp.sum(jnp.where(diag, yb[p], 0.0), axis=0, keepdims=True))
        for p in pairs:
            rows = y_rows[p] + [jnp.zeros((1, LANES), F32)] * (SUBLANES - n)
            y_ref[rows8, p * LANES:(p + 1) * LANES] = jnp.concatenate(rows, axis=0)

    n_full = n_tok // SUBLANES

    def full_group(g, carry):
        group(pl.multiple_of(g * SUBLANES, SUBLANES), SUBLANES)
        return carry

    lax.fori_loop(0, n_full, full_group, 0)
    if n_tok % SUBLANES:
        group(n_full * SUBLANES, n_tok % SUBLANES)

    ones = ones_ref[...]
    y = y_ref[...]
    mean = _dot_ones(y, ones) * (1.0 / DHA)
    yc = y - mean
    var = _dot_ones(yc * yc, ones) * (1.0 / DHA)
    yn = yc * lax.rsqrt(var + LNX_EPS) * g_ref[...] + beta_ref[...]
    bonus = _dot_ones(r_ref[...] * k_ref[...] * rk_ref[...], ones) * v_ref[...]
    y_o[...] = yn + bonus

    @pl.when(c == pl.num_programs(1) - 1)
    def _():
        s_o[...] = s_ref[...]


def _pack_state(s):
    b = s.shape[0]
    return s.reshape(b, N_PAIR, 2, DHA, DHA).transpose(0, 1, 3, 2, 4).reshape(b, N_PAIR, DHA, LANES)


def _unpack_state(s):
    b = s.shape[0]
    return s.reshape(b, N_PAIR, DHA, 2, DHA).transpose(0, 1, 3, 2, 4).reshape(b, HA, DHA, DHA)


def _rwkv_scan(seqs, s0, rw, ones_da, tc, n_tok):
    bsz, t, _ = seqs[0].shape
    assert tc == -(-n_tok // SUBLANES) * SUBLANES
    r_k, lnx_g, lnx_b = rw[7:10]
    row = lambda v: v.reshape(1, DA)
    seq_spec = pl.BlockSpec((None, tc, DA), lambda b, c: (b, c, 0))
    st_spec = pl.BlockSpec((None, N_PAIR, DHA, LANES), lambda b, c: (b, 0, 0, 0))
    full = lambda shape: pl.BlockSpec(shape, lambda b, c: (0, 0))
    y, s_fin = pl.pallas_call(
        functools.partial(_scan_kernel, n_tok),
        grid=(bsz, t // tc),
        in_specs=[seq_spec] * 6 + [st_spec, full((1, DA)), full((1, DA)), full((1, DA)), full((DA, DA)),
                                   full((2 * LANES, LANES))],
        out_specs=[seq_spec, st_spec],
        out_shape=[jax.ShapeDtypeStruct((bsz, t, DA), F32), jax.ShapeDtypeStruct((bsz, N_PAIR, DHA, LANES), F32)],
        scratch_shapes=[pltpu.VMEM((N_PAIR, DHA, LANES), F32), pltpu.VMEM((tc, DA), F32)],
        compiler_params=_cparams("parallel", "arbitrary"),
    )(*seqs, _pack_state(s0), row(r_k), row(lnx_g), row(lnx_b), ones_da,
      jnp.concatenate([ones_da[:LANES, :LANES]] * 2, axis=0))
    return y, _unpack_state(s_fin)


def _rope_kernel(qb_ref, kb_ref, vb_ref, qi_ref, ki_ref, c128_ref, s128_ref, c64_ref, s64_ref,
                 q_o, k_o, v_o, kbf_o, vbf_o, qi_o, ki_o, kcat_o):
    c128, s128, c64, s64 = c128_ref[...], s128_ref[...], c64_ref[...], s64_ref[...]
    lane = lax.broadcasted_iota(I32, c64.shape, 1)
    first_half = (lane & (DI - 1)) < DI // 2

    def rope128(x):
        return x * c128 + pltpu.roll(x, DHB // 2, 1) * s128

    def rope64(x):
        partner = jnp.where(first_half, pltpu.roll(x, LANES - DI // 2, 1), pltpu.roll(x, DI // 2, 1))
        return x * c64 + partner * s64

    for h in range(HB):
        sl = slice(h * DHB, (h + 1) * DHB)
        q_o[:, sl] = rope128(qb_ref[:, sl]).astype(BF16)
        kr = rope128(kb_ref[:, sl])
        k_o[h] = kr
        v_o[h] = vb_ref[:, sl]
        kbf_o[:, sl] = kr.astype(BF16)
    vbf_o[...] = vb_ref[...].astype(BF16)
    for h2 in range(HI * DI // LANES):
        sl = slice(h2 * LANES, (h2 + 1) * LANES)
        qi_o[:, sl] = rope64(qi_ref[:, sl])
    ki = rope64(ki_ref[...])[:, :DI]
    ki_o[...] = ki
    kcat_o[...] = _index_key_cat(ki)


def _rope_tables(pos, d):
    inv = ROPE_THETA ** (-jnp.arange(0, d, 2, dtype=F32) / d)
    ang = pos.astype(F32)[:, None] * inv[None, :]
    cos, sin = jnp.cos(ang), jnp.sin(ang)
    reps = LANES // d
    c = jnp.tile(jnp.concatenate([cos, cos], axis=1), (1, reps))
    s = jnp.tile(jnp.concatenate([-sin, sin], axis=1), (1, reps))
    return c, s


def _rope(p, pos, tm):
    t = p.shape[0]
    c128, s128 = _rope_tables(pos, DHB)
    c64, s64 = _rope_tables(pos, DI)
    col = lambda w, c0: pl.BlockSpec((tm, w), lambda i: (i, c0 // w))
    rows = lambda w: pl.BlockSpec((tm, w), lambda i: (i, 0))
    sds = lambda w, dt: jax.ShapeDtypeStruct((t, w), dt)
    heads = pl.BlockSpec((HB, tm, DHB), lambda i: (0, i, 0))
    hsds = jax.ShapeDtypeStruct((HB, t, DHB), F32)
    return pl.pallas_call(
        _rope_kernel,
        grid=(t // tm,),
        in_specs=[col(DB, C_QB), col(DB, C_KB), col(DB, C_VB), col(HI * DI, C_QI), col(LANES, C_KI),
                  rows(LANES), rows(LANES), rows(LANES), rows(LANES)],
        out_specs=[rows(DB), heads, heads, rows(DB), rows(DB), rows(HI * DI), rows(DI), rows(4 * DI)],
        out_shape=[sds(DB, BF16), hsds, hsds, sds(DB, BF16), sds(DB, BF16), sds(HI * DI, F32), sds(DI, F32),
                   sds(4 * DI, BF16)],
        compiler_params=_cparams("parallel"),
    )(p, p, p, p, p, c128, s128, c64, s64)


def _canonical(score):
    return jnp.where(score == 0.0, 0.0, score)


def _ordinal_to_float(u):
    s = u ^ INT_MIN
    b = s ^ ((s >> 31) & 0x7FFFFFFF)
    return lax.bitcast_convert_type(b, F32)


def _index_query_cat(qi):
    qi_hi, qi_lo = _split2(qi)
    out = []
    for h in range(HI):
        sl = slice(h * DI, (h + 1) * DI)
        out.append(jnp.concatenate([qi_hi[:, sl], qi_hi[:, sl], qi_lo[:, sl], qi_lo[:, sl]], axis=1))
    return out


def _index_key_cat(ki):
    k_hi, k_lo = _split2(ki)
    return jnp.concatenate([k_hi, k_lo, k_hi, k_lo], axis=1)


def _index_scores(q_cat, wi, k_cat):
    acc = None
    for h in range(HI):
        term = wi[:, h:h + 1] * jnp.maximum(_dot_nt(q_cat[h], k_cat), 0.0)
        acc = term if acc is None else acc + term
    return acc * (HI ** -0.5 * DI ** -0.5)


def _count(skey_ref, nc, ck, pred):
    rows = skey_ref.shape[0]

    def body(c, acc):
        base = pl.multiple_of(c * ck, ck)
        for u in range(ck // LANES):
            blk = skey_ref[:, pl.ds(base + u * LANES, LANES)]
            acc = acc + jnp.where(pred(blk, base + u * LANES), 1, 0)
        return acc

    acc = lax.fori_loop(0, nc, body, jnp.zeros((rows, LANES), I32))
    return jnp.sum(acc, axis=1, keepdims=True)


def _select_bias(skey_ref, bias_ref, cut_ref, nc, nc_total, ck, topk, idx_bits):
    rows = skey_ref.shape[0]
    lane = lax.broadcasted_iota(I32, (rows, LANES), 1)

    def bit_body(it, carry):
        ans_u, done = carry

        def refine(_):
            cand_u = ans_u | lax.shift_left(jnp.int32(1), 31 - it)
            cand = _ordinal_to_float(cand_u)
            cnt = _count(skey_ref, nc, ck, lambda blk, _: blk >= cand)
            take = (cnt >= topk) & (done == 0)
            return jnp.where(take, cand_u, ans_u), jnp.where(take & (cnt == topk), 1, done)

        return lax.cond(jnp.min(done) == 0, refine, lambda _: (ans_u, done), None)

    zeros = jnp.zeros((rows, 1), I32)
    ans_u = lax.fori_loop(0, 32, bit_body, (zeros, zeros))[0]
    ans_u = jnp.where((ans_u >= 0) & (ans_u < ORD_NEG_INF), ORD_NEG_INF, ans_u)
    thr = _ordinal_to_float(ans_u)
    n_gt = _count(skey_ref, nc, ck, lambda blk, _: blk > thr)
    n_ge = _count(skey_ref, nc, ck, lambda blk, _: blk >= thr)
    need = topk - n_gt
    tied = ((n_ge - n_gt) > need) & (thr > -jnp.inf)
    cut_ref[...] = jnp.full((rows, 1), 2 ** 30, I32)

    @pl.when(jnp.max(tied.astype(I32)) > 0)
    def _():
        def tie_body(it, ans):
            cand = ans | lax.shift_left(jnp.int32(1), idx_bits - 1 - it)
            g = _count(skey_ref, nc, ck, lambda blk, base: (blk == thr) & ((lane + base) < cand))
            return jnp.where(g < need, cand, ans)

        cut = lax.fori_loop(0, idx_bits, tie_body, jnp.zeros((rows, 1), I32))
        cut_ref[...] = jnp.where(tied, cut, 2 ** 30)

    cut = cut_ref[...]

    def write(c, carry):
        base = pl.multiple_of(c * ck, ck)
        for u in range(ck // LANES):
            off = base + u * LANES
            blk = skey_ref[:, pl.ds(off, LANES)]
            sel = ((blk > thr) | ((blk == thr) & ((lane + off) <= cut))) & (blk > -jnp.inf)
            bias_ref[:, pl.ds(off, LANES)] = jnp.where(sel, 0.0, NEG_BIAS).astype(bias_ref.dtype)
        return carry

    lax.fori_loop(0, nc, write, 0)

    def fill(c, carry):
        base = pl.multiple_of(c * ck, ck)
        bias_ref[:, pl.ds(base, ck)] = jnp.full((rows, ck), NEG_BIAS, bias_ref.dtype)
        return carry

    lax.fori_loop(nc, nc_total, fill, 0)


def _prompt_select_kernel(topk, ck, idx_bits, qi_ref, wi_ref, kcat_ref, bias_o, skey_ref, cut_ref):
    i = pl.program_id(0)
    tq = qi_ref.shape[0]
    t = kcat_ref.shape[0]
    nc = ((i + 1) * tq + ck - 1) // ck
    q_cat = _index_query_cat(qi_ref[...])
    wi = wi_ref[...]
    qpos = i * tq + lax.broadcasted_iota(I32, (tq, ck), 0)
    kiota = lax.broadcasted_iota(I32, (tq, ck), 1)

    def score_chunk(c, carry):
        base = pl.multiple_of(c * ck, ck)
        sc = _index_scores(q_cat, wi, kcat_ref[pl.ds(base, ck), :])
        sc = jnp.where(kiota + base <= qpos, sc, -jnp.inf)
        skey_ref[:, pl.ds(base, ck)] = _canonical(sc)
        return carry

    lax.fori_loop(0, nc, score_chunk, 0)
    _select_bias(skey_ref, bias_o, cut_ref, nc, t // ck, ck, topk, idx_bits)


def _prompt_select(qi, p, k_cat, tq, ck):
    t = qi.shape[0]
    topk = min(TOPK_MAX, t // 4)
    idx_bits = int(np.ceil(np.log2(t))) + 1
    return pl.pallas_call(
        functools.partial(_prompt_select_kernel, topk, ck, idx_bits),
        grid=(t // tq,),
        in_specs=[pl.BlockSpec((tq, HI * DI), lambda i: (i, 0)),
                  pl.BlockSpec((tq, LANES), lambda i: (i, C_WI // LANES)),
                  pl.BlockSpec((t, 4 * DI), lambda i: (0, 0))],
        out_specs=pl.BlockSpec((tq, t), lambda i: (i, 0)),
        out_shape=jax.ShapeDtypeStruct((t, t), BF16),
        scratch_shapes=[pltpu.VMEM((tq, t), F32), pltpu.VMEM((tq, 1), I32)],
        compiler_params=_cparams("parallel"),
    )(qi, p, k_cat)


def _sample_score_kernel(pps, rq, n_new, pt_ref, qit_ref, wif_ref, *refs):
    page_refs, knew_ref, skey_o, skey_new_o = refs[:pps], refs[pps], refs[pps + 1], refs[pps + 2]
    j = pl.program_id(1)
    q_hi, q_lo = _split2(qit_ref[...])
    wif = wif_ref[...]

    def scores(kc):
        k_hi, k_lo = _split2(kc)
        st = _dot(k_hi, q_hi) + _dot(k_lo, q_hi) + _dot(k_hi, q_lo)
        at = (jnp.maximum(st, 0.0) * wif).T
        sc = at[0:rq]
        for h in range(1, HI):
            sc = sc + at[h * rq:(h + 1) * rq]
        return sc * (HI ** -0.5 * DI ** -0.5)

    sc = scores(jnp.concatenate([r[...] for r in page_refs], axis=0))
    trow = lax.broadcasted_iota(I32, sc.shape, 0)
    skey_o[...] = _canonical(jnp.where(trow < n_new, sc, -jnp.inf))

    @pl.when(j == pl.num_programs(1) - 1)
    def _():
        scn = scores(knew_ref[...])
        tr = lax.broadcasted_iota(I32, scn.shape, 0)
        ln = lax.broadcasted_iota(I32, scn.shape, 1)
        skey_new_o[...] = _canonical(jnp.where((tr < n_new) & (ln <= tr), scn, -jnp.inf))


def _sample_scores(qit, wif, cache_kidx, knew_pad, page_table, n_new, rq, pps):
    bsz, n_pages = page_table.shape
    per_b = lambda b, j, pt: (b, 0, 0)
    page_spec = lambda u: pl.BlockSpec((None, PAGE, DI), lambda b, j, pt: (pt[b, j * pps + u], 0, 0))
    grid_spec = pltpu.PrefetchScalarGridSpec(
        num_scalar_prefetch=1,
        grid=(bsz, n_pages // pps),
        in_specs=[pl.BlockSpec((None, DI, LANES), per_b), pl.BlockSpec((None, 1, LANES), per_b)]
                 + [page_spec(u) for u in range(pps)] + [pl.BlockSpec((None, PAGE, DI), per_b)],
        out_specs=[pl.BlockSpec((None, rq, pps * PAGE), lambda b, j, pt: (b, 0, j)),
                   pl.BlockSpec((None, rq, PAGE), per_b)],
    )
    return pl.pallas_call(
        functools.partial(_sample_score_kernel, pps, rq, n_new),
        grid_spec=grid_spec,
        out_shape=[jax.ShapeDtypeStruct((bsz, rq, n_pages * PAGE), F32), jax.ShapeDtypeStruct((bsz, rq, PAGE), F32)],
        compiler_params=_cparams("parallel", "arbitrary"),
    )(page_table, qit, wif, *([cache_kidx] * pps), knew_pad)


def _sample_select_kernel(topk, ck, idx_bits, skey_past, skey_new, bias_o, skey_ref, cut_ref):
    past = skey_past.shape[1]
    skey_ref[:, :past] = skey_past[...]
    skey_ref[:, past:] = skey_new[...]
    nc = skey_ref.shape[1] // ck
    _select_bias(skey_ref, bias_o, cut_ref, nc, nc, ck, topk, idx_bits)


def _sample_select(skey_past, skey_new, topk, tr):
    rows, past = skey_past.shape
    n = past + skey_new.shape[1]
    idx_bits = int(np.ceil(np.log2(n))) + 1
    return pl.pallas_call(
        functools.partial(_sample_select_kernel, topk, PAGE, idx_bits),
        grid=(rows // tr,),
        in_specs=[pl.BlockSpec((tr, past), lambda i: (i, 0)), pl.BlockSpec((tr, n - past), lambda i: (i, 0))],
        out_specs=pl.BlockSpec((tr, n), lambda i: (i, 0)),
        out_shape=jax.ShapeDtypeStruct((rows, n), F32),
        scratch_shapes=[pltpu.VMEM((tr, n), F32), pltpu.VMEM((tr, 1), I32)],
        compiler_params=_cparams("parallel"),
    )(skey_past, skey_new)


def _prompt_attn_kernel(ck, q_ref, k_ref, v_ref, bias_ref, o_ref, m_ref, l_ref, acc_ref):
    i = pl.program_id(0)
    tq = q_ref.shape[0]
    nc = ((i + 1) * tq + ck - 1) // ck
    m_ref[...] = jnp.full(m_ref.shape, NEG_BIAS, F32)
    l_ref[...] = jnp.zeros(l_ref.shape, F32)
    acc_ref[...] = jnp.zeros(acc_ref.shape, F32)
    scale = DHB ** -0.5

    def chunk(c, carry):
        base = pl.multiple_of(c * ck, ck)
        bias = bias_ref[:, pl.ds(base, ck)].astype(F32)
        for h in range(HB):
            sl = slice(h * DHB, (h + 1) * DHB)
            s = _dot_nt(q_ref[:, sl], k_ref[pl.ds(base, ck), sl]) * scale + bias
            m_old = m_ref[h]
            m_new = jnp.maximum(m_old, jnp.max(s, axis=1, keepdims=True))
            alpha = jnp.exp(m_old - m_new)
            p = jnp.exp(s - m_new)
            l_ref[h] = alpha * l_ref[h] + jnp.sum(p, axis=1, keepdims=True)
            acc_ref[h] = alpha * acc_ref[h] + _dot(p.astype(BF16), v_ref[pl.ds(base, ck), sl])
            m_ref[h] = m_new
        return carry

    lax.fori_loop(0, nc, chunk, 0)
    for h in range(HB):
        o_ref[:, h * DHB:(h + 1) * DHB] = acc_ref[h] / l_ref[h]


def _prompt_attn(q_bf, k_bf, v_bf, bias, tq, ck):
    t = q_bf.shape[0]
    resident = lambda: pl.BlockSpec((t, DB), lambda i: (0, 0), pipeline_mode=pl.Buffered(1))
    return pl.pallas_call(
        functools.partial(_prompt_attn_kernel, ck),
        grid=(t // tq,),
        in_specs=[pl.BlockSpec((tq, DB), lambda i: (i, 0)), resident(), resident(),
                  pl.BlockSpec((tq, t), lambda i: (i, 0))],
        out_specs=pl.BlockSpec((tq, DB), lambda i: (i, 0)),
        out_shape=jax.ShapeDtypeStruct((t, DB), F32),
        scratch_shapes=[pltpu.VMEM((HB, tq, 1), F32), pltpu.VMEM((HB, tq, 1), F32), pltpu.VMEM((HB, tq, DHB), F32)],
        compiler_params=_cparams("parallel"),
    )(q_bf, k_bf, v_bf, bias)


def _sample_attn_kernel(pps, rq, pt_ref, qbd_ref, *refs):
    kpages, vpages = refs[:pps], refs[pps:2 * pps]
    knew_ref, vnew_ref, bias_ref, bias_new_ref, o_ref, m_ref, l_ref, acc_ref = refs[2 * pps:]
    j = pl.program_id(1)
    nr = HB * rq

    @pl.when(j == 0)
    def _():
        m_ref[...] = jnp.full(m_ref.shape, NEG_BIAS, F32)
        l_ref[...] = jnp.zeros(l_ref.shape, F32)
        acc_ref[...] = jnp.zeros(acc_ref.shape, F32)

    def accumulate(kp, vp, bias):
        st = _dot(kp, qbd_ref[...])
        s = st.T[:nr] * (DHB ** -0.5) + jnp.concatenate([bias] * HB, axis=0)
        m_old = m_ref[...]
        m_new = jnp.maximum(m_old, jnp.max(s, axis=1, keepdims=True))
        alpha = jnp.exp(m_old - m_new)
        p = jnp.exp(s - m_new)
        l_ref[...] = alpha * l_ref[...] + jnp.sum(p, axis=1, keepdims=True)
        acc_ref[...] = alpha * acc_ref[...] + _dot(p.astype(BF16), vp)
        m_ref[...] = m_new

    gather = lambda pages: jnp.concatenate(
        [jnp.concatenate([r[h].astype(BF16) for h in range(HB)], axis=1) for r in pages], axis=0)
    accumulate(gather(kpages), gather(vpages), bias_ref[...])

    @pl.when(j == pl.num_programs(1) - 1)
    def _():
        accumulate(knew_ref[...].astype(BF16), vnew_ref[...].astype(BF16), bias_new_ref[...])
        for h in range(HB):
            rs = slice(h * rq, (h + 1) * rq)
            cs = slice(h * DHB, (h + 1) * DHB)
            o_ref[:, cs] = acc_ref[rs, cs] / l_ref[rs, :]


def _sample_attn(qbd, cache_k, cache_v, knew_pad, vnew_pad, bias, page_table, rq, pps):
    bsz, n_pages = page_table.shape
    page_spec = lambda u: pl.BlockSpec((None, HB, PAGE, DHB), lambda b, j, pt: (pt[b, j * pps + u], 0, 0, 0))
    per_b = lambda b, j, pt: (b, 0, 0)
    grid_spec = pltpu.PrefetchScalarGridSpec(
        num_scalar_prefetch=1,
        grid=(bsz, n_pages // pps),
        in_specs=[pl.BlockSpec((None, DB, LANES), per_b)]
                 + [page_spec(u) for u in range(pps)] * 2
                 + [pl.BlockSpec((None, PAGE, DB), per_b), pl.BlockSpec((None, PAGE, DB), per_b),
                    pl.BlockSpec((None, rq, pps * PAGE), lambda b, j, pt: (b, 0, j)),
                    pl.BlockSpec((None, rq, PAGE), lambda b, j, pt: (b, 0, n_pages))],
        out_specs=pl.BlockSpec((None, rq, DB), per_b),
        scratch_shapes=[pltpu.VMEM((HB * rq, 1), F32), pltpu.VMEM((HB * rq, 1), F32), pltpu.VMEM((HB * rq, DB), F32)],
    )
    return pl.pallas_call(
        functools.partial(_sample_attn_kernel, pps, rq),
        grid_spec=grid_spec,
        out_shape=jax.ShapeDtypeStruct((bsz, rq, DB), F32),
        compiler_params=_cparams("parallel", "arbitrary"),
    )(page_table, qbd, *([cache_k] * pps), *([cache_v] * pps), knew_pad, vnew_pad, bias, bias)


def _cross_kernel(q_ref, mk_ref, mv_ref, o_ref):
    scale = DHC ** -0.5
    for h in range(HC):
        sl = slice(h * DHC, (h + 1) * DHC)
        s = _dot_nt(q_ref[:, sl].astype(BF16), mk_ref[h].astype(BF16)) * scale
        m = jnp.max(s, axis=1, keepdims=True)
        p = jnp.exp(s - m)
        l = jnp.sum(p, axis=1, keepdims=True)
        o_ref[:, sl] = _dot(p.astype(BF16), mv_ref[h].astype(BF16)) / l


def _cross_attn(q_arr, q_col, mk, mv, tm):
    bsz, t, _ = q_arr.shape
    return pl.pallas_call(
        _cross_kernel,
        grid=(bsz, t // tm),
        in_specs=[pl.BlockSpec((None, tm, DC), lambda b, i: (b, i, q_col // DC)),
                  pl.BlockSpec((None, HC, N_MEM, DHC), lambda b, i: (b, 0, 0, 0)),
                  pl.BlockSpec((None, HC, N_MEM, DHC), lambda b, i: (b, 0, 0, 0))],
        out_specs=pl.BlockSpec((None, tm, DC), lambda b, i: (b, i, 0)),
        out_shape=jax.ShapeDtypeStruct((bsz, t, DC), F32),
        compiler_params=_cparams("parallel", "parallel"),
    )(q_arr, mk, mv)


def _merge_kernel(x_ref, ya_ref, za_ref, yb_ref, zb_ref, yc_ref, zc_ref, ga_ref, gb_ref, gc_ref,
                  bga_ref, bgb_ref, bgc_ref, wa_ref, wb_ref, wc_ref, wo_ref, fg_ref, o_ref):
    ba = _dot((ya_ref[...] * _silu(za_ref[...])).astype(BF16), wa_ref[...])
    bb = _dot((yb_ref[...] * _silu(zb_ref[...])).astype(BF16), wb_ref[...])
    bc = _dot((yc_ref[...] * _silu(zc_ref[...])).astype(BF16), wc_ref[...])
    mix = (_sigmoid(ga_ref[...] + bga_ref[...]) * ba + _sigmoid(gb_ref[...] + bgb_ref[...]) * bb
           + _sigmoid(gc_ref[...] + bgc_ref[...]) * bc)
    y = x_ref[...] + _dot(mix.astype(BF16), wo_ref[...])
    ms = jnp.mean(y * y, axis=-1, keepdims=True)
    o_ref[...] = y * lax.rsqrt(ms + EPS) * fg_ref[...]


def _merge(x2d, p, ya, yb, yc, b_gate, wa, wb, wc, wo, fg, tm):
    t, d = x2d.shape
    rows = lambda w: pl.BlockSpec((tm, w), lambda i: (i, 0))
    col = lambda w, c0: pl.BlockSpec((tm, w), lambda i: (i, c0 // w))
    const = lambda shape, c=0: pl.BlockSpec(shape, lambda i: (0, c), pipeline_mode=pl.Buffered(1))
    bg = b_gate.reshape(1, 3 * d)
    return pl.pallas_call(
        _merge_kernel,
        grid=(t // tm,),
        in_specs=[rows(d), rows(DA), col(DA, C_ZA), rows(DB), col(DB, C_ZB), rows(DC), col(DC, C_ZC),
                  col(d, C_GATES), col(d, C_GATES + d), col(d, C_GATES + 2 * d),
                  const((1, d), 0), const((1, d), 1), const((1, d), 2),
                  const((DA, d)), const((DB, d)), const((DC, d)), const((d, d)), const((1, d))],
        out_specs=rows(d),
        out_shape=jax.ShapeDtypeStruct((t, d), F32),
        compiler_params=_cparams("parallel"),
    )(x2d, ya, p, yb, p, yc, p, p, p, p, bg, bg, bg, wa, wb, wc, wo, fg.reshape(1, d))


def _block_ones(n, seg):
    idx = np.arange(n) // seg
    return jnp.asarray(idx[:, None] == idx[None, :], dtype=BF16)


def _shift_row(p_row):
    return jnp.concatenate([p_row[:3 * DA], p_row[C_WDAD:C_WDAD + 2 * LORA]])


def kernel(x_prompt, x_sample, mem_prompt, cache_k, cache_v, cache_kidx, cache_mem_k, cache_mem_v, state_wkv, state_shift, page_table, norm_g, w_in, rwkv_mu, rwkv_w0, rwkv_w_up, rwkv_a0, rwkv_a_up, rwkv_k_k, rwkv_k_a, rwkv_r_k, rwkv_lnx_g, rwkv_lnx_b, mem_norm_g, w_mem_kv, b_gate, w_br_a, w_br_b, w_br_c, w_out, final_norm_g):
    depth = w_in.shape[0]
    assert depth == 1 and x_prompt.shape[0] == 1
    t = x_prompt.shape[1]
    bs, ts, d = x_sample.shape
    n_pages = page_table.shape[1]
    past = n_pages * PAGE
    rq = 8
    assert ts <= rq and (ts & (ts - 1)) == 0

    rw = (rwkv_mu[0], rwkv_w0[0], rwkv_w_up[0], rwkv_a0[0], rwkv_a_up[0], rwkv_k_k[0], rwkv_k_a[0],
          rwkv_r_k[0].reshape(DA), rwkv_lnx_g[0], rwkv_lnx_b[0])
    ones_da = _block_ones(DA, DHA)
    w_p = _permute_w_in(w_in[0])
    wa, wb, wc, wo = (w[0].astype(BF16) for w in (w_br_a, w_br_b, w_br_c, w_out))

    xp2 = x_prompt.reshape(t, d)
    xs2 = x_sample.reshape(bs * ts, d)
    tm_p = min(512, t)
    p_p = _proj(xp2, norm_g[0], w_p, tm_p, 1024)
    p_s = _proj(xs2, norm_g[0], w_p, bs * ts, 1024)
    mkv = _proj(mem_prompt.reshape(N_MEM, d), mem_norm_g[0], w_mem_kv[0].astype(BF16), N_MEM, 2 * DC)
    mk_p = mkv[:, :DC].reshape(1, N_MEM, HC, DHC)
    mv_p = mkv[:, DC:].reshape(1, N_MEM, HC, DHC)
    head_major = lambda a: jnp.transpose(a, (0, 2, 1, 3))

    seq_p = _rwkv_prep(p_p, jnp.zeros((1, SHIFT_W), F32), rw, ones_da, min(256, t), True)
    ya_p, wkv_p = _rwkv_scan([a[None] for a in seq_p], jnp.zeros((1, HA, DHA, DHA), F32), rw, ones_da,
                             min(256, t), min(256, t))
    pinit_s = jnp.repeat(state_shift[0], ts, axis=0)
    seq_s = _rwkv_prep(p_s, pinit_s, rw, ones_da, bs * ts, False, ts)
    pad_t = lambda a: jnp.pad(a.reshape(bs, ts, -1), ((0, 0), (0, rq - ts), (0, 0)))
    ya_s, wkv_s = _rwkv_scan([pad_t(a) for a in seq_s], state_wkv[0], rw, ones_da, rq, ts)
    ya_s = ya_s[:, :ts].reshape(bs * ts, DA)

    tokens_major = lambda a: jnp.transpose(a, (1, 0, 2))
    q_p, k_p, v_p, kbf_p, vbf_p, qi_p, ki_p, kcat_p = _rope(p_p, jnp.arange(t), min(512, t))
    pos_s = jnp.tile(past + jnp.arange(ts), bs)
    q_s, k_s, v_s, _, _, qi_s, ki_s, _ = _rope(p_s, pos_s, bs * ts)
    k_s, v_s = tokens_major(k_s), tokens_major(v_s)

    bias_p = _prompt_select(qi_p, p_p, kcat_p, min(128, t), 512)
    yb_p = _prompt_attn(q_p, kbf_p, vbf_p, bias_p, min(256, t), 1024)

    topk_s = min(TOPK_MAX, (past + ts) // 4)
    pad_rows = lambda a, n: jnp.pad(a, ((0, 0), (0, n - a.shape[1]), (0, 0)))
    pps = 8 if n_pages % 8 == 0 else 1
    pad_lanes = lambda a: jnp.pad(a, ((0, 0), (0, 0), (0, LANES - a.shape[2])))
    qit = pad_rows(qi_s.reshape(bs, ts, HI * DI), rq).reshape(bs, rq, HI, DI).transpose(0, 3, 2, 1)
    qit = pad_lanes(qit.reshape(bs, DI, HI * rq))
    wif = pad_rows(p_s[:, C_WI:C_WI + HI].reshape(bs, ts, HI), rq).transpose(0, 2, 1).reshape(bs, 1, HI * rq)
    wif = pad_lanes(wif)
    kinew = pad_rows(ki_s.reshape(bs, ts, DI), PAGE)
    skey_past, skey_new = _sample_scores(qit, wif, cache_kidx[0], kinew, page_table, ts, rq, pps)
    bias_s = _sample_select(skey_past.reshape(bs * rq, past), skey_new.reshape(bs * rq, PAGE), topk_s,
                            min(128, bs * rq))
    bias_s = bias_s.reshape(bs, rq, past + PAGE)
    knew = pad_rows(k_s.reshape(bs, ts, DB), PAGE)
    vnew = pad_rows(v_s.reshape(bs, ts, DB), PAGE)
    qh = pad_rows(q_s.reshape(bs, ts, DB), rq).reshape(bs, rq, HB, DHB).transpose(0, 2, 3, 1)
    head_eq = jnp.eye(HB, dtype=bool)[None, :, None, :, None]
    qbd = jnp.where(head_eq, qh[:, :, :, None, :], jnp.zeros((), BF16)).reshape(bs, DB, HB * rq)
    qbd = jnp.pad(qbd, ((0, 0), (0, 0), (0, LANES - HB * rq)))
    yb_s = _sample_attn(qbd, head_major(cache_k[0]), head_major(cache_v[0]), knew, vnew, bias_s, page_table,
                        rq, pps)
    yb_s = yb_s[:, :ts].reshape(bs * ts, DB)

    yc_p = _cross_attn(p_p[None], C_QC, head_major(mk_p), head_major(mv_p), min(512, t))[0]
    qc8 = pad_rows(p_s[:, C_QC:C_QC + DC].reshape(bs, ts, DC), rq)
    yc_s = _cross_attn(qc8, 0, head_major(cache_mem_k[0]), head_major(cache_mem_v[0]), rq)
    yc_s = yc_s[:, :ts].reshape(bs * ts, DC)

    y_p = _merge(xp2, p_p, ya_p[0], yb_p, yc_p, b_gate[0], wa, wb, wc, wo, final_norm_g, min(128, t))
    y_s = _merge(xs2, p_s, ya_s, yb_s, yc_s, b_gate[0], wa, wb, wc, wo, final_norm_g, bs * ts)

    p_s3 = p_s.reshape(bs, ts, N_P)
    shift_s = jnp.concatenate([p_s3[:, -1, :3 * DA], p_s3[:, -1, C_WDAD:C_WDAD + 2 * LORA]], axis=-1)
    return (y_p.reshape(1, t, d), y_s.reshape(bs, ts, d),
            tokens_major(k_p).reshape(1, 1, t, HB, DHB), tokens_major(v_p).reshape(1, 1, t, HB, DHB),
            ki_p.reshape(1, 1, t, DI),
            mk_p.reshape(1, 1, N_MEM, HC, DHC), mv_p.reshape(1, 1, N_MEM, HC, DHC),
            wkv_p.reshape(1, 1, HA, DHA, DHA), _shift_row(p_p[t - 1]).reshape(1, 1, SHIFT_W),
            k_s.reshape(1, bs, ts, HB, DHB), v_s.reshape(1, bs, ts, HB, DHB), ki_s.reshape(1, bs, ts, DI),
            wkv_s.reshape(1, bs, HA, DHA, DHA), shift_s.reshape(1, bs, SHIFT_W))
```

```python
import functools

import numpy as np
import jax
import jax.numpy as jnp
from jax import lax
from jax.experimental import pallas as pl
from jax.experimental.pallas import tpu as pltpu

F32, BF16, I32 = jnp.float32, jnp.bfloat16, jnp.int32

D_MODEL = 2048
HA, DHA = 12, 64
DA = HA * DHA
LORA = 64
LNX_EPS = 64e-5
HB, DHB = 6, 128
DB = HB * DHB
HI, DI = 8, 64
TOPK_MAX = 256
N_MEM = 256
HC, DHC = 4, 128
DC = HC * DHC
ROPE_THETA = 10000.0
EPS = 1e-6
PAGE = 128
SHIFT_W = 3 * DA + 2 * LORA
SPLIT_SIZES = (SHIFT_W, DA, DB, DB, DB, HI * DI, HI, DI, DB, DC, DC, 3 * D_MODEL)

LANES = 128
SUBLANES = 8
N_PAIR = HA // 2

C_RKV, C_ZA, C_QB, C_KB, C_VB, C_ZB = 0, 2304, 3072, 3840, 4608, 5376
C_QI, C_QC, C_ZC, C_WDAD, C_KI, C_WI, C_GATES = 6144, 6656, 7168, 7680, 7808, 7936, 8192
N_P = C_GATES + 3 * D_MODEL

NEG_BIAS = -1e30
ORD_NEG_INF = 0x007FFFFF
INT_MIN = -2147483648
VMEM_LIMIT = 56 * 1024 * 1024


def _cparams(*sem):
    return pltpu.CompilerParams(dimension_semantics=sem, vmem_limit_bytes=VMEM_LIMIT)


def _dot(a, b):
    return jnp.dot(a, b, preferred_element_type=F32)


def _dot_nt(a, b):
    return lax.dot_general(a, b, (((1,), (1,)), ((), ())), preferred_element_type=F32)


def _split2(x):
    hi = x.astype(BF16)
    lo = (x - hi.astype(F32)).astype(BF16)
    return hi, lo


def _dot_ones(x, ones_bf):
    hi = x.astype(BF16)
    r1 = x - hi.astype(F32)
    mid = r1.astype(BF16)
    lo = (r1 - mid.astype(F32)).astype(BF16)
    return _dot(hi, ones_bf) + _dot(mid, ones_bf) + _dot(lo, ones_bf)


def _dot3(a, b):
    a_hi, a_lo = _split2(a)
    b_hi, b_lo = _split2(b)
    return _dot(a_hi, b_hi) + _dot(a_hi, b_lo) + _dot(a_lo, b_hi)


def _sigmoid(x):
    return 1.0 / (1.0 + jnp.exp(-x))


def _silu(x):
    return x * _sigmoid(x)


def _proj_kernel(x_ref, g_ref, w_ref, o_ref, xn_ref):
    @pl.when(pl.program_id(1) == 0)
    def _():
        x = x_ref[...]
        ms = jnp.mean(x * x, axis=-1, keepdims=True)
        xn_ref[...] = (x * lax.rsqrt(ms + EPS) * g_ref[...]).astype(BF16)

    o_ref[...] = _dot(xn_ref[...], w_ref[...])


def _proj(x2d, g, w_bf, tm, tn):
    m, d = x2d.shape
    n = w_bf.shape[1]
    return pl.pallas_call(
        _proj_kernel,
        grid=(m // tm, n // tn),
        in_specs=[pl.BlockSpec((tm, d), lambda i, j: (i, 0)),
                  pl.BlockSpec((1, d), lambda i, j: (0, 0)),
                  pl.BlockSpec((d, tn), lambda i, j: (0, j))],
        out_specs=pl.BlockSpec((tm, tn), lambda i, j: (i, j)),
        out_shape=jax.ShapeDtypeStruct((m, n), F32),
        scratch_shapes=[pltpu.VMEM((tm, d), BF16)],
        compiler_params=_cparams("parallel", "arbitrary"),
    )(x2d, g.reshape(1, d), w_bf)


def _permute_w_in(w):
    o = [int(v) for v in np.cumsum((0,) + SPLIT_SIZES)]
    za0, qb0, kb0, vb0, qi0, wi0, ki0, zb0, qc0, zc0, g0 = o[1:12]
    z = lambda n: jnp.zeros((w.shape[0], n), w.dtype)
    parts = [w[:, 0:3 * DA], w[:, za0:za0 + DA], w[:, qb0:qb0 + DB], w[:, kb0:kb0 + DB], w[:, vb0:vb0 + DB],
             w[:, zb0:zb0 + DB], w[:, qi0:qi0 + HI * DI], w[:, qc0:qc0 + DC], w[:, zc0:zc0 + DC],
             w[:, 3 * DA:SHIFT_W], w[:, ki0:ki0 + DI], z(LANES - DI), w[:, wi0:wi0 + HI], z(LANES - HI), z(LANES),
             w[:, g0:g0 + 3 * D_MODEL]]
    out = jnp.concatenate(parts, axis=1).astype(BF16)
    assert out.shape[1] == N_P
    return out


def _prep_kernel(carry_mode, period, rkv_ref, wdad_ref, pinit_ref, mu_ref, w0_ref, wup_ref, a0_ref, aup_ref,
                 kk_ref, ka_ref, ones_ref, r_o, w_o, k_o, v_o, kn_o, b_o, carry_ref):
    tm = rkv_ref.shape[0]
    row = lax.broadcasted_iota(I32, (tm, 1), 0)
    cur = jnp.concatenate([rkv_ref[...], wdad_ref[...]], axis=1)
    rolled = pltpu.roll(cur, 1, 0)
    if carry_mode:
        @pl.when(pl.program_id(0) == 0)
        def _():
            carry_ref[...] = pinit_ref[...]
        prev = jnp.where(row == 0, carry_ref[...], rolled)
        carry_ref[...] = cur[tm - 1:tm, :]
    else:
        prev = jnp.where((row & (period - 1)) == 0, pinit_ref[...], rolled)
    m = cur + mu_ref[...] * (prev - cur)
    r = m[:, 0:DA]
    k = m[:, DA:2 * DA]
    v = m[:, 2 * DA:3 * DA]
    wd = m[:, 3 * DA:3 * DA + LORA]
    ad = m[:, 3 * DA + LORA:SHIFT_W]
    lw = w0_ref[...] + _dot3(jnp.tanh(wd), wup_ref[...])
    z = -lw
    softplus = jnp.maximum(z, 0.0) + jnp.log(1.0 + jnp.exp(-jnp.abs(z)))
    logw = -softplus - 0.5
    decay = jnp.exp(-jnp.exp(logw))
    a = _sigmoid(a0_ref[...] + _dot3(ad, aup_ref[...]))
    kk = k * kk_ref[...]
    ss = _dot_ones(kk * kk, ones_ref[...])
    kn = kk * lax.rsqrt(ss + 1e-12)
    r_o[...] = r
    w_o[...] = decay
    k_o[...] = k * (1.0 + (a - 1.0) * ka_ref[...])
    v_o[...] = v
    kn_o[...] = kn
    b_o[...] = kn * a


def _rwkv_prep(p, pinit, rw, ones_da, tm, carry_mode, period=1):
    t = p.shape[0]
    mu, w0, w_up, a0, a_up, k_k, k_a = rw[:7]
    row = lambda v: v.reshape(1, -1)
    full = lambda shape: pl.BlockSpec(shape, lambda i: (0, 0))
    pin_spec = full((1, SHIFT_W)) if carry_mode else pl.BlockSpec((tm, SHIFT_W), lambda i: (i, 0))
    out = jax.ShapeDtypeStruct((t, DA), F32)
    return pl.pallas_call(
        functools.partial(_prep_kernel, carry_mode, period),
        grid=(t // tm,),
        in_specs=[pl.BlockSpec((tm, 3 * DA), lambda i: (i, 0)),
                  pl.BlockSpec((tm, LANES), lambda i: (i, C_WDAD // LANES)),
                  pin_spec, full((1, SHIFT_W)), full((1, DA)), full((LORA, DA)), full((1, DA)), full((LORA, DA)),
                  full((1, DA)), full((1, DA)), full((DA, DA))],
        out_specs=[pl.BlockSpec((tm, DA), lambda i: (i, 0))] * 6,
        out_shape=[out] * 6,
        scratch_shapes=[pltpu.VMEM((1, SHIFT_W), F32)],
        compiler_params=_cparams("arbitrary"),
    )(p, p, pinit, row(mu), row(w0), w_up, row(a0), a_up, row(k_k), row(k_a), ones_da)


def _seg_sum_bcast(x, ones2_ref):
    hi = x.astype(BF16)
    lo = (x - hi.astype(F32)).astype(BF16)
    return _dot(jnp.concatenate([hi, lo], axis=1), ones2_ref[...])


def _scan_kernel(n_tok, r_ref, w_ref, k_ref, v_ref, kn_ref, b_ref, s0_ref, rk_ref, g_ref, beta_ref, ones_ref,
                 ones2_ref, y_o, s_o, s_ref, y_ref):
    c = pl.program_id(1)

    @pl.when(c == 0)
    def _():
        s_ref[...] = s0_ref[...]

    lane = lax.broadcasted_iota(I32, (DHA, LANES), 1)
    sub = lax.broadcasted_iota(I32, (DHA, LANES), 0)
    diag = (lane & (DHA - 1)) == sub
    diag_bf = jnp.where(diag, 1.0, 0.0).astype(BF16)
    lo_lanes = lane < DHA
    lo_row = lax.broadcasted_iota(I32, (SUBLANES, LANES), 1) < DHA
    pairs = range(N_PAIR)

    def on_diag(row):
        packed = jnp.broadcast_to(row, (2 * SUBLANES, LANES)).astype(BF16)
        return jnp.concatenate([packed] * (DHA // (2 * SUBLANES)), axis=0) * diag_bf

    def group(base, n):
        rows8 = pl.ds(base, SUBLANES)
        tile = lambda ref, p: ref[rows8, p * LANES:(p + 1) * LANES]
        v_hi = [tile(v_ref, p).astype(BF16).astype(F32) for p in pairs]
        v_lo = [tile(v_ref, p) - v_hi[p] for p in pairs]
        kn_a = [jnp.where(lo_row, tile(kn_ref, p), 0.0) for p in pairs]
        kn_b = [jnp.where(lo_row, 0.0, tile(kn_ref, p)) for p in pairs]
        kn_next = [pltpu.roll(tile(kn_ref, p), SUBLANES - 1, 0) for p in pairs]
        c_b = [_seg_sum_bcast(tile(b_ref, p) * kn_next[p], ones2_ref) for p in pairs]
        c_k = [_seg_sum_bcast(tile(k_ref, p) * kn_next[p], ones2_ref) for p in pairs]
        y_rows = [[] for _ in pairs]
        half_sums = lambda x, a, b: jnp.where(lo_lanes, jnp.sum(x * a, axis=1, keepdims=True),
                                              jnp.sum(x * b, axis=1, keepdims=True))
        for u in range(0, n, 2):
            r0, r1 = slice(u, u + 1), slice(u + 1, u + 2)
            vb = [[_dot(jnp.concatenate([on_diag(v_hi[p][r]), on_diag(v_lo[p][r])], axis=1), ones2_ref[...])
                   for r in (r0, r1)] for p in pairs]
            sa0, sa1 = [], []
            for p in pairs:
                s = s_ref[p]
                w0 = tile(w_ref, p)[r0]
                sa0.append(half_sums(s, kn_a[p][r0], kn_b[p][r0]))
                sa1.append(half_sums(s, w0 * kn_a[p][r1], w0 * kn_b[p][r1]))
            yb = []
            for p in pairs:
                fix = sa1[p] - sa0[p] * c_b[p][r0] + vb[p][0] * c_k[p][r0]
                s_mid = (s_ref[p] * tile(w_ref, p)[r0] - sa0[p] * tile(b_ref, p)[r0]
                         + vb[p][0] * tile(k_ref, p)[r0])
                s_new = s_mid * tile(w_ref, p)[r1] - fix * tile(b_ref, p)[r1] + vb[p][1] * tile(k_ref, p)[r1]
                s_ref[p] = s_new
                yb.append((_dot((s_mid * tile(r_ref, p)[r0]).astype(BF16), ones2_ref[:LANES, :]),
                           _dot((s_new * tile(r_ref, p)[r1]).astype(BF16), ones2_ref[:LANES, :])))
            for p in pairs:
                for y in yb[p]:
                    y_rows[p].append(jnp.sum(jnp.where(diag, y, 0.0), axis=0, keepdims=True))
        for p in pairs:
            rows = y_rows[p] + [jnp.zeros((1, LANES), F32)] * (SUBLANES - n)
            y_ref[rows8, p * LANES:(p + 1) * LANES] = jnp.concatenate(rows, axis=0)

    n_full = n_tok // SUBLANES

    def full_group(g, carry):
        group(pl.multiple_of(g * SUBLANES, SUBLANES), SUBLANES)
        return carry

    lax.fori_loop(0, n_full, full_group, 0)
    if n_tok % SUBLANES:
        group(n_full * SUBLANES, n_tok % SUBLANES)

    ones = ones_ref[...]
    y = y_ref[...]
    mean = _dot_ones(y, ones) * (1.0 / DHA)
    yc = y - mean
    var = _dot_ones(yc * yc, ones) * (1.0 / DHA)
    yn = yc * lax.rsqrt(var + LNX_EPS) * g_ref[...] + beta_ref[...]
    bonus = _dot_ones(r_ref[...] * k_ref[...] * rk_ref[...], ones) * v_ref[...]
    y_o[...] = yn + bonus

    @pl.when(c == pl.num_programs(1) - 1)
    def _():
        s_o[...] = s_ref[...]


def _pack_state(s):
    b = s.shape[0]
    return s.reshape(b, N_PAIR, 2, DHA, DHA).transpose(0, 1, 3, 2, 4).reshape(b, N_PAIR, DHA, LANES)


def _unpack_state(s):
    b = s.shape[0]
    return s.reshape(b, N_PAIR, DHA, 2, DHA).transpose(0, 1, 3, 2, 4).reshape(b, HA, DHA, DHA)


def _rwkv_scan(seqs, s0, rw, ones_da, tc, n_tok):
    bsz, t, _ = seqs[0].shape
    assert tc == -(-n_tok // SUBLANES) * SUBLANES
    assert n_tok % 2 == 0
    r_k, lnx_g, lnx_b = rw[7:10]
    row = lambda v: v.reshape(1, DA)
    seq_spec = pl.BlockSpec((None, tc, DA), lambda b, c: (b, c, 0))
    st_spec = pl.BlockSpec((None, N_PAIR, DHA, LANES), lambda b, c: (b, 0, 0, 0))
    full = lambda shape: pl.BlockSpec(shape, lambda b, c: (0, 0))
    y, s_fin = pl.pallas_call(
        functools.partial(_scan_kernel, n_tok),
        grid=(bsz, t // tc),
        in_specs=[seq_spec] * 6 + [st_spec, full((1, DA)), full((1, DA)), full((1, DA)), full((DA, DA)),
                                   full((2 * LANES, LANES))],
        out_specs=[seq_spec, st_spec],
        out_shape=[jax.ShapeDtypeStruct((bsz, t, DA), F32), jax.ShapeDtypeStruct((bsz, N_PAIR, DHA, LANES), F32)],
        scratch_shapes=[pltpu.VMEM((N_PAIR, DHA, LANES), F32), pltpu.VMEM((tc, DA), F32)],
        compiler_params=_cparams("parallel", "arbitrary"),
    )(*seqs, _pack_state(s0), row(r_k), row(lnx_g), row(lnx_b), ones_da,
      jnp.concatenate([ones_da[:LANES, :LANES]] * 2, axis=0))
    return y, _unpack_state(s_fin)


def _rope_kernel(qb_ref, kb_ref, vb_ref, qi_ref, ki_ref, c128_ref, s128_ref, c64_ref, s64_ref,
                 q_o, k_o, v_o, kbf_o, vbf_o, qi_o, ki_o, kcat_o):
    c128, s128, c64, s64 = c128_ref[...], s128_ref[...], c64_ref[...], s64_ref[...]
    lane = lax.broadcasted_iota(I32, c64.shape, 1)
    first_half = (lane & (DI - 1)) < DI // 2

    def rope128(x):
        return x * c128 + pltpu.roll(x, DHB // 2, 1) * s128

    def rope64(x):
        partner = jnp.where(first_half, pltpu.roll(x, LANES - DI // 2, 1), pltpu.roll(x, DI // 2, 1))
        return x * c64 + partner * s64

    for h in range(HB):
        sl = slice(h * DHB, (h + 1) * DHB)
        q_o[:, sl] = rope128(qb_ref[:, sl]).astype(BF16)
        kr = rope128(kb_ref[:, sl])
        k_o[h] = kr
        v_o[h] = vb_ref[:, sl]
        kbf_o[:, sl] = kr.astype(BF16)
    vbf_o[...] = vb_ref[...].astype(BF16)
    for h2 in range(HI * DI // LANES):
        sl = slice(h2 * LANES, (h2 + 1) * LANES)
        qi_o[:, sl] = rope64(qi_ref[:, sl])
    ki = rope64(ki_ref[...])[:, :DI]
    ki_o[...] = ki
    kcat_o[...] = _index_key_cat(ki)


def _rope_tables(pos, d):
    inv = ROPE_THETA ** (-jnp.arange(0, d, 2, dtype=F32) / d)
    ang = pos.astype(F32)[:, None] * inv[None, :]
    cos, sin = jnp.cos(ang), jnp.sin(ang)
    reps = LANES // d
    c = jnp.tile(jnp.concatenate([cos, cos], axis=1), (1, reps))
    s = jnp.tile(jnp.concatenate([-sin, sin], axis=1), (1, reps))
    return c, s


def _rope(p, pos, tm):
    t = p.shape[0]
    c128, s128 = _rope_tables(pos, DHB)
    c64, s64 = _rope_tables(pos, DI)
    col = lambda w, c0: pl.BlockSpec((tm, w), lambda i: (i, c0 // w))
    rows = lambda w: pl.BlockSpec((tm, w), lambda i: (i, 0))
    sds = lambda w, dt: jax.ShapeDtypeStruct((t, w), dt)
    heads = pl.BlockSpec((HB, tm, DHB), lambda i: (0, i, 0))
    hsds = jax.ShapeDtypeStruct((HB, t, DHB), F32)
    return pl.pallas_call(
        _rope_kernel,
        grid=(t // tm,),
        in_specs=[col(DB, C_QB), col(DB, C_KB), col(DB, C_VB), col(HI * DI, C_QI), col(LANES, C_KI),
                  rows(LANES), rows(LANES), rows(LANES), rows(LANES)],
        out_specs=[rows(DB), heads, heads, rows(DB), rows(DB), rows(HI * DI), rows(DI), rows(4 * DI)],
        out_shape=[sds(DB, BF16), hsds, hsds, sds(DB, BF16), sds(DB, BF16), sds(HI * DI, F32), sds(DI, F32),
                   sds(4 * DI, BF16)],
        compiler_params=_cparams("parallel"),
    )(p, p, p, p, p, c128, s128, c64, s64)


def _canonical(score):
    return jnp.where(score == 0.0, 0.0, score)


def _ordinal_to_float(u):
    s = u ^ INT_MIN
    b = s ^ ((s >> 31) & 0x7FFFFFFF)
    return lax.bitcast_convert_type(b, F32)


def _index_query_cat(qi):
    qi_hi, qi_lo = _split2(qi)
    out = []
    for h in range(HI):
        sl = slice(h * DI, (h + 1) * DI)
        out.append(jnp.concatenate([qi_hi[:, sl], qi_hi[:, sl], qi_lo[:, sl], qi_lo[:, sl]], axis=1))
    return out


def _index_key_cat(ki):
    k_hi, k_lo = _split2(ki)
    return jnp.concatenate([k_hi, k_lo, k_hi, k_lo], axis=1)


def _index_scores(q_cat, wi, k_cat):
    acc = None
    for h in range(HI):
        term = wi[:, h:h + 1] * jnp.maximum(_dot_nt(q_cat[h], k_cat), 0.0)
        acc = term if acc is None else acc + term
    return acc * (HI ** -0.5 * DI ** -0.5)


def _count(skey_ref, nc, ck, pred):
    rows = skey_ref.shape[0]

    def body(c, acc):
        base = pl.multiple_of(c * ck, ck)
        for u in range(ck // LANES):
            blk = skey_ref[:, pl.ds(base + u * LANES, LANES)]
            acc = acc + jnp.where(pred(blk, base + u * LANES), 1, 0)
        return acc

    acc = lax.fori_loop(0, nc, body, jnp.zeros((rows, LANES), I32))
    return jnp.sum(acc, axis=1, keepdims=True)


def _select_bias(skey_ref, bias_ref, cut_ref, nc, nc_total, ck, topk, idx_bits):
    rows = skey_ref.shape[0]
    lane = lax.broadcasted_iota(I32, (rows, LANES), 1)

    def bit_body(it, carry):
        ans_u, done = carry

        def refine(_):
            cand_u = ans_u | lax.shift_left(jnp.int32(1), 31 - it)
            cand = _ordinal_to_float(cand_u)
            cnt = _count(skey_ref, nc, ck, lambda blk, _: blk >= cand)
            take = (cnt >= topk) & (done == 0)
            return jnp.where(take, cand_u, ans_u), jnp.where(take & (cnt == topk), 1, done)

        return lax.cond(jnp.min(done) == 0, refine, lambda _: (ans_u, done), None)

    zeros = jnp.zeros((rows, 1), I32)
    ans_u = lax.fori_loop(0, 32, bit_body, (zeros, zeros))[0]
    ans_u = jnp.where((ans_u >= 0) & (ans_u < ORD_NEG_INF), ORD_NEG_INF, ans_u)
    thr = _ordinal_to_float(ans_u)
    n_gt = _count(skey_ref, nc, ck, lambda blk, _: blk > thr)
    n_ge = _count(skey_ref, nc, ck, lambda blk, _: blk >= thr)
    need = topk - n_gt
    tied = ((n_ge - n_gt) > need) & (thr > -jnp.inf)
    cut_ref[...] = jnp.full((rows, 1), 2 ** 30, I32)

    @pl.when(jnp.max(tied.astype(I32)) > 0)
    def _():
        def tie_body(it, ans):
            cand = ans | lax.shift_left(jnp.int32(1), idx_bits - 1 - it)
            g = _count(skey_ref, nc, ck, lambda blk, base: (blk == thr) & ((lane + base) < cand))
            return jnp.where(g < need, cand, ans)

        cut = lax.fori_loop(0, idx_bits, tie_body, jnp.zeros((rows, 1), I32))
        cut_ref[...] = jnp.where(tied, cut, 2 ** 30)

    cut = cut_ref[...]

    def write(c, carry):
        base = pl.multiple_of(c * ck, ck)
        for u in range(ck // LANES):
            off = base + u * LANES
            blk = skey_ref[:, pl.ds(off, LANES)]
            sel = ((blk > thr) | ((blk == thr) & ((lane + off) <= cut))) & (blk > -jnp.inf)
            bias_ref[:, pl.ds(off, LANES)] = jnp.where(sel, 0.0, NEG_BIAS).astype(bias_ref.dtype)
        return carry

    lax.fori_loop(0, nc, write, 0)

    def fill(c, carry):
        base = pl.multiple_of(c * ck, ck)
        bias_ref[:, pl.ds(base, ck)] = jnp.full((rows, ck), NEG_BIAS, bias_ref.dtype)
        return carry

    lax.fori_loop(nc, nc_total, fill, 0)


def _prompt_select_kernel(topk, ck, idx_bits, qi_ref, wi_ref, kcat_ref, bias_o, skey_ref, cut_ref):
    i = pl.program_id(0)
    tq = qi_ref.shape[0]
    t = kcat_ref.shape[0]
    nc = ((i + 1) * tq + ck - 1) // ck
    q_cat = _index_query_cat(qi_ref[...])
    wi = wi_ref[...]
    qpos = i * tq + lax.broadcasted_iota(I32, (tq, ck), 0)
    kiota = lax.broadcasted_iota(I32, (tq, ck), 1)

    def score_chunk(c, carry):
        base = pl.multiple_of(c * ck, ck)
        sc = _index_scores(q_cat, wi, kcat_ref[pl.ds(base, ck), :])
        sc = jnp.where(kiota + base <= qpos, sc, -jnp.inf)
        skey_ref[:, pl.ds(base, ck)] = _canonical(sc)
        return carry

    lax.fori_loop(0, nc, score_chunk, 0)
    _select_bias(skey_ref, bias_o, cut_ref, nc, t // ck, ck, topk, idx_bits)


def _prompt_select(qi, p, k_cat, tq, ck):
    t = qi.shape[0]
    topk = min(TOPK_MAX, t // 4)
    idx_bits = int(np.ceil(np.log2(t))) + 1
    return pl.pallas_call(
        functools.partial(_prompt_select_kernel, topk, ck, idx_bits),
        grid=(t // tq,),
        in_specs=[pl.BlockSpec((tq, HI * DI), lambda i: (i, 0)),
                  pl.BlockSpec((tq, LANES), lambda i: (i, C_WI // LANES)),
                  pl.BlockSpec((t, 4 * DI), lambda i: (0, 0))],
        out_specs=pl.BlockSpec((tq, t), lambda i: (i, 0)),
        out_shape=jax.ShapeDtypeStruct((t, t), BF16),
        scratch_shapes=[pltpu.VMEM((tq, t), F32), pltpu.VMEM((tq, 1), I32)],
        compiler_params=_cparams("parallel"),
    )(qi, p, k_cat)


def _sample_score_kernel(pps, rq, n_new, pt_ref, qit_ref, wif_ref, *refs):
    page_refs, knew_ref, skey_o, skey_new_o = refs[:pps], refs[pps], refs[pps + 1], refs[pps + 2]
    j = pl.program_id(1)
    q_hi, q_lo = _split2(qit_ref[...])
    wif = wif_ref[...]

    def scores(kt):
        k_hi, k_lo = _split2(kt)
        at = _dot(q_hi, k_hi) + _dot(q_hi, k_lo) + _dot(q_lo, k_hi)
        at = jnp.maximum(at, 0.0) * wif
        sc = at[0:rq]
        for h in range(1, HI):
            sc = sc + at[h * rq:(h + 1) * rq]
        return sc * (HI ** -0.5 * DI ** -0.5)

    sc = scores(jnp.concatenate([r[...] for r in page_refs], axis=1))
    trow = lax.broadcasted_iota(I32, sc.shape, 0)
    skey_o[...] = _canonical(jnp.where(trow < n_new, sc, -jnp.inf))

    @pl.when(j == pl.num_programs(1) - 1)
    def _():
        scn = scores(knew_ref[...])
        tr = lax.broadcasted_iota(I32, scn.shape, 0)
        ln = lax.broadcasted_iota(I32, scn.shape, 1)
        skey_new_o[...] = _canonical(jnp.where((tr < n_new) & (ln <= tr), scn, -jnp.inf))


def _sample_scores(qit, wif, cache_kidx, knew_pad, page_table, n_new, rq, pps):
    bsz, n_pages = page_table.shape
    per_b = lambda b, j, pt: (b, 0, 0)
    page_spec = lambda u: pl.BlockSpec((None, DI, PAGE), lambda b, j, pt: (pt[b, j * pps + u], 0, 0))
    grid_spec = pltpu.PrefetchScalarGridSpec(
        num_scalar_prefetch=1,
        grid=(bsz, n_pages // pps),
        in_specs=[pl.BlockSpec((None, LANES, DI), per_b), pl.BlockSpec((None, LANES, 1), per_b)]
                 + [page_spec(u) for u in range(pps)] + [pl.BlockSpec((None, DI, PAGE), per_b)],
        out_specs=[pl.BlockSpec((None, rq, pps * PAGE), lambda b, j, pt: (b, 0, j)),
                   pl.BlockSpec((None, rq, PAGE), per_b)],
    )
    return pl.pallas_call(
        functools.partial(_sample_score_kernel, pps, rq, n_new),
        grid_spec=grid_spec,
        out_shape=[jax.ShapeDtypeStruct((bsz, rq, n_pages * PAGE), F32), jax.ShapeDtypeStruct((bsz, rq, PAGE), F32)],
        compiler_params=_cparams("parallel", "arbitrary"),
    )(page_table, qit, wif, *([cache_kidx] * pps), knew_pad)


def _sample_select_kernel(topk, ck, idx_bits, skey_past, skey_new, bias_o, skey_ref, cut_ref):
    past = skey_past.shape[1]
    skey_ref[:, :past] = skey_past[...]
    skey_ref[:, past:] = skey_new[...]
    nc = skey_ref.shape[1] // ck
    _select_bias(skey_ref, bias_o, cut_ref, nc, nc, ck, topk, idx_bits)


def _sample_select(skey_past, skey_new, topk, tr):
    rows, past = skey_past.shape
    n = past + skey_new.shape[1]
    idx_bits = int(np.ceil(np.log2(n))) + 1
    return pl.pallas_call(
        functools.partial(_sample_select_kernel, topk, PAGE, idx_bits),
        grid=(rows // tr,),
        in_specs=[pl.BlockSpec((tr, past), lambda i: (i, 0)), pl.BlockSpec((tr, n - past), lambda i: (i, 0))],
        out_specs=pl.BlockSpec((tr, n), lambda i: (i, 0)),
        out_shape=jax.ShapeDtypeStruct((rows, n), F32),
        scratch_shapes=[pltpu.VMEM((tr, n), F32), pltpu.VMEM((tr, 1), I32)],
        compiler_params=_cparams("parallel"),
    )(skey_past, skey_new)


def _prompt_attn_kernel(ck, q_ref, k_ref, v_ref, bias_ref, o_ref, m_ref, l_ref, acc_ref):
    i = pl.program_id(0)
    tq = q_ref.shape[0]
    nc = ((i + 1) * tq + ck - 1) // ck
    m_ref[...] = jnp.full(m_ref.shape, NEG_BIAS, F32)
    l_ref[...] = jnp.zeros(l_ref.shape, F32)
    acc_ref[...] = jnp.zeros(acc_ref.shape, F32)
    scale = DHB ** -0.5

    def chunk(c, carry):
        base = pl.multiple_of(c * ck, ck)
        bias = bias_ref[:, pl.ds(base, ck)].astype(F32)
        for h in range(HB):
            sl = slice(h * DHB, (h + 1) * DHB)
            s = _dot_nt(q_ref[:, sl], k_ref[pl.ds(base, ck), sl]) * scale + bias
            m_old = m_ref[h]
            m_new = jnp.maximum(m_old, jnp.max(s, axis=1, keepdims=True))
            alpha = jnp.exp(m_old - m_new)
            p = jnp.exp(s - m_new)
            l_ref[h] = alpha * l_ref[h] + jnp.sum(p, axis=1, keepdims=True)
            acc_ref[h] = alpha * acc_ref[h] + _dot(p.astype(BF16), v_ref[pl.ds(base, ck), sl])
            m_ref[h] = m_new
        return carry

    lax.fori_loop(0, nc, chunk, 0)
    for h in range(HB):
        o_ref[:, h * DHB:(h + 1) * DHB] = acc_ref[h] / l_ref[h]


def _prompt_attn(q_bf, k_bf, v_bf, bias, tq, ck):
    t = q_bf.shape[0]
    resident = lambda: pl.BlockSpec((t, DB), lambda i: (0, 0), pipeline_mode=pl.Buffered(1))
    return pl.pallas_call(
        functools.partial(_prompt_attn_kernel, ck),
        grid=(t // tq,),
        in_specs=[pl.BlockSpec((tq, DB), lambda i: (i, 0)), resident(), resident(),
                  pl.BlockSpec((tq, t), lambda i: (i, 0))],
        out_specs=pl.BlockSpec((tq, DB), lambda i: (i, 0)),
        out_shape=jax.ShapeDtypeStruct((t, DB), F32),
        scratch_shapes=[pltpu.VMEM((HB, tq, 1), F32), pltpu.VMEM((HB, tq, 1), F32), pltpu.VMEM((HB, tq, DHB), F32)],
        compiler_params=_cparams("parallel"),
    )(q_bf, k_bf, v_bf, bias)


def _sample_attn_kernel(pps, rq, pt_ref, qbd_ref, *refs):
    kpages, vpages = refs[:pps], refs[pps:2 * pps]
    knew_ref, vnew_ref, bias_ref, bias_new_ref, o_ref, m_ref, l_ref, acc_ref = refs[2 * pps:]
    j = pl.program_id(1)
    nr = HB * rq

    @pl.when(j == 0)
    def _():
        m_ref[...] = jnp.full(m_ref.shape, NEG_BIAS, F32)
        l_ref[...] = jnp.zeros(l_ref.shape, F32)
        acc_ref[...] = jnp.zeros(acc_ref.shape, F32)

    def accumulate(kp, vp, bias):
        st = _dot(kp, qbd_ref[...])
        s = st.T[:nr] * (DHB ** -0.5) + jnp.concatenate([bias] * HB, axis=0)
        m_old = m_ref[...]
        m_new = jnp.maximum(m_old, jnp.max(s, axis=1, keepdims=True))
        alpha = jnp.exp(m_old - m_new)
        p = jnp.exp(s - m_new)
        l_ref[...] = alpha * l_ref[...] + jnp.sum(p, axis=1, keepdims=True)
        acc_ref[...] = alpha * acc_ref[...] + _dot(p.astype(BF16), vp)
        m_ref[...] = m_new

    gather = lambda pages: jnp.concatenate(
        [jnp.concatenate([r[h].astype(BF16) for h in range(HB)], axis=1) for r in pages], axis=0)
    accumulate(gather(kpages), gather(vpages), bias_ref[...])

    @pl.when(j == pl.num_programs(1) - 1)
    def _():
        accumulate(knew_ref[...].astype(BF16), vnew_ref[...].astype(BF16), bias_new_ref[...])
        for h in range(HB):
            rs = slice(h * rq, (h + 1) * rq)
            cs = slice(h * DHB, (h + 1) * DHB)
            o_ref[:, cs] = acc_ref[rs, cs] / l_ref[rs, :]


def _sample_attn(qbd, cache_k, cache_v, knew_pad, vnew_pad, bias, page_table, rq, pps):
    bsz, n_pages = page_table.shape
    page_spec = lambda u: pl.BlockSpec((None, HB, PAGE, DHB), lambda b, j, pt: (pt[b, j * pps + u], 0, 0, 0))
    per_b = lambda b, j, pt: (b, 0, 0)
    grid_spec = pltpu.PrefetchScalarGridSpec(
        num_scalar_prefetch=1,
        grid=(bsz, n_pages // pps),
        in_specs=[pl.BlockSpec((None, DB, LANES), per_b)]
                 + [page_spec(u) for u in range(pps)] * 2
                 + [pl.BlockSpec((None, PAGE, DB), per_b), pl.BlockSpec((None, PAGE, DB), per_b),
                    pl.BlockSpec((None, rq, pps * PAGE), lambda b, j, pt: (b, 0, j)),
                    pl.BlockSpec((None, rq, PAGE), lambda b, j, pt: (b, 0, n_pages))],
        out_specs=pl.BlockSpec((None, rq, DB), per_b),
        scratch_shapes=[pltpu.VMEM((HB * rq, 1), F32), pltpu.VMEM((HB * rq, 1), F32), pltpu.VMEM((HB * rq, DB), F32)],
    )
    return pl.pallas_call(
        functools.partial(_sample_attn_kernel, pps, rq),
        grid_spec=grid_spec,
        out_shape=jax.ShapeDtypeStruct((bsz, rq, DB), F32),
        compiler_params=_cparams("parallel", "arbitrary"),
    )(page_table, qbd, *([cache_k] * pps), *([cache_v] * pps), knew_pad, vnew_pad, bias, bias)


def _cross_kernel(q_ref, mk_ref, mv_ref, o_ref):
    scale = DHC ** -0.5
    for h in range(HC):
        sl = slice(h * DHC, (h + 1) * DHC)
        s = _dot_nt(q_ref[:, sl].astype(BF16), mk_ref[h].astype(BF16)) * scale
        m = jnp.max(s, axis=1, keepdims=True)
        p = jnp.exp(s - m)
        l = jnp.sum(p, axis=1, keepdims=True)
        o_ref[:, sl] = _dot(p.astype(BF16), mv_ref[h].astype(BF16)) / l


def _cross_attn(q_arr, q_col, mk, mv, tm):
    bsz, t, _ = q_arr.shape
    return pl.pallas_call(
        _cross_kernel,
        grid=(bsz, t // tm),
        in_specs=[pl.BlockSpec((None, tm, DC), lambda b, i: (b, i, q_col // DC)),
                  pl.BlockSpec((None, HC, N_MEM, DHC), lambda b, i: (b, 0, 0, 0)),
                  pl.BlockSpec((None, HC, N_MEM, DHC), lambda b, i: (b, 0, 0, 0))],
        out_specs=pl.BlockSpec((None, tm, DC), lambda b, i: (b, i, 0)),
        out_shape=jax.ShapeDtypeStruct((bsz, t, DC), F32),
        compiler_params=_cparams("parallel", "parallel"),
    )(q_arr, mk, mv)


def _merge_kernel(x_ref, ya_ref, za_ref, yb_ref, zb_ref, yc_ref, zc_ref, ga_ref, gb_ref, gc_ref,
                  bga_ref, bgb_ref, bgc_ref, wa_ref, wb_ref, wc_ref, wo_ref, fg_ref, o_ref):
    ba = _dot((ya_ref[...] * _silu(za_ref[...])).astype(BF16), wa_ref[...])
    bb = _dot((yb_ref[...] * _silu(zb_ref[...])).astype(BF16), wb_ref[...])
    bc = _dot((yc_ref[...] * _silu(zc_ref[...])).astype(BF16), wc_ref[...])
    mix = (_sigmoid(ga_ref[...] + bga_ref[...]) * ba + _sigmoid(gb_ref[...] + bgb_ref[...]) * bb
           + _sigmoid(gc_ref[...] + bgc_ref[...]) * bc)
    y = x_ref[...] + _dot(mix.astype(BF16), wo_ref[...])
    ms = jnp.mean(y * y, axis=-1, keepdims=True)
    o_ref[...] = y * lax.rsqrt(ms + EPS) * fg_ref[...]


def _merge(x2d, p, ya, yb, yc, b_gate, wa, wb, wc, wo, fg, tm):
    t, d = x2d.shape
    rows = lambda w: pl.BlockSpec((tm, w), lambda i: (i, 0))
    col = lambda w, c0: pl.BlockSpec((tm, w), lambda i: (i, c0 // w))
    const = lambda shape, c=0: pl.BlockSpec(shape, lambda i: (0, c), pipeline_mode=pl.Buffered(1))
    bg = b_gate.reshape(1, 3 * d)
    return pl.pallas_call(
        _merge_kernel,
        grid=(t // tm,),
        in_specs=[rows(d), rows(DA), col(DA, C_ZA), rows(DB), col(DB, C_ZB), rows(DC), col(DC, C_ZC),
                  col(d, C_GATES), col(d, C_GATES + d), col(d, C_GATES + 2 * d),
                  const((1, d), 0), const((1, d), 1), const((1, d), 2),
                  const((DA, d)), const((DB, d)), const((DC, d)), const((d, d)), const((1, d))],
        out_specs=rows(d),
        out_shape=jax.ShapeDtypeStruct((t, d), F32),
        compiler_params=_cparams("parallel"),
    )(x2d, ya, p, yb, p, yc, p, p, p, p, bg, bg, bg, wa, wb, wc, wo, fg.reshape(1, d))


def _block_ones(n, seg):
    idx = np.arange(n) // seg
    return jnp.asarray(idx[:, None] == idx[None, :], dtype=BF16)


def _shift_row(p_row):
    return jnp.concatenate([p_row[:3 * DA], p_row[C_WDAD:C_WDAD + 2 * LORA]])


def kernel(x_prompt, x_sample, mem_prompt, cache_k, cache_v, cache_kidx, cache_mem_k, cache_mem_v, state_wkv, state_shift, page_table, norm_g, w_in, rwkv_mu, rwkv_w0, rwkv_w_up, rwkv_a0, rwkv_a_up, rwkv_k_k, rwkv_k_a, rwkv_r_k, rwkv_lnx_g, rwkv_lnx_b, mem_norm_g, w_mem_kv, b_gate, w_br_a, w_br_b, w_br_c, w_out, final_norm_g):
    depth = w_in.shape[0]
    assert depth == 1 and x_prompt.shape[0] == 1
    t = x_prompt.shape[1]
    bs, ts, d = x_sample.shape
    n_pages = page_table.shape[1]
    past = n_pages * PAGE
    rq = 8
    assert ts <= rq and (ts & (ts - 1)) == 0

    rw = (rwkv_mu[0], rwkv_w0[0], rwkv_w_up[0], rwkv_a0[0], rwkv_a_up[0], rwkv_k_k[0], rwkv_k_a[0],
          rwkv_r_k[0].reshape(DA), rwkv_lnx_g[0], rwkv_lnx_b[0])
    ones_da = _block_ones(DA, DHA)
    w_p = _permute_w_in(w_in[0])
    wa, wb, wc, wo = (w[0].astype(BF16) for w in (w_br_a, w_br_b, w_br_c, w_out))

    xp2 = x_prompt.reshape(t, d)
    xs2 = x_sample.reshape(bs * ts, d)
    tm_p = min(512, t)
    p_p = _proj(xp2, norm_g[0], w_p, tm_p, 1024)
    p_s = _proj(xs2, norm_g[0], w_p, bs * ts, 1024)
    mkv = _proj(mem_prompt.reshape(N_MEM, d), mem_norm_g[0], w_mem_kv[0].astype(BF16), N_MEM, 2 * DC)
    mk_p = mkv[:, :DC].reshape(1, N_MEM, HC, DHC)
    mv_p = mkv[:, DC:].reshape(1, N_MEM, HC, DHC)
    head_major = lambda a: jnp.transpose(a, (0, 2, 1, 3))

    seq_p = _rwkv_prep(p_p, jnp.zeros((1, SHIFT_W), F32), rw, ones_da, min(256, t), True)
    ya_p, wkv_p = _rwkv_scan([a[None] for a in seq_p], jnp.zeros((1, HA, DHA, DHA), F32), rw, ones_da,
                             min(256, t), min(256, t))
    pinit_s = jnp.repeat(state_shift[0], ts, axis=0)
    seq_s = _rwkv_prep(p_s, pinit_s, rw, ones_da, bs * ts, False, ts)
    pad_t = lambda a: jnp.pad(a.reshape(bs, ts, -1), ((0, 0), (0, rq - ts), (0, 0)))
    ya_s, wkv_s = _rwkv_scan([pad_t(a) for a in seq_s], state_wkv[0], rw, ones_da, rq, ts)
    ya_s = ya_s[:, :ts].reshape(bs * ts, DA)

    tokens_major = lambda a: jnp.transpose(a, (1, 0, 2))
    q_p, k_p, v_p, kbf_p, vbf_p, qi_p, ki_p, kcat_p = _rope(p_p, jnp.arange(t), min(512, t))
    pos_s = jnp.tile(past + jnp.arange(ts), bs)
    q_s, k_s, v_s, _, _, qi_s, ki_s, _ = _rope(p_s, pos_s, bs * ts)
    k_s, v_s = tokens_major(k_s), tokens_major(v_s)

    bias_p = _prompt_select(qi_p, p_p, kcat_p, min(128, t), 512)
    yb_p = _prompt_attn(q_p, kbf_p, vbf_p, bias_p, min(256, t), 1024)

    topk_s = min(TOPK_MAX, (past + ts) // 4)
    pad_rows = lambda a, n: jnp.pad(a, ((0, 0), (0, n - a.shape[1]), (0, 0)))
    pps = 8 if n_pages % 8 == 0 else 1
    qit = pad_rows(qi_s.reshape(bs, ts, HI * DI), rq).reshape(bs, rq, HI, DI).transpose(0, 2, 1, 3)
    qit = pad_rows(qit.reshape(bs, HI * rq, DI), LANES)
    wif = pad_rows(p_s[:, C_WI:C_WI + HI].reshape(bs, ts, HI), rq).transpose(0, 2, 1).reshape(bs, HI * rq, 1)
    wif = pad_rows(wif, LANES)
    kinew = jnp.swapaxes(pad_rows(ki_s.reshape(bs, ts, DI), PAGE), 1, 2)
    skey_past, skey_new = _sample_scores(qit, wif, jnp.swapaxes(cache_kidx[0], 1, 2), kinew, page_table, ts,
                                         rq, pps)
    bias_s = _sample_select(skey_past.reshape(bs * rq, past), skey_new.reshape(bs * rq, PAGE), topk_s,
                            min(128, bs * rq))
    bias_s = bias_s.reshape(bs, rq, past + PAGE)
    knew = pad_rows(k_s.reshape(bs, ts, DB), PAGE)
    vnew = pad_rows(v_s.reshape(bs, ts, DB), PAGE)
    qh = pad_rows(q_s.reshape(bs, ts, DB), rq).reshape(bs, rq, HB, DHB).transpose(0, 2, 3, 1)
    head_eq = jnp.eye(HB, dtype=bool)[None, :, None, :, None]
    qbd = jnp.where(head_eq, qh[:, :, :, None, :], jnp.zeros((), BF16)).reshape(bs, DB, HB * rq)
    qbd = jnp.pad(qbd, ((0, 0), (0, 0), (0, LANES - HB * rq)))
    yb_s = _sample_attn(qbd, head_major(cache_k[0]), head_major(cache_v[0]), knew, vnew, bias_s, page_table,
                        rq, pps)
    yb_s = yb_s[:, :ts].reshape(bs * ts, DB)

    yc_p = _cross_attn(p_p[None], C_QC, head_major(mk_p), head_major(mv_p), min(512, t))[0]
    qc8 = pad_rows(p_s[:, C_QC:C_QC + DC].reshape(bs, ts, DC), rq)
    yc_s = _cross_attn(qc8, 0, head_major(cache_mem_k[0]), head_major(cache_mem_v[0]), rq)
    yc_s = yc_s[:, :ts].reshape(bs * ts, DC)

    y_p = _merge(xp2, p_p, ya_p[0], yb_p, yc_p, b_gate[0], wa, wb, wc, wo, final_norm_g, min(128, t))
    y_s = _merge(xs2, p_s, ya_s, yb_s, yc_s, b_gate[0], wa, wb, wc, wo, final_norm_g, bs * ts)

    p_s3 = p_s.reshape(bs, ts, N_P)
    shift_s = jnp.concatenate([p_s3[:, -1, :3 * DA], p_s3[:, -1, C_WDAD:C_WDAD + 2 * LORA]], axis=-1)
    return (y_p.reshape(1, t, d), y_s.reshape(bs, ts, d),
            tokens_major(k_p).reshape(1, 1, t, HB, DHB), tokens_major(v_p).reshape(1, 1, t, HB, DHB),
            ki_p.reshape(1, 1, t, DI),
            mk_p.reshape(1, 1, N_MEM, HC, DHC), mv_p.reshape(1, 1, N_MEM, HC, DHC),
            wkv_p.reshape(1, 1, HA, DHA, DHA), _shift_row(p_p[t - 1]).reshape(1, 1, SHIFT_W),
            k_s.reshape(1, bs, ts, HB, DHB), v_s.reshape(1, bs, ts, HB, DHB), ki_s.reshape(1, bs, ts, DI),
            wkv_s.reshape(1, bs, HA, DHA, DHA), shift_s.reshape(1, bs, SHIFT_W))
```

```python
import functools

import numpy as np
import jax
import jax.numpy as jnp
from jax import lax
from jax.experimental import pallas as pl
from jax.experimental.pallas import tpu as pltpu

F32, BF16, I32 = jnp.float32, jnp.bfloat16, jnp.int32

D_MODEL = 2048
HA, DHA = 12, 64
DA = HA * DHA
LORA = 64
LNX_EPS = 64e-5
HB, DHB = 6, 128
DB = HB * DHB
HI, DI = 8, 64
TOPK_MAX = 256
N_MEM = 256
HC, DHC = 4, 128
DC = HC * DHC
ROPE_THETA = 10000.0
EPS = 1e-6
PAGE = 128
SHIFT_W = 3 * DA + 2 * LORA
SPLIT_SIZES = (SHIFT_W, DA, DB, DB, DB, HI * DI, HI, DI, DB, DC, DC, 3 * D_MODEL)

LANES = 128
SUBLANES = 8
N_PAIR = HA // 2

C_RKV, C_ZA, C_QB, C_KB, C_VB, C_ZB = 0, 2304, 3072, 3840, 4608, 5376
C_QI, C_QC, C_ZC, C_WDAD, C_KI, C_WI, C_GATES = 6144, 6656, 7168, 7680, 7808, 7936, 8192
N_P = C_GATES + 3 * D_MODEL

NEG_BIAS = -1e30
ORD_NEG_INF = 0x007FFFFF
INT_MIN = -2147483648
VMEM_LIMIT = 56 * 1024 * 1024


def _cparams(*sem):
    return pltpu.CompilerParams(dimension_semantics=sem, vmem_limit_bytes=VMEM_LIMIT)


def _dot(a, b):
    return jnp.dot(a, b, preferred_element_type=F32)


def _dot_nt(a, b):
    return lax.dot_general(a, b, (((1,), (1,)), ((), ())), preferred_element_type=F32)


def _split2(x):
    hi = x.astype(BF16)
    lo = (x - hi.astype(F32)).astype(BF16)
    return hi, lo


def _dot_ones(x, ones_bf):
    hi = x.astype(BF16)
    r1 = x - hi.astype(F32)
    mid = r1.astype(BF16)
    lo = (r1 - mid.astype(F32)).astype(BF16)
    return _dot(hi, ones_bf) + _dot(mid, ones_bf) + _dot(lo, ones_bf)


def _dot3(a, b):
    a_hi, a_lo = _split2(a)
    b_hi, b_lo = _split2(b)
    return _dot(a_hi, b_hi) + _dot(a_hi, b_lo) + _dot(a_lo, b_hi)


def _sigmoid(x):
    return 1.0 / (1.0 + jnp.exp(-x))


def _silu(x):
    return x * _sigmoid(x)


def _proj_kernel(x_ref, g_ref, w_ref, o_ref, xn_ref):
    @pl.when(pl.program_id(1) == 0)
    def _():
        x = x_ref[...]
        ms = jnp.mean(x * x, axis=-1, keepdims=True)
        xn_ref[...] = (x * lax.rsqrt(ms + EPS) * g_ref[...]).astype(BF16)

    o_ref[...] = _dot(xn_ref[...], w_ref[...])


def _proj(x2d, g, w_bf, tm, tn):
    m, d = x2d.shape
    n = w_bf.shape[1]
    return pl.pallas_call(
        _proj_kernel,
        grid=(m // tm, n // tn),
        in_specs=[pl.BlockSpec((tm, d), lambda i, j: (i, 0)),
                  pl.BlockSpec((1, d), lambda i, j: (0, 0)),
                  pl.BlockSpec((d, tn), lambda i, j: (0, j))],
        out_specs=pl.BlockSpec((tm, tn), lambda i, j: (i, j)),
        out_shape=jax.ShapeDtypeStruct((m, n), F32),
        scratch_shapes=[pltpu.VMEM((tm, d), BF16)],
        compiler_params=_cparams("parallel", "arbitrary"),
    )(x2d, g.reshape(1, d), w_bf)


def _permute_w_in(w):
    o = [int(v) for v in np.cumsum((0,) + SPLIT_SIZES)]
    za0, qb0, kb0, vb0, qi0, wi0, ki0, zb0, qc0, zc0, g0 = o[1:12]
    z = lambda n: jnp.zeros((w.shape[0], n), w.dtype)
    parts = [w[:, 0:3 * DA], w[:, za0:za0 + DA], w[:, qb0:qb0 + DB], w[:, kb0:kb0 + DB], w[:, vb0:vb0 + DB],
             w[:, zb0:zb0 + DB], w[:, qi0:qi0 + HI * DI], w[:, qc0:qc0 + DC], w[:, zc0:zc0 + DC],
             w[:, 3 * DA:SHIFT_W], w[:, ki0:ki0 + DI], z(LANES - DI), w[:, wi0:wi0 + HI], z(LANES - HI), z(LANES),
             w[:, g0:g0 + 3 * D_MODEL]]
    out = jnp.concatenate(parts, axis=1).astype(BF16)
    assert out.shape[1] == N_P
    return out


def _prep_kernel(carry_mode, period, rkv_ref, wdad_ref, pinit_ref, mu_ref, w0_ref, wup_ref, a0_ref, aup_ref,
                 kk_ref, ka_ref, ones_ref, r_o, w_o, k_o, v_o, kn_o, b_o, carry_ref):
    tm = rkv_ref.shape[0]
    row = lax.broadcasted_iota(I32, (tm, 1), 0)
    cur = jnp.concatenate([rkv_ref[...], wdad_ref[...]], axis=1)
    rolled = pltpu.roll(cur, 1, 0)
    if carry_mode:
        @pl.when(pl.program_id(0) == 0)
        def _():
            carry_ref[...] = pinit_ref[...]
        prev = jnp.where(row == 0, carry_ref[...], rolled)
        carry_ref[...] = cur[tm - 1:tm, :]
    else:
        prev = jnp.where((row & (period - 1)) == 0, pinit_ref[...], rolled)
    m = cur + mu_ref[...] * (prev - cur)
    r = m[:, 0:DA]
    k = m[:, DA:2 * DA]
    v = m[:, 2 * DA:3 * DA]
    wd = m[:, 3 * DA:3 * DA + LORA]
    ad = m[:, 3 * DA + LORA:SHIFT_W]
    lw = w0_ref[...] + _dot3(jnp.tanh(wd), wup_ref[...])
    z = -lw
    softplus = jnp.maximum(z, 0.0) + jnp.log(1.0 + jnp.exp(-jnp.abs(z)))
    logw = -softplus - 0.5
    decay = jnp.exp(-jnp.exp(logw))
    a = _sigmoid(a0_ref[...] + _dot3(ad, aup_ref[...]))
    kk = k * kk_ref[...]
    ss = _dot_ones(kk * kk, ones_ref[...])
    kn = kk * lax.rsqrt(ss + 1e-12)
    r_o[...] = r
    w_o[...] = decay
    k_o[...] = k * (1.0 + (a - 1.0) * ka_ref[...])
    v_o[...] = v
    kn_o[...] = kn
    b_o[...] = kn * a


def _rwkv_prep(p, pinit, rw, ones_da, tm, carry_mode, period=1):
    t = p.shape[0]
    mu, w0, w_up, a0, a_up, k_k, k_a = rw[:7]
    row = lambda v: v.reshape(1, -1)
    full = lambda shape: pl.BlockSpec(shape, lambda i: (0, 0))
    pin_spec = full((1, SHIFT_W)) if carry_mode else pl.BlockSpec((tm, SHIFT_W), lambda i: (i, 0))
    out = jax.ShapeDtypeStruct((t, DA), F32)
    return pl.pallas_call(
        functools.partial(_prep_kernel, carry_mode, period),
        grid=(t // tm,),
        in_specs=[pl.BlockSpec((tm, 3 * DA), lambda i: (i, 0)),
                  pl.BlockSpec((tm, LANES), lambda i: (i, C_WDAD // LANES)),
                  pin_spec, full((1, SHIFT_W)), full((1, DA)), full((LORA, DA)), full((1, DA)), full((LORA, DA)),
                  full((1, DA)), full((1, DA)), full((DA, DA))],
        out_specs=[pl.BlockSpec((tm, DA), lambda i: (i, 0))] * 6,
        out_shape=[out] * 6,
        scratch_shapes=[pltpu.VMEM((1, SHIFT_W), F32)],
        compiler_params=_cparams("arbitrary"),
    )(p, p, pinit, row(mu), row(w0), w_up, row(a0), a_up, row(k_k), row(k_a), ones_da)


def _seg_sum_bcast(x, ones2_ref):
    hi = x.astype(BF16)
    lo = (x - hi.astype(F32)).astype(BF16)
    return _dot(jnp.concatenate([hi, lo], axis=1), ones2_ref[...])


def _scan_kernel(n_tok, r_ref, w_ref, k_ref, v_ref, kn_ref, b_ref, s0_ref, rk_ref, g_ref, beta_ref, ones_ref,
                 ones2_ref, y_o, s_o, s_ref, y_ref):
    c = pl.program_id(1)

    @pl.when(c == 0)
    def _():
        s_ref[...] = s0_ref[...]

    lane = lax.broadcasted_iota(I32, (DHA, LANES), 1)
    sub = lax.broadcasted_iota(I32, (DHA, LANES), 0)
    diag = (lane & (DHA - 1)) == sub
    diag_bf = jnp.where(diag, 1.0, 0.0).astype(BF16)
    lo_lanes = lane < DHA
    lo_row = lax.broadcasted_iota(I32, (SUBLANES, LANES), 1) < DHA
    pairs = range(N_PAIR)

    def on_diag(row):
        packed = jnp.broadcast_to(row, (2 * SUBLANES, LANES)).astype(BF16)
        return jnp.concatenate([packed] * (DHA // (2 * SUBLANES)), axis=0) * diag_bf

    def group(base, n):
        rows8 = pl.ds(base, SUBLANES)
        tile = lambda ref, p: ref[rows8, p * LANES:(p + 1) * LANES]
        v_hi = [tile(v_ref, p).astype(BF16).astype(F32) for p in pairs]
        v_lo = [tile(v_ref, p) - v_hi[p] for p in pairs]
        kn_a = [jnp.where(lo_row, tile(kn_ref, p), 0.0) for p in pairs]
        kn_b = [jnp.where(lo_row, 0.0, tile(kn_ref, p)) for p in pairs]
        kn_next = [pltpu.roll(tile(kn_ref, p), SUBLANES - 1, 0) for p in pairs]
        c_b = [_seg_sum_bcast(tile(b_ref, p) * kn_next[p], ones2_ref) for p in pairs]
        c_k = [_seg_sum_bcast(tile(k_ref, p) * kn_next[p], ones2_ref) for p in pairs]
        y_rows = [[] for _ in pairs]
        half_sums = lambda x, a, b: jnp.where(lo_lanes, jnp.sum(x * a, axis=1, keepdims=True),
                                              jnp.sum(x * b, axis=1, keepdims=True))
        for u in range(0, n, 2):
            r0, r1 = slice(u, u + 1), slice(u + 1, u + 2)
            vb = [[_dot(jnp.concatenate([on_diag(v_hi[p][r]), on_diag(v_lo[p][r])], axis=1), ones2_ref[...])
                   for r in (r0, r1)] for p in pairs]
            sa0, sa1 = [], []
            for p in pairs:
                s = s_ref[p]
                w0 = tile(w_ref, p)[r0]
                sa0.append(half_sums(s, kn_a[p][r0], kn_b[p][r0]))
                sa1.append(half_sums(s, w0 * kn_a[p][r1], w0 * kn_b[p][r1]))
            yb = []
            for p in pairs:
                fix = sa1[p] - sa0[p] * c_b[p][r0] + vb[p][0] * c_k[p][r0]
                s_mid = (s_ref[p] * tile(w_ref, p)[r0] - sa0[p] * tile(b_ref, p)[r0]
                         + vb[p][0] * tile(k_ref, p)[r0])
                s_new = s_mid * tile(w_ref, p)[r1] - fix * tile(b_ref, p)[r1] + vb[p][1] * tile(k_ref, p)[r1]
                s_ref[p] = s_new
                yb.append((_dot((s_mid * tile(r_ref, p)[r0]).astype(BF16), ones2_ref[:LANES, :]),
                           _dot((s_new * tile(r_ref, p)[r1]).astype(BF16), ones2_ref[:LANES, :])))
            for p in pairs:
                for y in yb[p]:
                    y_rows[p].append(jnp.sum(jnp.where(diag, y, 0.0), axis=0, keepdims=True))
        for p in pairs:
            rows = y_rows[p] + [jnp.zeros((1, LANES), F32)] * (SUBLANES - n)
            y_ref[rows8, p * LANES:(p + 1) * LANES] = jnp.concatenate(rows, axis=0)

    n_full = n_tok // SUBLANES

    def full_group(g, carry):
        group(pl.multiple_of(g * SUBLANES, SUBLANES), SUBLANES)
        return carry

    lax.fori_loop(0, n_full, full_group, 0)
    if n_tok % SUBLANES:
        group(n_full * SUBLANES, n_tok % SUBLANES)

    ones = ones_ref[...]
    y = y_ref[...]
    mean = _dot_ones(y, ones) * (1.0 / DHA)
    yc = y - mean
    var = _dot_ones(yc * yc, ones) * (1.0 / DHA)
    yn = yc * lax.rsqrt(var + LNX_EPS) * g_ref[...] + beta_ref[...]
    bonus = _dot_ones(r_ref[...] * k_ref[...] * rk_ref[...], ones) * v_ref[...]
    y_o[...] = yn + bonus

    @pl.when(c == pl.num_programs(1) - 1)
    def _():
        s_o[...] = s_ref[...]


def _pack_state(s):
    b = s.shape[0]
    return s.reshape(b, N_PAIR, 2, DHA, DHA).transpose(0, 1, 3, 2, 4).reshape(b, N_PAIR, DHA, LANES)


def _unpack_state(s):
    b = s.shape[0]
    return s.reshape(b, N_PAIR, DHA, 2, DHA).transpose(0, 1, 3, 2, 4).reshape(b, HA, DHA, DHA)


def _rwkv_scan(seqs, s0, rw, ones_da, tc, n_tok):
    bsz, t, _ = seqs[0].shape
    assert tc == -(-n_tok // SUBLANES) * SUBLANES
    assert n_tok % 2 == 0
    r_k, lnx_g, lnx_b = rw[7:10]
    row = lambda v: v.reshape(1, DA)
    seq_spec = pl.BlockSpec((None, tc, DA), lambda b, c: (b, c, 0))
    st_spec = pl.BlockSpec((None, N_PAIR, DHA, LANES), lambda b, c: (b, 0, 0, 0))
    full = lambda shape: pl.BlockSpec(shape, lambda b, c: (0, 0))
    y, s_fin = pl.pallas_call(
        functools.partial(_scan_kernel, n_tok),
        grid=(bsz, t // tc),
        in_specs=[seq_spec] * 6 + [st_spec, full((1, DA)), full((1, DA)), full((1, DA)), full((DA, DA)),
                                   full((2 * LANES, LANES))],
        out_specs=[seq_spec, st_spec],
        out_shape=[jax.ShapeDtypeStruct((bsz, t, DA), F32), jax.ShapeDtypeStruct((bsz, N_PAIR, DHA, LANES), F32)],
        scratch_shapes=[pltpu.VMEM((N_PAIR, DHA, LANES), F32), pltpu.VMEM((tc, DA), F32)],
        compiler_params=_cparams("parallel", "arbitrary"),
    )(*seqs, _pack_state(s0), row(r_k), row(lnx_g), row(lnx_b), ones_da,
      jnp.concatenate([ones_da[:LANES, :LANES]] * 2, axis=0))
    return y, _unpack_state(s_fin)


def _rope_kernel(qb_ref, kb_ref, vb_ref, qi_ref, ki_ref, c128_ref, s128_ref, c64_ref, s64_ref,
                 q_o, k_o, v_o, kbf_o, vbf_o, qi_o, ki_o, kcat_o):
    c128, s128, c64, s64 = c128_ref[...], s128_ref[...], c64_ref[...], s64_ref[...]
    lane = lax.broadcasted_iota(I32, c64.shape, 1)
    first_half = (lane & (DI - 1)) < DI // 2

    def rope128(x):
        return x * c128 + pltpu.roll(x, DHB // 2, 1) * s128

    def rope64(x):
        partner = jnp.where(first_half, pltpu.roll(x, LANES - DI // 2, 1), pltpu.roll(x, DI // 2, 1))
        return x * c64 + partner * s64

    for h in range(HB):
        sl = slice(h * DHB, (h + 1) * DHB)
        q_o[:, sl] = rope128(qb_ref[:, sl]).astype(BF16)
        kr = rope128(kb_ref[:, sl])
        k_o[h] = kr
        v_o[h] = vb_ref[:, sl]
        kbf_o[:, sl] = kr.astype(BF16)
    vbf_o[...] = vb_ref[...].astype(BF16)
    for h2 in range(HI * DI // LANES):
        sl = slice(h2 * LANES, (h2 + 1) * LANES)
        qi_o[:, sl] = rope64(qi_ref[:, sl])
    ki = rope64(ki_ref[...])[:, :DI]
    ki_o[...] = ki
    kcat_o[...] = _index_key_cat(ki)


def _rope_tables(pos, d):
    inv = ROPE_THETA ** (-jnp.arange(0, d, 2, dtype=F32) / d)
    ang = pos.astype(F32)[:, None] * inv[None, :]
    cos, sin = jnp.cos(ang), jnp.sin(ang)
    reps = LANES // d
    c = jnp.tile(jnp.concatenate([cos, cos], axis=1), (1, reps))
    s = jnp.tile(jnp.concatenate([-sin, sin], axis=1), (1, reps))
    return c, s


def _rope(p, pos, tm):
    t = p.shape[0]
    c128, s128 = _rope_tables(pos, DHB)
    c64, s64 = _rope_tables(pos, DI)
    col = lambda w, c0: pl.BlockSpec((tm, w), lambda i: (i, c0 // w))
    rows = lambda w: pl.BlockSpec((tm, w), lambda i: (i, 0))
    sds = lambda w, dt: jax.ShapeDtypeStruct((t, w), dt)
    heads = pl.BlockSpec((HB, tm, DHB), lambda i: (0, i, 0))
    hsds = jax.ShapeDtypeStruct((HB, t, DHB), F32)
    return pl.pallas_call(
        _rope_kernel,
        grid=(t // tm,),
        in_specs=[col(DB, C_QB), col(DB, C_KB), col(DB, C_VB), col(HI * DI, C_QI), col(LANES, C_KI),
                  rows(LANES), rows(LANES), rows(LANES), rows(LANES)],
        out_specs=[rows(DB), heads, heads, rows(DB), rows(DB), rows(HI * DI), rows(DI), rows(4 * DI)],
        out_shape=[sds(DB, BF16), hsds, hsds, sds(DB, BF16), sds(DB, BF16), sds(HI * DI, F32), sds(DI, F32),
                   sds(4 * DI, BF16)],
        compiler_params=_cparams("parallel"),
    )(p, p, p, p, p, c128, s128, c64, s64)


def _canonical(score):
    return jnp.where(score == 0.0, 0.0, score)


def _ordinal_to_float(u):
    s = u ^ INT_MIN
    b = s ^ ((s >> 31) & 0x7FFFFFFF)
    return lax.bitcast_convert_type(b, F32)


def _index_query_cat(qi):
    qi_hi, qi_lo = _split2(qi)
    out = []
    for h in range(HI):
        sl = slice(h * DI, (h + 1) * DI)
        out.append(jnp.concatenate([qi_hi[:, sl], qi_hi[:, sl], qi_lo[:, sl], qi_lo[:, sl]], axis=1))
    return out


def _index_key_cat(ki):
    k_hi, k_lo = _split2(ki)
    return jnp.concatenate([k_hi, k_lo, k_hi, k_lo], axis=1)


def _count(sc_ref, nc, ck, pred):
    n_acc = 8

    def body(c, accs):
        accs = list(accs)
        base = pl.multiple_of(c * ck, ck)
        for j in range(ck // SUBLANES):
            row0 = base + j * SUBLANES
            accs[j % n_acc] = accs[j % n_acc] + jnp.where(pred(sc_ref[pl.ds(row0, SUBLANES), :], row0), 1, 0)
        return tuple(accs)

    accs = lax.fori_loop(0, nc, body, (jnp.zeros((SUBLANES, LANES), I32),) * n_acc)
    return jnp.sum(functools.reduce(lambda a, b: a + b, accs), axis=0, keepdims=True)


def _select_bias(sc_ref, bias_ref, cut_ref, nc, nc_total, ck, topk, idx_bits):
    sub8 = lax.broadcasted_iota(I32, (SUBLANES, LANES), 0)
    sub128 = lax.broadcasted_iota(I32, (LANES, LANES), 0)

    bits_per_check = 4

    def bit_group(g, carry):
        def refine(carry):
            ans_u, done = carry
            for b in range(bits_per_check):
                cand_u = ans_u | lax.shift_left(jnp.int32(1), 31 - (g * bits_per_check + b))
                cand = _ordinal_to_float(cand_u)
                cnt = _count(sc_ref, nc, ck, lambda blk, _: blk >= cand)
                take = (cnt >= topk) & (done == 0)
                ans_u, done = jnp.where(take, cand_u, ans_u), jnp.where(take & (cnt == topk), 1, done)
            return ans_u, done

        return lax.cond(jnp.min(carry[1]) == 0, refine, lambda carry: carry, carry)

    zeros = jnp.zeros((1, LANES), I32)
    ans_u = lax.fori_loop(0, 32 // bits_per_check, bit_group, (zeros, zeros))[0]
    ans_u = jnp.where((ans_u >= 0) & (ans_u < ORD_NEG_INF), ORD_NEG_INF, ans_u)
    thr = _ordinal_to_float(ans_u)
    n_gt = _count(sc_ref, nc, ck, lambda blk, _: blk > thr)
    n_ge = _count(sc_ref, nc, ck, lambda blk, _: blk >= thr)
    need = topk - n_gt
    tied = ((n_ge - n_gt) > need) & (thr > -jnp.inf)
    cut_ref[...] = jnp.full((1, LANES), 2 ** 30, I32)

    @pl.when(jnp.max(tied.astype(I32)) > 0)
    def _():
        def tie_body(it, ans):
            cand = ans | lax.shift_left(jnp.int32(1), idx_bits - 1 - it)
            g = _count(sc_ref, nc, ck, lambda blk, row0: (blk == thr) & ((sub8 + row0) < cand))
            return jnp.where(g < need, cand, ans)

        cut = lax.fori_loop(0, idx_bits, tie_body, jnp.zeros((1, LANES), I32))
        cut_ref[...] = jnp.where(tied, cut, 2 ** 30)

    cut = cut_ref[...]

    def write(c, carry):
        base = pl.multiple_of(c * ck, ck)
        for u in range(ck // LANES):
            off = base + u * LANES
            blk = sc_ref[pl.ds(off, LANES), :]
            sel = ((blk > thr) | ((blk == thr) & ((sub128 + off) <= cut))) & (blk > -jnp.inf)
            bias_ref[:, pl.ds(off, LANES)] = jnp.where(sel, 0.0, NEG_BIAS).T.astype(bias_ref.dtype)
        return carry

    lax.fori_loop(0, nc, write, 0)

    def fill(c, carry):
        base = pl.multiple_of(c * ck, ck)
        bias_ref[:, pl.ds(base, ck)] = jnp.full((bias_ref.shape[0], ck), NEG_BIAS, bias_ref.dtype)
        return carry

    lax.fori_loop(nc, nc_total, fill, 0)


def _prompt_select_kernel(topk, ck, idx_bits, qi_ref, wi_ref, kcat_ref, bias_o, sc_ref, cut_ref):
    i = pl.program_id(0)
    tq = qi_ref.shape[0]
    t = kcat_ref.shape[0]
    nc = ((i + 1) * tq + ck - 1) // ck
    q_cat = _index_query_cat(qi_ref[...])
    q_pair = [jnp.concatenate(q_cat[2 * j:2 * j + 2], axis=0) for j in range(HI // 2)]
    wi_t = wi_ref[...].T
    kpos = lax.broadcasted_iota(I32, (ck, tq), 0)
    qpos = i * tq + lax.broadcasted_iota(I32, (ck, tq), 1)
    scale = HI ** -0.5 * DI ** -0.5

    def score_chunk(c, carry):
        base = pl.multiple_of(c * ck, ck)
        k_cat = kcat_ref[pl.ds(base, ck), :]
        acc = None
        for h2 in range(HI // 2):
            s2 = _dot_nt(k_cat, q_pair[h2])
            for e in range(2):
                h = 2 * h2 + e
                term = wi_t[h:h + 1, :] * jnp.maximum(s2[:, e * tq:(e + 1) * tq], 0.0)
                acc = term if acc is None else acc + term
        sc = jnp.where(kpos + base <= qpos, acc * scale, -jnp.inf)
        sc_ref[pl.ds(base, ck), :] = _canonical(sc)
        return carry

    lax.fori_loop(0, nc, score_chunk, 0)
    _select_bias(sc_ref, bias_o, cut_ref, nc, t // ck, ck, topk, idx_bits)


def _prompt_select(qi, p, k_cat, ck):
    t = qi.shape[0]
    tq = LANES
    topk = min(TOPK_MAX, t // 4)
    idx_bits = int(np.ceil(np.log2(t))) + 1
    return pl.pallas_call(
        functools.partial(_prompt_select_kernel, topk, ck, idx_bits),
        grid=(t // tq,),
        in_specs=[pl.BlockSpec((tq, HI * DI), lambda i: (i, 0)),
                  pl.BlockSpec((tq, LANES), lambda i: (i, C_WI // LANES)),
                  pl.BlockSpec((t, 4 * DI), lambda i: (0, 0))],
        out_specs=pl.BlockSpec((tq, t), lambda i: (i, 0)),
        out_shape=jax.ShapeDtypeStruct((t, t), BF16),
        scratch_shapes=[pltpu.VMEM((t, tq), F32), pltpu.VMEM((1, LANES), I32)],
        compiler_params=_cparams("parallel"),
    )(qi, p, k_cat)


def _sample_score_kernel(pps, rq, n_new, pt_ref, qit_ref, wif_ref, *refs):
    page_refs, knew_ref, skey_o, skey_new_o = refs[:pps], refs[pps], refs[pps + 1], refs[pps + 2]
    j = pl.program_id(1)
    q_hi, q_lo = _split2(qit_ref[...])
    wif = wif_ref[...]

    def scores(kt):
        k_hi, k_lo = _split2(kt)
        at = _dot(q_hi, k_hi) + _dot(q_hi, k_lo) + _dot(q_lo, k_hi)
        at = jnp.maximum(at, 0.0) * wif
        sc = at[0:rq]
        for h in range(1, HI):
            sc = sc + at[h * rq:(h + 1) * rq]
        return sc * (HI ** -0.5 * DI ** -0.5)

    sc = scores(jnp.concatenate([r[...] for r in page_refs], axis=1))
    trow = lax.broadcasted_iota(I32, sc.shape, 0)
    skey_o[...] = _canonical(jnp.where(trow < n_new, sc, -jnp.inf))

    @pl.when(j == pl.num_programs(1) - 1)
    def _():
        scn = scores(knew_ref[...])
        tr = lax.broadcasted_iota(I32, scn.shape, 0)
        ln = lax.broadcasted_iota(I32, scn.shape, 1)
        skey_new_o[...] = _canonical(jnp.where((tr < n_new) & (ln <= tr), scn, -jnp.inf))


def _sample_scores(qit, wif, cache_kidx, knew_pad, page_table, n_new, rq, pps):
    bsz, n_pages = page_table.shape
    per_b = lambda b, j, pt: (b, 0, 0)
    page_spec = lambda u: pl.BlockSpec((None, DI, PAGE), lambda b, j, pt: (pt[b, j * pps + u], 0, 0))
    grid_spec = pltpu.PrefetchScalarGridSpec(
        num_scalar_prefetch=1,
        grid=(bsz, n_pages // pps),
        in_specs=[pl.BlockSpec((None, LANES, DI), per_b), pl.BlockSpec((None, LANES, 1), per_b)]
                 + [page_spec(u) for u in range(pps)] + [pl.BlockSpec((None, DI, PAGE), per_b)],
        out_specs=[pl.BlockSpec((None, rq, pps * PAGE), lambda b, j, pt: (b, 0, j)),
                   pl.BlockSpec((None, rq, PAGE), per_b)],
    )
    return pl.pallas_call(
        functools.partial(_sample_score_kernel, pps, rq, n_new),
        grid_spec=grid_spec,
        out_shape=[jax.ShapeDtypeStruct((bsz, rq, n_pages * PAGE), F32), jax.ShapeDtypeStruct((bsz, rq, PAGE), F32)],
        compiler_params=_cparams("parallel", "arbitrary"),
    )(page_table, qit, wif, *([cache_kidx] * pps), knew_pad)


def _sample_select_kernel(topk, ck, idx_bits, sc_past, sc_new, bias_o, sc_ref, cut_ref):
    past = sc_past.shape[1]

    def flip(c, carry):
        off = pl.multiple_of(c * LANES, LANES)
        sc_ref[pl.ds(off, LANES), :] = sc_past[:, pl.ds(off, LANES)].T
        return carry

    lax.fori_loop(0, past // LANES, flip, 0)
    sc_ref[past:, :] = sc_new[...].T
    nc = sc_ref.shape[0] // ck
    _select_bias(sc_ref, bias_o, cut_ref, nc, nc, ck, topk, idx_bits)


def _sample_select(skey_past, skey_new, topk):
    rows, past = skey_past.shape
    n = past + skey_new.shape[1]
    tr = LANES
    idx_bits = int(np.ceil(np.log2(n))) + 1
    return pl.pallas_call(
        functools.partial(_sample_select_kernel, topk, PAGE, idx_bits),
        grid=(rows // tr,),
        in_specs=[pl.BlockSpec((tr, past), lambda i: (i, 0)), pl.BlockSpec((tr, n - past), lambda i: (i, 0))],
        out_specs=pl.BlockSpec((tr, n), lambda i: (i, 0)),
        out_shape=jax.ShapeDtypeStruct((rows, n), F32),
        scratch_shapes=[pltpu.VMEM((n, tr), F32), pltpu.VMEM((1, LANES), I32)],
        compiler_params=_cparams("parallel"),
    )(skey_past, skey_new)


def _prompt_attn_kernel(ck, q_ref, k_ref, v_ref, bias_ref, o_ref, m_ref, l_ref, acc_ref):
    i = pl.program_id(0)
    tq = q_ref.shape[0]
    nc = ((i + 1) * tq + ck - 1) // ck
    m_ref[...] = jnp.full(m_ref.shape, NEG_BIAS, F32)
    l_ref[...] = jnp.zeros(l_ref.shape, F32)
    acc_ref[...] = jnp.zeros(acc_ref.shape, F32)
    scale = DHB ** -0.5

    def chunk(c, carry):
        base = pl.multiple_of(c * ck, ck)
        bias = bias_ref[:, pl.ds(base, ck)].astype(F32)
        for h in range(HB):
            sl = slice(h * DHB, (h + 1) * DHB)
            s = _dot_nt(q_ref[:, sl], k_ref[pl.ds(base, ck), sl]) * scale + bias
            m_old = m_ref[h]
            m_new = jnp.maximum(m_old, jnp.max(s, axis=1, keepdims=True))
            alpha = jnp.exp(m_old - m_new)
            p = jnp.exp(s - m_new)
            l_ref[h] = alpha * l_ref[h] + jnp.sum(p, axis=1, keepdims=True)
            acc_ref[h] = alpha * acc_ref[h] + _dot(p.astype(BF16), v_ref[pl.ds(base, ck), sl])
            m_ref[h] = m_new
        return carry

    lax.fori_loop(0, nc, chunk, 0)
    for h in range(HB):
        o_ref[:, h * DHB:(h + 1) * DHB] = acc_ref[h] / l_ref[h]


def _prompt_attn(q_bf, k_bf, v_bf, bias, tq, ck):
    t = q_bf.shape[0]
    resident = lambda: pl.BlockSpec((t, DB), lambda i: (0, 0), pipeline_mode=pl.Buffered(1))
    return pl.pallas_call(
        functools.partial(_prompt_attn_kernel, ck),
        grid=(t // tq,),
        in_specs=[pl.BlockSpec((tq, DB), lambda i: (i, 0)), resident(), resident(),
                  pl.BlockSpec((tq, t), lambda i: (i, 0))],
        out_specs=pl.BlockSpec((tq, DB), lambda i: (i, 0)),
        out_shape=jax.ShapeDtypeStruct((t, DB), F32),
        scratch_shapes=[pltpu.VMEM((HB, tq, 1), F32), pltpu.VMEM((HB, tq, 1), F32), pltpu.VMEM((HB, tq, DHB), F32)],
        compiler_params=_cparams("parallel"),
    )(q_bf, k_bf, v_bf, bias)


def _sample_attn_kernel(pps, rq, pt_ref, qbd_ref, *refs):
    kpages, vpages = refs[:pps], refs[pps:2 * pps]
    knew_ref, vnew_ref, bias_ref, bias_new_ref, o_ref, m_ref, l_ref, acc_ref = refs[2 * pps:]
    j = pl.program_id(1)
    nr = HB * rq

    @pl.when(j == 0)
    def _():
        m_ref[...] = jnp.full(m_ref.shape, NEG_BIAS, F32)
        l_ref[...] = jnp.zeros(l_ref.shape, F32)
        acc_ref[...] = jnp.zeros(acc_ref.shape, F32)

    def accumulate(kp, vp, bias):
        st = _dot(kp, qbd_ref[...])
        s = st.T[:nr] * (DHB ** -0.5) + jnp.concatenate([bias] * HB, axis=0)
        m_old = m_ref[...]
        m_new = jnp.maximum(m_old, jnp.max(s, axis=1, keepdims=True))
        alpha = jnp.exp(m_old - m_new)
        p = jnp.exp(s - m_new)
        l_ref[...] = alpha * l_ref[...] + jnp.sum(p, axis=1, keepdims=True)
        acc_ref[...] = alpha * acc_ref[...] + _dot(p.astype(BF16), vp)
        m_ref[...] = m_new

    gather = lambda pages: jnp.concatenate(
        [jnp.concatenate([r[h].astype(BF16) for h in range(HB)], axis=1) for r in pages], axis=0)
    accumulate(gather(kpages), gather(vpages), bias_ref[...])

    @pl.when(j == pl.num_programs(1) - 1)
    def _():
        accumulate(knew_ref[...].astype(BF16), vnew_ref[...].astype(BF16), bias_new_ref[...])
        for h in range(HB):
            rs = slice(h * rq, (h + 1) * rq)
            cs = slice(h * DHB, (h + 1) * DHB)
            o_ref[:, cs] = acc_ref[rs, cs] / l_ref[rs, :]


def _sample_attn(qbd, cache_k, cache_v, knew_pad, vnew_pad, bias, page_table, rq, pps):
    bsz, n_pages = page_table.shape
    page_spec = lambda u: pl.BlockSpec((None, HB, PAGE, DHB), lambda b, j, pt: (pt[b, j * pps + u], 0, 0, 0))
    per_b = lambda b, j, pt: (b, 0, 0)
    grid_spec = pltpu.PrefetchScalarGridSpec(
        num_scalar_prefetch=1,
        grid=(bsz, n_pages // pps),
        in_specs=[pl.BlockSpec((None, DB, LANES), per_b)]
                 + [page_spec(u) for u in range(pps)] * 2
                 + [pl.BlockSpec((None, PAGE, DB), per_b), pl.BlockSpec((None, PAGE, DB), per_b),
                    pl.BlockSpec((None, rq, pps * PAGE), lambda b, j, pt: (b, 0, j)),
                    pl.BlockSpec((None, rq, PAGE), lambda b, j, pt: (b, 0, n_pages))],
        out_specs=pl.BlockSpec((None, rq, DB), per_b),
        scratch_shapes=[pltpu.VMEM((HB * rq, 1), F32), pltpu.VMEM((HB * rq, 1), F32), pltpu.VMEM((HB * rq, DB), F32)],
    )
    return pl.pallas_call(
        functools.partial(_sample_attn_kernel, pps, rq),
        grid_spec=grid_spec,
        out_shape=jax.ShapeDtypeStruct((bsz, rq, DB), F32),
        compiler_params=_cparams("parallel", "arbitrary"),
    )(page_table, qbd, *([cache_k] * pps), *([cache_v] * pps), knew_pad, vnew_pad, bias, bias)


def _cross_kernel(q_ref, mk_ref, mv_ref, o_ref):
    scale = DHC ** -0.5
    for h in range(HC):
        sl = slice(h * DHC, (h + 1) * DHC)
        s = _dot_nt(q_ref[:, sl].astype(BF16), mk_ref[h].astype(BF16)) * scale
        m = jnp.max(s, axis=1, keepdims=True)
        p = jnp.exp(s - m)
        l = jnp.sum(p, axis=1, keepdims=True)
        o_ref[:, sl] = _dot(p.astype(BF16), mv_ref[h].astype(BF16)) / l


def _cross_attn(q_arr, q_col, mk, mv, tm):
    bsz, t, _ = q_arr.shape
    return pl.pallas_call(
        _cross_kernel,
        grid=(bsz, t // tm),
        in_specs=[pl.BlockSpec((None, tm, DC), lambda b, i: (b, i, q_col // DC)),
                  pl.BlockSpec((None, HC, N_MEM, DHC), lambda b, i: (b, 0, 0, 0)),
                  pl.BlockSpec((None, HC, N_MEM, DHC), lambda b, i: (b, 0, 0, 0))],
        out_specs=pl.BlockSpec((None, tm, DC), lambda b, i: (b, i, 0)),
        out_shape=jax.ShapeDtypeStruct((bsz, t, DC), F32),
        compiler_params=_cparams("parallel", "parallel"),
    )(q_arr, mk, mv)


def _merge_kernel(x_ref, ya_ref, za_ref, yb_ref, zb_ref, yc_ref, zc_ref, ga_ref, gb_ref, gc_ref,
                  bga_ref, bgb_ref, bgc_ref, wa_ref, wb_ref, wc_ref, wo_ref, fg_ref, o_ref):
    ba = _dot((ya_ref[...] * _silu(za_ref[...])).astype(BF16), wa_ref[...])
    bb = _dot((yb_ref[...] * _silu(zb_ref[...])).astype(BF16), wb_ref[...])
    bc = _dot((yc_ref[...] * _silu(zc_ref[...])).astype(BF16), wc_ref[...])
    mix = (_sigmoid(ga_ref[...] + bga_ref[...]) * ba + _sigmoid(gb_ref[...] + bgb_ref[...]) * bb
           + _sigmoid(gc_ref[...] + bgc_ref[...]) * bc)
    y = x_ref[...] + _dot(mix.astype(BF16), wo_ref[...])
    ms = jnp.mean(y * y, axis=-1, keepdims=True)
    o_ref[...] = y * lax.rsqrt(ms + EPS) * fg_ref[...]


def _merge(x2d, p, ya, yb, yc, b_gate, wa, wb, wc, wo, fg, tm):
    t, d = x2d.shape
    rows = lambda w: pl.BlockSpec((tm, w), lambda i: (i, 0))
    col = lambda w, c0: pl.BlockSpec((tm, w), lambda i: (i, c0 // w))
    const = lambda shape, c=0: pl.BlockSpec(shape, lambda i: (0, c), pipeline_mode=pl.Buffered(1))
    bg = b_gate.reshape(1, 3 * d)
    return pl.pallas_call(
        _merge_kernel,
        grid=(t // tm,),
        in_specs=[rows(d), rows(DA), col(DA, C_ZA), rows(DB), col(DB, C_ZB), rows(DC), col(DC, C_ZC),
                  col(d, C_GATES), col(d, C_GATES + d), col(d, C_GATES + 2 * d),
                  const((1, d), 0), const((1, d), 1), const((1, d), 2),
                  const((DA, d)), const((DB, d)), const((DC, d)), const((d, d)), const((1, d))],
        out_specs=rows(d),
        out_shape=jax.ShapeDtypeStruct((t, d), F32),
        compiler_params=_cparams("parallel"),
    )(x2d, ya, p, yb, p, yc, p, p, p, p, bg, bg, bg, wa, wb, wc, wo, fg.reshape(1, d))


def _block_ones(n, seg):
    idx = np.arange(n) // seg
    return jnp.asarray(idx[:, None] == idx[None, :], dtype=BF16)


def _shift_row(p_row):
    return jnp.concatenate([p_row[:3 * DA], p_row[C_WDAD:C_WDAD + 2 * LORA]])


def kernel(x_prompt, x_sample, mem_prompt, cache_k, cache_v, cache_kidx, cache_mem_k, cache_mem_v, state_wkv, state_shift, page_table, norm_g, w_in, rwkv_mu, rwkv_w0, rwkv_w_up, rwkv_a0, rwkv_a_up, rwkv_k_k, rwkv_k_a, rwkv_r_k, rwkv_lnx_g, rwkv_lnx_b, mem_norm_g, w_mem_kv, b_gate, w_br_a, w_br_b, w_br_c, w_out, final_norm_g):
    depth = w_in.shape[0]
    assert depth == 1 and x_prompt.shape[0] == 1
    t = x_prompt.shape[1]
    bs, ts, d = x_sample.shape
    n_pages = page_table.shape[1]
    past = n_pages * PAGE
    rq = 8
    assert ts <= rq and (ts & (ts - 1)) == 0

    rw = (rwkv_mu[0], rwkv_w0[0], rwkv_w_up[0], rwkv_a0[0], rwkv_a_up[0], rwkv_k_k[0], rwkv_k_a[0],
          rwkv_r_k[0].reshape(DA), rwkv_lnx_g[0], rwkv_lnx_b[0])
    ones_da = _block_ones(DA, DHA)
    w_p = _permute_w_in(w_in[0])
    wa, wb, wc, wo = (w[0].astype(BF16) for w in (w_br_a, w_br_b, w_br_c, w_out))

    xp2 = x_prompt.reshape(t, d)
    xs2 = x_sample.reshape(bs * ts, d)
    tm_p = min(512, t)
    p_p = _proj(xp2, norm_g[0], w_p, tm_p, 1024)
    p_s = _proj(xs2, norm_g[0], w_p, bs * ts, 1024)
    mkv = _proj(mem_prompt.reshape(N_MEM, d), mem_norm_g[0], w_mem_kv[0].astype(BF16), N_MEM, 2 * DC)
    mk_p = mkv[:, :DC].reshape(1, N_MEM, HC, DHC)
    mv_p = mkv[:, DC:].reshape(1, N_MEM, HC, DHC)
    head_major = lambda a: jnp.transpose(a, (0, 2, 1, 3))

    seq_p = _rwkv_prep(p_p, jnp.zeros((1, SHIFT_W), F32), rw, ones_da, min(256, t), True)
    ya_p, wkv_p = _rwkv_scan([a[None] for a in seq_p], jnp.zeros((1, HA, DHA, DHA), F32), rw, ones_da,
                             min(256, t), min(256, t))
    pinit_s = jnp.repeat(state_shift[0], ts, axis=0)
    seq_s = _rwkv_prep(p_s, pinit_s, rw, ones_da, bs * ts, False, ts)
    pad_t = lambda a: jnp.pad(a.reshape(bs, ts, -1), ((0, 0), (0, rq - ts), (0, 0)))
    ya_s, wkv_s = _rwkv_scan([pad_t(a) for a in seq_s], state_wkv[0], rw, ones_da, rq, ts)
    ya_s = ya_s[:, :ts].reshape(bs * ts, DA)

    tokens_major = lambda a: jnp.transpose(a, (1, 0, 2))
    q_p, k_p, v_p, kbf_p, vbf_p, qi_p, ki_p, kcat_p = _rope(p_p, jnp.arange(t), min(512, t))
    pos_s = jnp.tile(past + jnp.arange(ts), bs)
    q_s, k_s, v_s, _, _, qi_s, ki_s, _ = _rope(p_s, pos_s, bs * ts)
    k_s, v_s = tokens_major(k_s), tokens_major(v_s)

    bias_p = _prompt_select(qi_p, p_p, kcat_p, 512)
    yb_p = _prompt_attn(q_p, kbf_p, vbf_p, bias_p, min(256, t), 1024)

    topk_s = min(TOPK_MAX, (past + ts) // 4)
    pad_rows = lambda a, n: jnp.pad(a, ((0, 0), (0, n - a.shape[1]), (0, 0)))
    pps = 8 if n_pages % 8 == 0 else 1
    qit = pad_rows(qi_s.reshape(bs, ts, HI * DI), rq).reshape(bs, rq, HI, DI).transpose(0, 2, 1, 3)
    qit = pad_rows(qit.reshape(bs, HI * rq, DI), LANES)
    wif = pad_rows(p_s[:, C_WI:C_WI + HI].reshape(bs, ts, HI), rq).transpose(0, 2, 1).reshape(bs, HI * rq, 1)
    wif = pad_rows(wif, LANES)
    kinew = jnp.swapaxes(pad_rows(ki_s.reshape(bs, ts, DI), PAGE), 1, 2)
    skey_past, skey_new = _sample_scores(qit, wif, jnp.swapaxes(cache_kidx[0], 1, 2), kinew, page_table, ts,
                                         rq, pps)
    n_rows = -(-bs * rq // LANES) * LANES
    lane_rows = lambda a: jnp.pad(a.reshape(bs * rq, -1), ((0, n_rows - bs * rq), (0, 0)), constant_values=-jnp.inf)
    bias_s = _sample_select(lane_rows(skey_past), lane_rows(skey_new), topk_s)
    bias_s = bias_s[:bs * rq].reshape(bs, rq, past + PAGE)
    knew = pad_rows(k_s.reshape(bs, ts, DB), PAGE)
    vnew = pad_rows(v_s.reshape(bs, ts, DB), PAGE)
    qh = pad_rows(q_s.reshape(bs, ts, DB), rq).reshape(bs, rq, HB, DHB).transpose(0, 2, 3, 1)
    head_eq = jnp.eye(HB, dtype=bool)[None, :, None, :, None]
    qbd = jnp.where(head_eq, qh[:, :, :, None, :], jnp.zeros((), BF16)).reshape(bs, DB, HB * rq)
    qbd = jnp.pad(qbd, ((0, 0), (0, 0), (0, LANES - HB * rq)))
    yb_s = _sample_attn(qbd, head_major(cache_k[0]), head_major(cache_v[0]), knew, vnew, bias_s, page_table,
                        rq, pps)
    yb_s = yb_s[:, :ts].reshape(bs * ts, DB)

    yc_p = _cross_attn(p_p[None], C_QC, head_major(mk_p), head_major(mv_p), min(512, t))[0]
    qc8 = pad_rows(p_s[:, C_QC:C_QC + DC].reshape(bs, ts, DC), rq)
    yc_s = _cross_attn(qc8, 0, head_major(cache_mem_k[0]), head_major(cache_mem_v[0]), rq)
    yc_s = yc_s[:, :ts].reshape(bs * ts, DC)

    y_p = _merge(xp2, p_p, ya_p[0], yb_p, yc_p, b_gate[0], wa, wb, wc, wo, final_norm_g, min(128, t))
    y_s = _merge(xs2, p_s, ya_s, yb_s, yc_s, b_gate[0], wa, wb, wc, wo, final_norm_g, bs * ts)

    p_s3 = p_s.reshape(bs, ts, N_P)
    shift_s = jnp.concatenate([p_s3[:, -1, :3 * DA], p_s3[:, -1, C_WDAD:C_WDAD + 2 * LORA]], axis=-1)
    return (y_p.reshape(1, t, d), y_s.reshape(bs, ts, d),
            tokens_major(k_p).reshape(1, 1, t, HB, DHB), tokens_major(v_p).reshape(1, 1, t, HB, DHB),
            ki_p.reshape(1, 1, t, DI),
            mk_p.reshape(1, 1, N_MEM, HC, DHC), mv_p.reshape(1, 1, N_MEM, HC, DHC),
            wkv_p.reshape(1, 1, HA, DHA, DHA), _shift_row(p_p[t - 1]).reshape(1, 1, SHIFT_W),
            k_s.reshape(1, bs, ts, HB, DHB), v_s.reshape(1, bs, ts, HB, DHB), ki_s.reshape(1, bs, ts, DI),
            wkv_s.reshape(1, bs, HA, DHA, DHA), shift_s.reshape(1, bs, SHIFT_W))
```

```python
import functools

import numpy as np
import jax
import jax.numpy as jnp
from jax import lax
from jax.experimental import pallas as pl
from jax.experimental.pallas import tpu as pltpu

F32, BF16, I32 = jnp.float32, jnp.bfloat16, jnp.int32

D_MODEL = 2048
HA, DHA = 12, 64
DA = HA * DHA
LORA = 64
LNX_EPS = 64e-5
HB, DHB = 6, 128
DB = HB * DHB
HI, DI = 8, 64
TOPK_MAX = 256
N_MEM = 256
HC, DHC = 4, 128
DC = HC * DHC
ROPE_THETA = 10000.0
EPS = 1e-6
PAGE = 128
SHIFT_W = 3 * DA + 2 * LORA
SPLIT_SIZES = (SHIFT_W, DA, DB, DB, DB, HI * DI, HI, DI, DB, DC, DC, 3 * D_MODEL)

LANES = 128
SUBLANES = 8
N_PAIR = HA // 2

C_RKV, C_ZA, C_QB, C_KB, C_VB, C_ZB = 0, 2304, 3072, 3840, 4608, 5376
C_QI, C_QC, C_ZC, C_WDAD, C_KI, C_WI, C_GATES = 6144, 6656, 7168, 7680, 7808, 7936, 8192
N_P = C_GATES + 3 * D_MODEL

NEG_BIAS = -1e30
ORD_NEG_INF = 0x007FFFFF
INT_MIN = -2147483648
VMEM_LIMIT = 56 * 1024 * 1024


def _cparams(*sem):
    return pltpu.CompilerParams(dimension_semantics=sem, vmem_limit_bytes=VMEM_LIMIT)


def _dot(a, b):
    return jnp.dot(a, b, preferred_element_type=F32)


def _dot_nt(a, b):
    return lax.dot_general(a, b, (((1,), (1,)), ((), ())), preferred_element_type=F32)


def _split2(x):
    hi = x.astype(BF16)
    lo = (x - hi.astype(F32)).astype(BF16)
    return hi, lo


def _dot_ones(x, ones_bf):
    hi = x.astype(BF16)
    r1 = x - hi.astype(F32)
    mid = r1.astype(BF16)
    lo = (r1 - mid.astype(F32)).astype(BF16)
    return _dot(hi, ones_bf) + _dot(mid, ones_bf) + _dot(lo, ones_bf)


def _dot3(a, b):
    a_hi, a_lo = _split2(a)
    b_hi, b_lo = _split2(b)
    return _dot(a_hi, b_hi) + _dot(a_hi, b_lo) + _dot(a_lo, b_hi)


def _sigmoid(x):
    return 1.0 / (1.0 + jnp.exp(-x))


def _silu(x):
    return x * _sigmoid(x)


def _proj_kernel(x_ref, g_ref, w_ref, o_ref, xn_ref):
    @pl.when(pl.program_id(1) == 0)
    def _():
        x = x_ref[...]
        ms = jnp.mean(x * x, axis=-1, keepdims=True)
        xn_ref[...] = (x * lax.rsqrt(ms + EPS) * g_ref[...]).astype(BF16)

    o_ref[...] = _dot(xn_ref[...], w_ref[...])


def _proj(x2d, g, w_bf, tm, tn):
    m, d = x2d.shape
    n = w_bf.shape[1]
    return pl.pallas_call(
        _proj_kernel,
        grid=(m // tm, n // tn),
        in_specs=[pl.BlockSpec((tm, d), lambda i, j: (i, 0)),
                  pl.BlockSpec((1, d), lambda i, j: (0, 0)),
                  pl.BlockSpec((d, tn), lambda i, j: (0, j))],
        out_specs=pl.BlockSpec((tm, tn), lambda i, j: (i, j)),
        out_shape=jax.ShapeDtypeStruct((m, n), F32),
        scratch_shapes=[pltpu.VMEM((tm, d), BF16)],
        compiler_params=_cparams("parallel", "arbitrary"),
    )(x2d, g.reshape(1, d), w_bf)


def _permute_w_in(w):
    o = [int(v) for v in np.cumsum((0,) + SPLIT_SIZES)]
    za0, qb0, kb0, vb0, qi0, wi0, ki0, zb0, qc0, zc0, g0 = o[1:12]
    z = lambda n: jnp.zeros((w.shape[0], n), w.dtype)
    parts = [w[:, 0:3 * DA], w[:, za0:za0 + DA], w[:, qb0:qb0 + DB], w[:, kb0:kb0 + DB], w[:, vb0:vb0 + DB],
             w[:, zb0:zb0 + DB], w[:, qi0:qi0 + HI * DI], w[:, qc0:qc0 + DC], w[:, zc0:zc0 + DC],
             w[:, 3 * DA:SHIFT_W], w[:, ki0:ki0 + DI], z(LANES - DI), w[:, wi0:wi0 + HI], z(LANES - HI), z(LANES),
             w[:, g0:g0 + 3 * D_MODEL]]
    out = jnp.concatenate(parts, axis=1).astype(BF16)
    assert out.shape[1] == N_P
    return out


def _prep_kernel(carry_mode, period, rkv_ref, wdad_ref, pinit_ref, mu_ref, w0_ref, wup_ref, a0_ref, aup_ref,
                 kk_ref, ka_ref, ones_ref, r_o, w_o, k_o, v_o, kn_o, b_o, carry_ref):
    tm = rkv_ref.shape[0]
    row = lax.broadcasted_iota(I32, (tm, 1), 0)
    cur = jnp.concatenate([rkv_ref[...], wdad_ref[...]], axis=1)
    rolled = pltpu.roll(cur, 1, 0)
    if carry_mode:
        @pl.when(pl.program_id(0) == 0)
        def _():
            carry_ref[...] = pinit_ref[...]
        prev = jnp.where(row == 0, carry_ref[...], rolled)
        carry_ref[...] = cur[tm - 1:tm, :]
    else:
        prev = jnp.where((row & (period - 1)) == 0, pinit_ref[...], rolled)
    m = cur + mu_ref[...] * (prev - cur)
    r = m[:, 0:DA]
    k = m[:, DA:2 * DA]
    v = m[:, 2 * DA:3 * DA]
    wd = m[:, 3 * DA:3 * DA + LORA]
    ad = m[:, 3 * DA + LORA:SHIFT_W]
    lw = w0_ref[...] + _dot3(jnp.tanh(wd), wup_ref[...])
    z = -lw
    softplus = jnp.maximum(z, 0.0) + jnp.log(1.0 + jnp.exp(-jnp.abs(z)))
    logw = -softplus - 0.5
    decay = jnp.exp(-jnp.exp(logw))
    a = _sigmoid(a0_ref[...] + _dot3(ad, aup_ref[...]))
    kk = k * kk_ref[...]
    ss = _dot_ones(kk * kk, ones_ref[...])
    kn = kk * lax.rsqrt(ss + 1e-12)
    r_o[...] = r
    w_o[...] = decay
    k_o[...] = k * (1.0 + (a - 1.0) * ka_ref[...])
    v_o[...] = v
    kn_o[...] = kn
    b_o[...] = kn * a


def _rwkv_prep(p, pinit, rw, ones_da, tm, carry_mode, period=1):
    t = p.shape[0]
    mu, w0, w_up, a0, a_up, k_k, k_a = rw[:7]
    row = lambda v: v.reshape(1, -1)
    full = lambda shape: pl.BlockSpec(shape, lambda i: (0, 0))
    pin_spec = full((1, SHIFT_W)) if carry_mode else pl.BlockSpec((tm, SHIFT_W), lambda i: (i, 0))
    out = jax.ShapeDtypeStruct((t, DA), F32)
    return pl.pallas_call(
        functools.partial(_prep_kernel, carry_mode, period),
        grid=(t // tm,),
        in_specs=[pl.BlockSpec((tm, 3 * DA), lambda i: (i, 0)),
                  pl.BlockSpec((tm, LANES), lambda i: (i, C_WDAD // LANES)),
                  pin_spec, full((1, SHIFT_W)), full((1, DA)), full((LORA, DA)), full((1, DA)), full((LORA, DA)),
                  full((1, DA)), full((1, DA)), full((DA, DA))],
        out_specs=[pl.BlockSpec((tm, DA), lambda i: (i, 0))] * 6,
        out_shape=[out] * 6,
        scratch_shapes=[pltpu.VMEM((1, SHIFT_W), F32)],
        compiler_params=_cparams("arbitrary"),
    )(p, p, pinit, row(mu), row(w0), w_up, row(a0), a_up, row(k_k), row(k_a), ones_da)


def _seg_sum_bcast(x, ones2_ref):
    hi = x.astype(BF16)
    lo = (x - hi.astype(F32)).astype(BF16)
    return _dot(jnp.concatenate([hi, lo], axis=1), ones2_ref[...])


def _scan_kernel(n_tok, r_ref, w_ref, k_ref, v_ref, kn_ref, b_ref, s0_ref, rk_ref, g_ref, beta_ref,
                 ones2_ref, y_o, s_o, s_ref, y_ref):
    c = pl.program_id(1)

    @pl.when(c == 0)
    def _():
        for p in range(N_PAIR):
            s_ref[p] = jnp.concatenate([s0_ref[2 * p], s0_ref[2 * p + 1]], axis=1)

    lane = lax.broadcasted_iota(I32, (DHA, LANES), 1)
    sub = lax.broadcasted_iota(I32, (DHA, LANES), 0)
    diag = (lane & (DHA - 1)) == sub
    diag_bf = jnp.where(diag, 1.0, 0.0).astype(BF16)
    lo_lanes = lane < DHA
    lo_row = lax.broadcasted_iota(I32, (SUBLANES, LANES), 1) < DHA
    pairs = range(N_PAIR)

    def on_diag(row):
        packed = jnp.broadcast_to(row, (2 * SUBLANES, LANES)).astype(BF16)
        return jnp.concatenate([packed] * (DHA // (2 * SUBLANES)), axis=0) * diag_bf

    def group(base, n):
        rows8 = pl.ds(base, SUBLANES)
        tile = lambda ref, p: ref[rows8, p * LANES:(p + 1) * LANES]
        v_hi = [tile(v_ref, p).astype(BF16).astype(F32) for p in pairs]
        v_lo = [tile(v_ref, p) - v_hi[p] for p in pairs]
        kn_a = [jnp.where(lo_row, tile(kn_ref, p), 0.0) for p in pairs]
        kn_b = [jnp.where(lo_row, 0.0, tile(kn_ref, p)) for p in pairs]
        kn_next = [pltpu.roll(tile(kn_ref, p), SUBLANES - 1, 0) for p in pairs]
        c_b = [_seg_sum_bcast(tile(b_ref, p) * kn_next[p], ones2_ref) for p in pairs]
        c_k = [_seg_sum_bcast(tile(k_ref, p) * kn_next[p], ones2_ref) for p in pairs]
        y_rows = [[] for _ in pairs]
        half_sums = lambda x, a, b: jnp.where(lo_lanes, jnp.sum(x * a, axis=1, keepdims=True),
                                              jnp.sum(x * b, axis=1, keepdims=True))
        for u in range(0, n, 2):
            r0, r1 = slice(u, u + 1), slice(u + 1, u + 2)
            vb = [[_dot(jnp.concatenate([on_diag(v_hi[p][r]), on_diag(v_lo[p][r])], axis=1), ones2_ref[...])
                   for r in (r0, r1)] for p in pairs]
            sa0, sa1 = [], []
            for p in pairs:
                s = s_ref[p]
                w0 = tile(w_ref, p)[r0]
                sa0.append(half_sums(s, kn_a[p][r0], kn_b[p][r0]))
                sa1.append(half_sums(s, w0 * kn_a[p][r1], w0 * kn_b[p][r1]))
            yb = []
            for p in pairs:
                fix = sa1[p] - sa0[p] * c_b[p][r0] + vb[p][0] * c_k[p][r0]
                s_mid = (s_ref[p] * tile(w_ref, p)[r0] - sa0[p] * tile(b_ref, p)[r0]
                         + vb[p][0] * tile(k_ref, p)[r0])
                s_new = s_mid * tile(w_ref, p)[r1] - fix * tile(b_ref, p)[r1] + vb[p][1] * tile(k_ref, p)[r1]
                s_ref[p] = s_new
                yb.append((_dot((s_mid * tile(r_ref, p)[r0]).astype(BF16), ones2_ref[:LANES, :]),
                           _dot((s_new * tile(r_ref, p)[r1]).astype(BF16), ones2_ref[:LANES, :])))
            for p in pairs:
                for y in yb[p]:
                    y_rows[p].append(jnp.sum(jnp.where(diag, y, 0.0), axis=0, keepdims=True))
        for p in pairs:
            rows = y_rows[p] + [jnp.zeros((1, LANES), F32)] * (SUBLANES - n)
            y_ref[rows8, p * LANES:(p + 1) * LANES] = jnp.concatenate(rows, axis=0)

    n_full = n_tok // SUBLANES

    def full_group(g, carry):
        group(pl.multiple_of(g * SUBLANES, SUBLANES), SUBLANES)
        return carry

    lax.fori_loop(0, n_full, full_group, 0)
    if n_tok % SUBLANES:
        group(n_full * SUBLANES, n_tok % SUBLANES)

    ones = ones2_ref[:LANES, :]
    for p in pairs:
        sl = slice(p * LANES, (p + 1) * LANES)
        y = y_ref[:, sl]
        mean = _dot_ones(y, ones) * (1.0 / DHA)
        yc = y - mean
        var = _dot_ones(yc * yc, ones) * (1.0 / DHA)
        yn = yc * lax.rsqrt(var + LNX_EPS) * g_ref[:, sl] + beta_ref[:, sl]
        bonus = _dot_ones(r_ref[:, sl] * k_ref[:, sl] * rk_ref[:, sl], ones) * v_ref[:, sl]
        y_o[:, sl] = yn + bonus

    @pl.when(c == pl.num_programs(1) - 1)
    def _():
        for p in pairs:
            s_o[2 * p] = s_ref[p][:, :DHA]
            s_o[2 * p + 1] = s_ref[p][:, DHA:]


def _rwkv_scan(seqs, s0, rw, ones_da, tc, n_tok):
    bsz, t, _ = seqs[0].shape
    assert tc == -(-n_tok // SUBLANES) * SUBLANES
    assert n_tok % 2 == 0
    r_k, lnx_g, lnx_b = rw[7:10]
    row = lambda v: v.reshape(1, DA)
    seq_spec = pl.BlockSpec((None, tc, DA), lambda b, c: (b, c, 0))
    st_spec = pl.BlockSpec((None, HA, DHA, DHA), lambda b, c: (b, 0, 0, 0))
    full = lambda shape: pl.BlockSpec(shape, lambda b, c: (0, 0))
    return pl.pallas_call(
        functools.partial(_scan_kernel, n_tok),
        grid=(bsz, t // tc),
        in_specs=[seq_spec] * 6 + [st_spec, full((1, DA)), full((1, DA)), full((1, DA)), full((2 * LANES, LANES))],
        out_specs=[seq_spec, st_spec],
        out_shape=[jax.ShapeDtypeStruct((bsz, t, DA), F32), jax.ShapeDtypeStruct((bsz, HA, DHA, DHA), F32)],
        scratch_shapes=[pltpu.VMEM((N_PAIR, DHA, LANES), F32), pltpu.VMEM((tc, DA), F32)],
        compiler_params=_cparams("parallel", "arbitrary"),
    )(*seqs, s0, row(r_k), row(lnx_g), row(lnx_b), jnp.concatenate([ones_da[:LANES, :LANES]] * 2, axis=0))


def _rope_kernel(qb_ref, kb_ref, vb_ref, qi_ref, ki_ref, c128_ref, s128_ref, c64_ref, s64_ref,
                 q_o, k_o, v_o, kbf_o, vbf_o, qi_o, ki_o, kcat_o):
    c128, s128, c64, s64 = c128_ref[...], s128_ref[...], c64_ref[...], s64_ref[...]
    lane = lax.broadcasted_iota(I32, c64.shape, 1)
    first_half = (lane & (DI - 1)) < DI // 2

    def rope128(x):
        return x * c128 + pltpu.roll(x, DHB // 2, 1) * s128

    def rope64(x):
        partner = jnp.where(first_half, pltpu.roll(x, LANES - DI // 2, 1), pltpu.roll(x, DI // 2, 1))
        return x * c64 + partner * s64

    for h in range(HB):
        sl = slice(h * DHB, (h + 1) * DHB)
        q_o[:, sl] = rope128(qb_ref[:, sl]).astype(BF16)
        kr = rope128(kb_ref[:, sl])
        k_o[h] = kr
        v_o[h] = vb_ref[:, sl]
        kbf_o[:, sl] = kr.astype(BF16)
    vbf_o[...] = vb_ref[...].astype(BF16)
    for h2 in range(HI * DI // LANES):
        sl = slice(h2 * LANES, (h2 + 1) * LANES)
        qi_o[:, sl] = rope64(qi_ref[:, sl])
    ki = rope64(ki_ref[...])[:, :DI]
    ki_o[...] = ki
    kcat_o[...] = _index_key_cat(ki)


def _rope_tables(pos, d):
    inv = ROPE_THETA ** (-jnp.arange(0, d, 2, dtype=F32) / d)
    ang = pos.astype(F32)[:, None] * inv[None, :]
    cos, sin = jnp.cos(ang), jnp.sin(ang)
    reps = LANES // d
    c = jnp.tile(jnp.concatenate([cos, cos], axis=1), (1, reps))
    s = jnp.tile(jnp.concatenate([-sin, sin], axis=1), (1, reps))
    return c, s


def _rope(p, pos, tm):
    t = p.shape[0]
    c128, s128 = _rope_tables(pos, DHB)
    c64, s64 = _rope_tables(pos, DI)
    col = lambda w, c0: pl.BlockSpec((tm, w), lambda i: (i, c0 // w))
    rows = lambda w: pl.BlockSpec((tm, w), lambda i: (i, 0))
    sds = lambda w, dt: jax.ShapeDtypeStruct((t, w), dt)
    heads = pl.BlockSpec((HB, tm, DHB), lambda i: (0, i, 0))
    hsds = jax.ShapeDtypeStruct((HB, t, DHB), F32)
    return pl.pallas_call(
        _rope_kernel,
        grid=(t // tm,),
        in_specs=[col(DB, C_QB), col(DB, C_KB), col(DB, C_VB), col(HI * DI, C_QI), col(LANES, C_KI),
                  rows(LANES), rows(LANES), rows(LANES), rows(LANES)],
        out_specs=[rows(DB), heads, heads, rows(DB), rows(DB), rows(HI * DI), rows(DI), rows(4 * DI)],
        out_shape=[sds(DB, BF16), hsds, hsds, sds(DB, BF16), sds(DB, BF16), sds(HI * DI, F32), sds(DI, F32),
                   sds(4 * DI, BF16)],
        compiler_params=_cparams("parallel"),
    )(p, p, p, p, p, c128, s128, c64, s64)


def _canonical(score):
    return jnp.where(score == 0.0, 0.0, score)


def _ordinal_to_float(u):
    s = u ^ INT_MIN
    b = s ^ ((s >> 31) & 0x7FFFFFFF)
    return lax.bitcast_convert_type(b, F32)


def _index_query_cat(qi):
    qi_hi, qi_lo = _split2(qi)
    out = []
    for h in range(HI):
        sl = slice(h * DI, (h + 1) * DI)
        out.append(jnp.concatenate([qi_hi[:, sl], qi_hi[:, sl], qi_lo[:, sl], qi_lo[:, sl]], axis=1))
    return out


def _index_key_cat(ki):
    k_hi, k_lo = _split2(ki)
    return jnp.concatenate([k_hi, k_lo, k_hi, k_lo], axis=1)


def _count(sc_ref, nc, ck, pred):
    n_acc = 8

    def body(c, accs):
        accs = list(accs)
        base = pl.multiple_of(c * ck, ck)
        for j in range(ck // SUBLANES):
            row0 = base + j * SUBLANES
            accs[j % n_acc] = accs[j % n_acc] + jnp.where(pred(sc_ref[pl.ds(row0, SUBLANES), :], row0), 1, 0)
        return tuple(accs)

    accs = lax.fori_loop(0, nc, body, (jnp.zeros((SUBLANES, LANES), I32),) * n_acc)
    return jnp.sum(functools.reduce(lambda a, b: a + b, accs), axis=0, keepdims=True)


def _select_bias(sc_ref, bias_ref, cut_ref, nc, nc_total, ck, topk, idx_bits):
    sub8 = lax.broadcasted_iota(I32, (SUBLANES, LANES), 0)
    sub128 = lax.broadcasted_iota(I32, (LANES, LANES), 0)

    bits_per_check = 4

    def bit_group(g, carry):
        def refine(carry):
            ans_u, done = carry
            for b in range(bits_per_check):
                cand_u = ans_u | lax.shift_left(jnp.int32(1), 31 - (g * bits_per_check + b))
                cand = _ordinal_to_float(cand_u)
                cnt = _count(sc_ref, nc, ck, lambda blk, _: blk >= cand)
                take = (cnt >= topk) & (done == 0)
                ans_u, done = jnp.where(take, cand_u, ans_u), jnp.where(take & (cnt == topk), 1, done)
            return ans_u, done

        return lax.cond(jnp.min(carry[1]) == 0, refine, lambda carry: carry, carry)

    zeros = jnp.zeros((1, LANES), I32)
    ans_u = lax.fori_loop(0, 32 // bits_per_check, bit_group, (zeros, zeros))[0]
    ans_u = jnp.where((ans_u >= 0) & (ans_u < ORD_NEG_INF), ORD_NEG_INF, ans_u)
    thr = _ordinal_to_float(ans_u)
    n_gt = _count(sc_ref, nc, ck, lambda blk, _: blk > thr)
    n_ge = _count(sc_ref, nc, ck, lambda blk, _: blk >= thr)
    need = topk - n_gt
    tied = ((n_ge - n_gt) > need) & (thr > -jnp.inf)
    cut_ref[...] = jnp.full((1, LANES), 2 ** 30, I32)

    @pl.when(jnp.max(tied.astype(I32)) > 0)
    def _():
        def tie_body(it, ans):
            cand = ans | lax.shift_left(jnp.int32(1), idx_bits - 1 - it)
            g = _count(sc_ref, nc, ck, lambda blk, row0: (blk == thr) & ((sub8 + row0) < cand))
            return jnp.where(g < need, cand, ans)

        cut = lax.fori_loop(0, idx_bits, tie_body, jnp.zeros((1, LANES), I32))
        cut_ref[...] = jnp.where(tied, cut, 2 ** 30)

    cut = cut_ref[...]

    def write(c, carry):
        base = pl.multiple_of(c * ck, ck)
        for u in range(ck // LANES):
            off = base + u * LANES
            blk = sc_ref[pl.ds(off, LANES), :]
            sel = ((blk > thr) | ((blk == thr) & ((sub128 + off) <= cut))) & (blk > -jnp.inf)
            bias_ref[:, pl.ds(off, LANES)] = jnp.where(sel, 0.0, NEG_BIAS).T.astype(bias_ref.dtype)
        return carry

    lax.fori_loop(0, nc, write, 0)

    def fill(c, carry):
        base = pl.multiple_of(c * ck, ck)
        bias_ref[:, pl.ds(base, ck)] = jnp.full((bias_ref.shape[0], ck), NEG_BIAS, bias_ref.dtype)
        return carry

    lax.fori_loop(nc, nc_total, fill, 0)


def _prompt_select_kernel(topk, ck, idx_bits, qi_ref, wi_ref, kcat_ref, bias_o, sc_ref, cut_ref):
    i = pl.program_id(0)
    tq = qi_ref.shape[0]
    t = kcat_ref.shape[0]
    nc = ((i + 1) * tq + ck - 1) // ck
    q_cat = _index_query_cat(qi_ref[...])
    q_pair = [jnp.concatenate(q_cat[2 * j:2 * j + 2], axis=0) for j in range(HI // 2)]
    wi_t = wi_ref[...].T
    kpos = lax.broadcasted_iota(I32, (ck, tq), 0)
    qpos = i * tq + lax.broadcasted_iota(I32, (ck, tq), 1)
    scale = HI ** -0.5 * DI ** -0.5

    def score_chunk(c, carry):
        base = pl.multiple_of(c * ck, ck)
        k_cat = kcat_ref[pl.ds(base, ck), :]
        acc = None
        for h2 in range(HI // 2):
            s2 = _dot_nt(k_cat, q_pair[h2])
            for e in range(2):
                h = 2 * h2 + e
                term = wi_t[h:h + 1, :] * jnp.maximum(s2[:, e * tq:(e + 1) * tq], 0.0)
                acc = term if acc is None else acc + term
        sc = jnp.where(kpos + base <= qpos, acc * scale, -jnp.inf)
        sc_ref[pl.ds(base, ck), :] = _canonical(sc)
        return carry

    lax.fori_loop(0, nc, score_chunk, 0)
    _select_bias(sc_ref, bias_o, cut_ref, nc, t // ck, ck, topk, idx_bits)


def _prompt_select(qi, p, k_cat, ck):
    t = qi.shape[0]
    tq = LANES
    topk = min(TOPK_MAX, t // 4)
    idx_bits = int(np.ceil(np.log2(t))) + 1
    return pl.pallas_call(
        functools.partial(_prompt_select_kernel, topk, ck, idx_bits),
        grid=(t // tq,),
        in_specs=[pl.BlockSpec((tq, HI * DI), lambda i: (i, 0)),
                  pl.BlockSpec((tq, LANES), lambda i: (i, C_WI // LANES)),
                  pl.BlockSpec((t, 4 * DI), lambda i: (0, 0))],
        out_specs=pl.BlockSpec((tq, t), lambda i: (i, 0)),
        out_shape=jax.ShapeDtypeStruct((t, t), BF16),
        scratch_shapes=[pltpu.VMEM((t, tq), F32), pltpu.VMEM((1, LANES), I32)],
        compiler_params=_cparams("parallel"),
    )(qi, p, k_cat)


def _sample_score_kernel(pps, rq, n_new, pt_ref, qit_ref, wif_ref, *refs):
    page_refs, knew_ref, skey_o, skey_new_o = refs[:pps], refs[pps], refs[pps + 1], refs[pps + 2]
    j = pl.program_id(1)
    q_hi, q_lo = _split2(qit_ref[...])
    wif = wif_ref[...]

    def scores(kt):
        k_hi, k_lo = _split2(kt)
        at = _dot(q_hi, k_hi) + _dot(q_hi, k_lo) + _dot(q_lo, k_hi)
        at = jnp.maximum(at, 0.0) * wif
        sc = at[0:rq]
        for h in range(1, HI):
            sc = sc + at[h * rq:(h + 1) * rq]
        return sc * (HI ** -0.5 * DI ** -0.5)

    sc = scores(jnp.concatenate([r[...] for r in page_refs], axis=1))
    trow = lax.broadcasted_iota(I32, sc.shape, 0)
    skey_o[...] = _canonical(jnp.where(trow < n_new, sc, -jnp.inf))

    @pl.when(j == pl.num_programs(1) - 1)
    def _():
        scn = scores(knew_ref[...])
        tr = lax.broadcasted_iota(I32, scn.shape, 0)
        ln = lax.broadcasted_iota(I32, scn.shape, 1)
        skey_new_o[...] = _canonical(jnp.where((tr < n_new) & (ln <= tr), scn, -jnp.inf))


def _sample_scores(qit, wif, cache_kidx, knew_pad, page_table, n_new, rq, pps):
    bsz, n_pages = page_table.shape
    per_b = lambda b, j, pt: (b, 0, 0)
    page_spec = lambda u: pl.BlockSpec((None, DI, PAGE), lambda b, j, pt: (pt[b, j * pps + u], 0, 0))
    grid_spec = pltpu.PrefetchScalarGridSpec(
        num_scalar_prefetch=1,
        grid=(bsz, n_pages // pps),
        in_specs=[pl.BlockSpec((None, LANES, DI), per_b), pl.BlockSpec((None, LANES, 1), per_b)]
                 + [page_spec(u) for u in range(pps)] + [pl.BlockSpec((None, DI, PAGE), per_b)],
        out_specs=[pl.BlockSpec((None, rq, pps * PAGE), lambda b, j, pt: (b, 0, j)),
                   pl.BlockSpec((None, rq, PAGE), per_b)],
    )
    return pl.pallas_call(
        functools.partial(_sample_score_kernel, pps, rq, n_new),
        grid_spec=grid_spec,
        out_shape=[jax.ShapeDtypeStruct((bsz, rq, n_pages * PAGE), F32), jax.ShapeDtypeStruct((bsz, rq, PAGE), F32)],
        compiler_params=_cparams("parallel", "arbitrary"),
    )(page_table, qit, wif, *([cache_kidx] * pps), knew_pad)


def _sample_select_kernel(topk, ck, idx_bits, sc_past, sc_new, bias_o, sc_ref, cut_ref):
    past = sc_past.shape[1]

    def flip(c, carry):
        off = pl.multiple_of(c * LANES, LANES)
        sc_ref[pl.ds(off, LANES), :] = sc_past[:, pl.ds(off, LANES)].T
        return carry

    lax.fori_loop(0, past // LANES, flip, 0)
    sc_ref[past:, :] = sc_new[...].T
    nc = sc_ref.shape[0] // ck
    _select_bias(sc_ref, bias_o, cut_ref, nc, nc, ck, topk, idx_bits)


def _sample_select(skey_past, skey_new, topk):
    rows, past = skey_past.shape
    n = past + skey_new.shape[1]
    tr = LANES
    idx_bits = int(np.ceil(np.log2(n))) + 1
    return pl.pallas_call(
        functools.partial(_sample_select_kernel, topk, PAGE, idx_bits),
        grid=(rows // tr,),
        in_specs=[pl.BlockSpec((tr, past), lambda i: (i, 0)), pl.BlockSpec((tr, n - past), lambda i: (i, 0))],
        out_specs=pl.BlockSpec((tr, n), lambda i: (i, 0)),
        out_shape=jax.ShapeDtypeStruct((rows, n), F32),
        scratch_shapes=[pltpu.VMEM((n, tr), F32), pltpu.VMEM((1, LANES), I32)],
        compiler_params=_cparams("parallel"),
    )(skey_past, skey_new)


def _prompt_attn_kernel(ck, q_ref, k_ref, v_ref, bias_ref, o_ref, m_ref, l_ref, acc_ref):
    i = pl.program_id(0)
    tq = q_ref.shape[0]
    nc = ((i + 1) * tq + ck - 1) // ck
    m_ref[...] = jnp.full(m_ref.shape, NEG_BIAS, F32)
    l_ref[...] = jnp.zeros(l_ref.shape, F32)
    acc_ref[...] = jnp.zeros(acc_ref.shape, F32)
    scale = DHB ** -0.5

    def chunk(c, carry):
        base = pl.multiple_of(c * ck, ck)
        bias = bias_ref[:, pl.ds(base, ck)].astype(F32)
        for h in range(HB):
            sl = slice(h * DHB, (h + 1) * DHB)
            s = _dot_nt(q_ref[:, sl], k_ref[pl.ds(base, ck), sl]) * scale + bias
            m_old = m_ref[h]
            m_new = jnp.maximum(m_old, jnp.max(s, axis=1, keepdims=True))
            alpha = jnp.exp(m_old - m_new)
            p = jnp.exp(s - m_new)
            l_ref[h] = alpha * l_ref[h] + jnp.sum(p, axis=1, keepdims=True)
            acc_ref[h] = alpha * acc_ref[h] + _dot(p.astype(BF16), v_ref[pl.ds(base, ck), sl])
            m_ref[h] = m_new
        return carry

    lax.fori_loop(0, nc, chunk, 0)
    for h in range(HB):
        o_ref[:, h * DHB:(h + 1) * DHB] = acc_ref[h] / l_ref[h]


def _prompt_attn(q_bf, k_bf, v_bf, bias, tq, ck):
    t = q_bf.shape[0]
    resident = lambda: pl.BlockSpec((t, DB), lambda i: (0, 0), pipeline_mode=pl.Buffered(1))
    return pl.pallas_call(
        functools.partial(_prompt_attn_kernel, ck),
        grid=(t // tq,),
        in_specs=[pl.BlockSpec((tq, DB), lambda i: (i, 0)), resident(), resident(),
                  pl.BlockSpec((tq, t), lambda i: (i, 0))],
        out_specs=pl.BlockSpec((tq, DB), lambda i: (i, 0)),
        out_shape=jax.ShapeDtypeStruct((t, DB), F32),
        scratch_shapes=[pltpu.VMEM((HB, tq, 1), F32), pltpu.VMEM((HB, tq, 1), F32), pltpu.VMEM((HB, tq, DHB), F32)],
        compiler_params=_cparams("parallel"),
    )(q_bf, k_bf, v_bf, bias)


def _sample_attn_kernel(pps, rq, pt_ref, qbd_ref, *refs):
    kpages, vpages = refs[:pps], refs[pps:2 * pps]
    knew_ref, vnew_ref, bias_ref, bias_new_ref, o_ref, m_ref, l_ref, acc_ref = refs[2 * pps:]
    j = pl.program_id(1)
    nr = HB * rq

    @pl.when(j == 0)
    def _():
        m_ref[...] = jnp.full(m_ref.shape, NEG_BIAS, F32)
        l_ref[...] = jnp.zeros(l_ref.shape, F32)
        acc_ref[...] = jnp.zeros(acc_ref.shape, F32)

    def accumulate(kp, vp, bias):
        st = _dot(kp, qbd_ref[...])
        s = st.T[:nr] * (DHB ** -0.5) + jnp.concatenate([bias] * HB, axis=0)
        m_old = m_ref[...]
        m_new = jnp.maximum(m_old, jnp.max(s, axis=1, keepdims=True))
        alpha = jnp.exp(m_old - m_new)
        p = jnp.exp(s - m_new)
        l_ref[...] = alpha * l_ref[...] + jnp.sum(p, axis=1, keepdims=True)
        acc_ref[...] = alpha * acc_ref[...] + _dot(p.astype(BF16), vp)
        m_ref[...] = m_new

    gather = lambda pages: jnp.concatenate(
        [jnp.concatenate([r[h].astype(BF16) for h in range(HB)], axis=1) for r in pages], axis=0)
    accumulate(gather(kpages), gather(vpages), bias_ref[...])

    @pl.when(j == pl.num_programs(1) - 1)
    def _():
        accumulate(knew_ref[...].astype(BF16), vnew_ref[...].astype(BF16), bias_new_ref[...])
        for h in range(HB):
            rs = slice(h * rq, (h + 1) * rq)
            cs = slice(h * DHB, (h + 1) * DHB)
            o_ref[:, cs] = acc_ref[rs, cs] / l_ref[rs, :]


def _sample_attn(qbd, cache_k, cache_v, knew_pad, vnew_pad, bias, page_table, rq, pps):
    bsz, n_pages = page_table.shape
    page_spec = lambda u: pl.BlockSpec((None, HB, PAGE, DHB), lambda b, j, pt: (pt[b, j * pps + u], 0, 0, 0))
    per_b = lambda b, j, pt: (b, 0, 0)
    grid_spec = pltpu.PrefetchScalarGridSpec(
        num_scalar_prefetch=1,
        grid=(bsz, n_pages // pps),
        in_specs=[pl.BlockSpec((None, DB, LANES), per_b)]
                 + [page_spec(u) for u in range(pps)] * 2
                 + [pl.BlockSpec((None, PAGE, DB), per_b), pl.BlockSpec((None, PAGE, DB), per_b),
                    pl.BlockSpec((None, rq, pps * PAGE), lambda b, j, pt: (b, 0, j)),
                    pl.BlockSpec((None, rq, PAGE), lambda b, j, pt: (b, 0, n_pages))],
        out_specs=pl.BlockSpec((None, rq, DB), per_b),
        scratch_shapes=[pltpu.VMEM((HB * rq, 1), F32), pltpu.VMEM((HB * rq, 1), F32), pltpu.VMEM((HB * rq, DB), F32)],
    )
    return pl.pallas_call(
        functools.partial(_sample_attn_kernel, pps, rq),
        grid_spec=grid_spec,
        out_shape=jax.ShapeDtypeStruct((bsz, rq, DB), F32),
        compiler_params=_cparams("parallel", "arbitrary"),
    )(page_table, qbd, *([cache_k] * pps), *([cache_v] * pps), knew_pad, vnew_pad, bias, bias)


def _cross_kernel(q_ref, mk_ref, mv_ref, o_ref):
    scale = DHC ** -0.5
    for h in range(HC):
        sl = slice(h * DHC, (h + 1) * DHC)
        s = _dot_nt(q_ref[:, sl].astype(BF16), mk_ref[h].astype(BF16)) * scale
        m = jnp.max(s, axis=1, keepdims=True)
        p = jnp.exp(s - m)
        l = jnp.sum(p, axis=1, keepdims=True)
        o_ref[:, sl] = _dot(p.astype(BF16), mv_ref[h].astype(BF16)) / l


def _cross_attn(q_arr, q_col, mk, mv, tm):
    bsz, t, _ = q_arr.shape
    return pl.pallas_call(
        _cross_kernel,
        grid=(bsz, t // tm),
        in_specs=[pl.BlockSpec((None, tm, DC), lambda b, i: (b, i, q_col // DC)),
                  pl.BlockSpec((None, HC, N_MEM, DHC), lambda b, i: (b, 0, 0, 0)),
                  pl.BlockSpec((None, HC, N_MEM, DHC), lambda b, i: (b, 0, 0, 0))],
        out_specs=pl.BlockSpec((None, tm, DC), lambda b, i: (b, i, 0)),
        out_shape=jax.ShapeDtypeStruct((bsz, t, DC), F32),
        compiler_params=_cparams("parallel", "parallel"),
    )(q_arr, mk, mv)


def _merge_kernel(x_ref, ya_ref, za_ref, yb_ref, zb_ref, yc_ref, zc_ref, ga_ref, gb_ref, gc_ref,
                  bga_ref, bgb_ref, bgc_ref, wa_ref, wb_ref, wc_ref, wo_ref, fg_ref, o_ref):
    ba = _dot((ya_ref[...] * _silu(za_ref[...])).astype(BF16), wa_ref[...])
    bb = _dot((yb_ref[...] * _silu(zb_ref[...])).astype(BF16), wb_ref[...])
    bc = _dot((yc_ref[...] * _silu(zc_ref[...])).astype(BF16), wc_ref[...])
    mix = (_sigmoid(ga_ref[...] + bga_ref[...]) * ba + _sigmoid(gb_ref[...] + bgb_ref[...]) * bb
           + _sigmoid(gc_ref[...] + bgc_ref[...]) * bc)
    y = x_ref[...] + _dot(mix.astype(BF16), wo_ref[...])
    ms = jnp.mean(y * y, axis=-1, keepdims=True)
    o_ref[...] = y * lax.rsqrt(ms + EPS) * fg_ref[...]


def _merge(x2d, p, ya, yb, yc, b_gate, wa, wb, wc, wo, fg, tm):
    t, d = x2d.shape
    rows = lambda w: pl.BlockSpec((tm, w), lambda i: (i, 0))
    col = lambda w, c0: pl.BlockSpec((tm, w), lambda i: (i, c0 // w))
    const = lambda shape, c=0: pl.BlockSpec(shape, lambda i: (0, c), pipeline_mode=pl.Buffered(1))
    bg = b_gate.reshape(1, 3 * d)
    return pl.pallas_call(
        _merge_kernel,
        grid=(t // tm,),
        in_specs=[rows(d), rows(DA), col(DA, C_ZA), rows(DB), col(DB, C_ZB), rows(DC), col(DC, C_ZC),
                  col(d, C_GATES), col(d, C_GATES + d), col(d, C_GATES + 2 * d),
                  const((1, d), 0), const((1, d), 1), const((1, d), 2),
                  const((DA, d)), const((DB, d)), const((DC, d)), const((d, d)), const((1, d))],
        out_specs=rows(d),
        out_shape=jax.ShapeDtypeStruct((t, d), F32),
        compiler_params=_cparams("parallel"),
    )(x2d, ya, p, yb, p, yc, p, p, p, p, bg, bg, bg, wa, wb, wc, wo, fg.reshape(1, d))


def _block_ones(n, seg):
    idx = np.arange(n) // seg
    return jnp.asarray(idx[:, None] == idx[None, :], dtype=BF16)


def _shift_row(p_row):
    return jnp.concatenate([p_row[:3 * DA], p_row[C_WDAD:C_WDAD + 2 * LORA]])


def kernel(x_prompt, x_sample, mem_prompt, cache_k, cache_v, cache_kidx, cache_mem_k, cache_mem_v, state_wkv, state_shift, page_table, norm_g, w_in, rwkv_mu, rwkv_w0, rwkv_w_up, rwkv_a0, rwkv_a_up, rwkv_k_k, rwkv_k_a, rwkv_r_k, rwkv_lnx_g, rwkv_lnx_b, mem_norm_g, w_mem_kv, b_gate, w_br_a, w_br_b, w_br_c, w_out, final_norm_g):
    depth = w_in.shape[0]
    assert depth == 1 and x_prompt.shape[0] == 1
    t = x_prompt.shape[1]
    bs, ts, d = x_sample.shape
    n_pages = page_table.shape[1]
    past = n_pages * PAGE
    rq = 8
    assert ts <= rq and (ts & (ts - 1)) == 0

    rw = (rwkv_mu[0], rwkv_w0[0], rwkv_w_up[0], rwkv_a0[0], rwkv_a_up[0], rwkv_k_k[0], rwkv_k_a[0],
          rwkv_r_k[0].reshape(DA), rwkv_lnx_g[0], rwkv_lnx_b[0])
    ones_da = _block_ones(DA, DHA)
    w_p = _permute_w_in(w_in[0])
    wa, wb, wc, wo = (w[0].astype(BF16) for w in (w_br_a, w_br_b, w_br_c, w_out))

    xp2 = x_prompt.reshape(t, d)
    xs2 = x_sample.reshape(bs * ts, d)
    tm_p = min(1024, t)
    p_p = _proj(xp2, norm_g[0], w_p, tm_p, 1024)
    p_s = _proj(xs2, norm_g[0], w_p, bs * ts, 1024)
    mkv = _proj(mem_prompt.reshape(N_MEM, d), mem_norm_g[0], w_mem_kv[0].astype(BF16), N_MEM, 2 * DC)
    mk_p = mkv[:, :DC].reshape(1, N_MEM, HC, DHC)
    mv_p = mkv[:, DC:].reshape(1, N_MEM, HC, DHC)
    head_major = lambda a: jnp.transpose(a, (0, 2, 1, 3))

    seq_p = _rwkv_prep(p_p, jnp.zeros((1, SHIFT_W), F32), rw, ones_da, min(256, t), True)
    ya_p, wkv_p = _rwkv_scan([a[None] for a in seq_p], jnp.zeros((1, HA, DHA, DHA), F32), rw, ones_da,
                             min(256, t), min(256, t))
    pinit_s = jnp.repeat(state_shift[0], ts, axis=0)
    seq_s = _rwkv_prep(p_s, pinit_s, rw, ones_da, bs * ts, False, ts)
    pad_t = lambda a: jnp.pad(a.reshape(bs, ts, -1), ((0, 0), (0, rq - ts), (0, 0)))
    ya_s, wkv_s = _rwkv_scan([pad_t(a) for a in seq_s], state_wkv[0], rw, ones_da, rq, ts)
    ya_s = ya_s[:, :ts].reshape(bs * ts, DA)

    tokens_major = lambda a: jnp.transpose(a, (1, 0, 2))
    q_p, k_p, v_p, kbf_p, vbf_p, qi_p, ki_p, kcat_p = _rope(p_p, jnp.arange(t), min(512, t))
    pos_s = jnp.tile(past + jnp.arange(ts), bs)
    q_s, k_s, v_s, _, _, qi_s, ki_s, _ = _rope(p_s, pos_s, bs * ts)
    k_s, v_s = tokens_major(k_s), tokens_major(v_s)

    bias_p = _prompt_select(qi_p, p_p, kcat_p, 512)
    yb_p = _prompt_attn(q_p, kbf_p, vbf_p, bias_p, min(256, t), 1024)

    topk_s = min(TOPK_MAX, (past + ts) // 4)
    pad_rows = lambda a, n: jnp.pad(a, ((0, 0), (0, n - a.shape[1]), (0, 0)))
    pps = 8 if n_pages % 8 == 0 else 1
    qit = pad_rows(qi_s.reshape(bs, ts, HI * DI), rq).reshape(bs, rq, HI, DI).transpose(0, 2, 1, 3)
    qit = pad_rows(qit.reshape(bs, HI * rq, DI), LANES)
    wif = pad_rows(p_s[:, C_WI:C_WI + HI].reshape(bs, ts, HI), rq).transpose(0, 2, 1).reshape(bs, HI * rq, 1)
    wif = pad_rows(wif, LANES)
    kinew = jnp.swapaxes(pad_rows(ki_s.reshape(bs, ts, DI), PAGE), 1, 2)
    skey_past, skey_new = _sample_scores(qit, wif, jnp.swapaxes(cache_kidx[0], 1, 2), kinew, page_table, ts,
                                         rq, pps)
    n_rows = -(-bs * rq // LANES) * LANES
    lane_rows = lambda a: jnp.pad(a.reshape(bs * rq, -1), ((0, n_rows - bs * rq), (0, 0)), constant_values=-jnp.inf)
    bias_s = _sample_select(lane_rows(skey_past), lane_rows(skey_new), topk_s)
    bias_s = bias_s[:bs * rq].reshape(bs, rq, past + PAGE)
    knew = pad_rows(k_s.reshape(bs, ts, DB), PAGE)
    vnew = pad_rows(v_s.reshape(bs, ts, DB), PAGE)
    qh = pad_rows(q_s.reshape(bs, ts, DB), rq).reshape(bs, rq, HB, DHB).transpose(0, 2, 3, 1)
    head_eq = jnp.eye(HB, dtype=bool)[None, :, None, :, None]
    qbd = jnp.where(head_eq, qh[:, :, :, None, :], jnp.zeros((), BF16)).reshape(bs, DB, HB * rq)
    qbd = jnp.pad(qbd, ((0, 0), (0, 0), (0, LANES - HB * rq)))
    yb_s = _sample_attn(qbd, head_major(cache_k[0]), head_major(cache_v[0]), knew, vnew, bias_s, page_table,
                        rq, pps)
    yb_s = yb_s[:, :ts].reshape(bs * ts, DB)

    yc_p = _cross_attn(p_p[None], C_QC, head_major(mk_p), head_major(mv_p), min(512, t))[0]
    qc8 = pad_rows(p_s[:, C_QC:C_QC + DC].reshape(bs, ts, DC), rq)
    yc_s = _cross_attn(qc8, 0, head_major(cache_mem_k[0]), head_major(cache_mem_v[0]), rq)
    yc_s = yc_s[:, :ts].reshape(bs * ts, DC)

    y_p = _merge(xp2, p_p, ya_p[0], yb_p, yc_p, b_gate[0], wa, wb, wc, wo, final_norm_g, min(256, t))
    y_s = _merge(xs2, p_s, ya_s, yb_s, yc_s, b_gate[0], wa, wb, wc, wo, final_norm_g, bs * ts)

    p_s3 = p_s.reshape(bs, ts, N_P)
    shift_s = jnp.concatenate([p_s3[:, -1, :3 * DA], p_s3[:, -1, C_WDAD:C_WDAD + 2 * LORA]], axis=-1)
    return (y_p.reshape(1, t, d), y_s.reshape(bs, ts, d),
            tokens_major(k_p).reshape(1, 1, t, HB, DHB), tokens_major(v_p).reshape(1, 1, t, HB, DHB),
            ki_p.reshape(1, 1, t, DI),
            mk_p.reshape(1, 1, N_MEM, HC, DHC), mv_p.reshape(1, 1, N_MEM, HC, DHC),
            wkv_p.reshape(1, 1, HA, DHA, DHA), _shift_row(p_p[t - 1]).reshape(1, 1, SHIFT_W),
            k_s.reshape(1, bs, ts, HB, DHB), v_s.reshape(1, bs, ts, HB, DHB), ki_s.reshape(1, bs, ts, DI),
            wkv_s.reshape(1, bs, HA, DHA, DHA), shift_s.reshape(1, bs, SHIFT_W))
```

```python
import functools

import numpy as np
import jax
import jax.numpy as jnp
from jax import lax
from jax.experimental import pallas as pl
from jax.experimental.pallas import tpu as pltpu

F32, BF16, I32 = jnp.float32, jnp.bfloat16, jnp.int32

D_MODEL = 2048
HA, DHA = 12, 64
DA = HA * DHA
LORA = 64
LNX_EPS = 64e-5
HB, DHB = 6, 128
DB = HB * DHB
HI, DI = 8, 64
TOPK_MAX = 256
N_MEM = 256
HC, DHC = 4, 128
DC = HC * DHC
ROPE_THETA = 10000.0
EPS = 1e-6
PAGE = 128
SHIFT_W = 3 * DA + 2 * LORA
SPLIT_SIZES = (SHIFT_W, DA, DB, DB, DB, HI * DI, HI, DI, DB, DC, DC, 3 * D_MODEL)

LANES = 128
SUBLANES = 8
N_PAIR = HA // 2

C_RKV, C_ZA, C_QB, C_KB, C_VB, C_ZB = 0, 2304, 3072, 3840, 4608, 5376
C_QI, C_QC, C_ZC, C_WDAD, C_KI, C_WI, C_GATES = 6144, 6656, 7168, 7680, 7808, 7936, 8192
N_P = C_GATES + 3 * D_MODEL

NEG_BIAS = -1e30
ORD_NEG_INF = 0x007FFFFF
INT_MIN = -2147483648
VMEM_LIMIT = 56 * 1024 * 1024


def _cparams(*sem):
    return pltpu.CompilerParams(dimension_semantics=sem, vmem_limit_bytes=VMEM_LIMIT)


def _dot(a, b):
    return jnp.dot(a, b, preferred_element_type=F32)


def _dot_nt(a, b):
    return lax.dot_general(a, b, (((1,), (1,)), ((), ())), preferred_element_type=F32)


def _split2(x):
    hi = x.astype(BF16)
    lo = (x - hi.astype(F32)).astype(BF16)
    return hi, lo


def _dot_ones(x, ones_bf):
    hi = x.astype(BF16)
    r1 = x - hi.astype(F32)
    mid = r1.astype(BF16)
    lo = (r1 - mid.astype(F32)).astype(BF16)
    return _dot(hi, ones_bf) + _dot(mid, ones_bf) + _dot(lo, ones_bf)


def _dot3(a, b):
    a_hi, a_lo = _split2(a)
    b_hi, b_lo = _split2(b)
    return _dot(a_hi, b_hi) + _dot(a_hi, b_lo) + _dot(a_lo, b_hi)


def _sigmoid(x):
    return 1.0 / (1.0 + jnp.exp(-x))


def _silu(x):
    return x * _sigmoid(x)


def _proj_kernel(x_ref, g_ref, w_ref, o_ref, xn_ref):
    @pl.when(pl.program_id(1) == 0)
    def _():
        x = x_ref[...]
        ms = jnp.mean(x * x, axis=-1, keepdims=True)
        xn_ref[...] = (x * lax.rsqrt(ms + EPS) * g_ref[...]).astype(BF16)

    o_ref[...] = _dot(xn_ref[...], w_ref[...])


def _proj(x2d, g, w_bf, tm, tn):
    m, d = x2d.shape
    n = w_bf.shape[1]
    return pl.pallas_call(
        _proj_kernel,
        grid=(m // tm, n // tn),
        in_specs=[pl.BlockSpec((tm, d), lambda i, j: (i, 0)),
                  pl.BlockSpec((1, d), lambda i, j: (0, 0)),
                  pl.BlockSpec((d, tn), lambda i, j: (0, j))],
        out_specs=pl.BlockSpec((tm, tn), lambda i, j: (i, j)),
        out_shape=jax.ShapeDtypeStruct((m, n), F32),
        scratch_shapes=[pltpu.VMEM((tm, d), BF16)],
        compiler_params=_cparams("parallel", "arbitrary"),
    )(x2d, g.reshape(1, d), w_bf)


def _permute_w_in(w):
    o = [int(v) for v in np.cumsum((0,) + SPLIT_SIZES)]
    za0, qb0, kb0, vb0, qi0, wi0, ki0, zb0, qc0, zc0, g0 = o[1:12]
    z = lambda n: jnp.zeros((w.shape[0], n), w.dtype)
    parts = [w[:, 0:3 * DA], w[:, za0:za0 + DA], w[:, qb0:qb0 + DB], w[:, kb0:kb0 + DB], w[:, vb0:vb0 + DB],
             w[:, zb0:zb0 + DB], w[:, qi0:qi0 + HI * DI], w[:, qc0:qc0 + DC], w[:, zc0:zc0 + DC],
             w[:, 3 * DA:SHIFT_W], w[:, ki0:ki0 + DI], z(LANES - DI), w[:, wi0:wi0 + HI], z(LANES - HI), z(LANES),
             w[:, g0:g0 + 3 * D_MODEL]]
    out = jnp.concatenate(parts, axis=1).astype(BF16)
    assert out.shape[1] == N_P
    return out


def _prep_kernel(carry_mode, period, rkv_ref, wdad_ref, pinit_ref, mu_ref, w0_ref, wup_ref, a0_ref, aup_ref,
                 kk_ref, ka_ref, ones_ref, r_o, w_o, k_o, v_o, kn_o, b_o, carry_ref):
    tm = rkv_ref.shape[0]
    row = lax.broadcasted_iota(I32, (tm, 1), 0)
    cur = jnp.concatenate([rkv_ref[...], wdad_ref[...]], axis=1)
    rolled = pltpu.roll(cur, 1, 0)
    if carry_mode:
        @pl.when(pl.program_id(0) == 0)
        def _():
            carry_ref[...] = pinit_ref[...]
        prev = jnp.where(row == 0, carry_ref[...], rolled)
        carry_ref[...] = cur[tm - 1:tm, :]
    else:
        prev = jnp.where((row & (period - 1)) == 0, pinit_ref[...], rolled)
    m = cur + mu_ref[...] * (prev - cur)
    r = m[:, 0:DA]
    k = m[:, DA:2 * DA]
    v = m[:, 2 * DA:3 * DA]
    wd = m[:, 3 * DA:3 * DA + LORA]
    ad = m[:, 3 * DA + LORA:SHIFT_W]
    lw = w0_ref[...] + _dot3(jnp.tanh(wd), wup_ref[...])
    z = -lw
    softplus = jnp.maximum(z, 0.0) + jnp.log(1.0 + jnp.exp(-jnp.abs(z)))
    logw = -softplus - 0.5
    decay = jnp.exp(-jnp.exp(logw))
    a = _sigmoid(a0_ref[...] + _dot3(ad, aup_ref[...]))
    kk = k * kk_ref[...]
    ss = _dot_ones(kk * kk, ones_ref[...])
    kn = kk * lax.rsqrt(ss + 1e-12)
    r_o[...] = r
    w_o[...] = decay
    k_o[...] = k * (1.0 + (a - 1.0) * ka_ref[...])
    v_o[...] = v
    kn_o[...] = kn
    b_o[...] = kn * a


def _rwkv_prep(p, pinit, rw, ones_da, tm, carry_mode, period=1):
    t = p.shape[0]
    mu, w0, w_up, a0, a_up, k_k, k_a = rw[:7]
    row = lambda v: v.reshape(1, -1)
    full = lambda shape: pl.BlockSpec(shape, lambda i: (0, 0))
    pin_spec = full((1, SHIFT_W)) if carry_mode else pl.BlockSpec((tm, SHIFT_W), lambda i: (i, 0))
    out = jax.ShapeDtypeStruct((t, DA), F32)
    return pl.pallas_call(
        functools.partial(_prep_kernel, carry_mode, period),
        grid=(t // tm,),
        in_specs=[pl.BlockSpec((tm, 3 * DA), lambda i: (i, 0)),
                  pl.BlockSpec((tm, LANES), lambda i: (i, C_WDAD // LANES)),
                  pin_spec, full((1, SHIFT_W)), full((1, DA)), full((LORA, DA)), full((1, DA)), full((LORA, DA)),
                  full((1, DA)), full((1, DA)), full((DA, DA))],
        out_specs=[pl.BlockSpec((tm, DA), lambda i: (i, 0))] * 6,
        out_shape=[out] * 6,
        scratch_shapes=[pltpu.VMEM((1, SHIFT_W), F32)],
        compiler_params=_cparams("arbitrary"),
    )(p, p, pinit, row(mu), row(w0), w_up, row(a0), a_up, row(k_k), row(k_a), ones_da)


def _seg_sum_bcast(x, ones2_ref):
    hi = x.astype(BF16)
    lo = (x - hi.astype(F32)).astype(BF16)
    return _dot(jnp.concatenate([hi, lo], axis=1), ones2_ref[...])


def _scan_kernel(n_tok, r_ref, w_ref, k_ref, v_ref, kn_ref, b_ref, s0_ref, rk_ref, g_ref, beta_ref,
                 ones2_ref, y_o, s_o, s_ref, y_ref):
    c = pl.program_id(1)

    @pl.when(c == 0)
    def _():
        for p in range(N_PAIR):
            s_ref[p] = jnp.concatenate([s0_ref[2 * p], s0_ref[2 * p + 1]], axis=1)

    lane = lax.broadcasted_iota(I32, (DHA, LANES), 1)
    sub = lax.broadcasted_iota(I32, (DHA, LANES), 0)
    diag = (lane & (DHA - 1)) == sub
    diag_bf = jnp.where(diag, 1.0, 0.0).astype(BF16)
    lo_lanes = lane < DHA
    lo_row = lax.broadcasted_iota(I32, (SUBLANES, LANES), 1) < DHA
    sub8v = lax.broadcasted_iota(I32, (SUBLANES, DHA), 0)
    pairs = range(N_PAIR)

    def on_diag(row):
        packed = jnp.broadcast_to(row, (2 * SUBLANES, LANES)).astype(BF16)
        return jnp.concatenate([packed] * (DHA // (2 * SUBLANES)), axis=0) * diag_bf

    def group(base, n):
        rows8 = pl.ds(base, SUBLANES)
        tile = lambda ref, p: ref[rows8, p * LANES:(p + 1) * LANES]
        v_hi = [tile(v_ref, p).astype(BF16).astype(F32) for p in pairs]
        v_lo = [tile(v_ref, p) - v_hi[p] for p in pairs]
        kn_a = [jnp.where(lo_row, tile(kn_ref, p), 0.0) for p in pairs]
        kn_b = [jnp.where(lo_row, 0.0, tile(kn_ref, p)) for p in pairs]
        kn_next = [pltpu.roll(tile(kn_ref, p), SUBLANES - 1, 0) for p in pairs]
        c_b = [_seg_sum_bcast(tile(b_ref, p) * kn_next[p], ones2_ref) for p in pairs]
        c_k = [_seg_sum_bcast(tile(k_ref, p) * kn_next[p], ones2_ref) for p in pairs]
        r16 = [jnp.concatenate([jnp.where(lo_row, tile(r_ref, p), 0.0), jnp.where(lo_row, 0.0, tile(r_ref, p))],
                               axis=0).astype(BF16) for p in pairs]
        y_lo = [jnp.zeros((SUBLANES, DHA), F32) for _ in pairs]
        y_hi = [jnp.zeros((SUBLANES, DHA), F32) for _ in pairs]

        def emit_y(p, tok, s):
            yy = _dot_nt(r16[p], s.astype(BF16))
            y_lo[p] = jnp.where(sub8v == tok, yy[:SUBLANES], y_lo[p])
            y_hi[p] = jnp.where(sub8v == tok, yy[SUBLANES:], y_hi[p])

        half_sums = lambda x, a, b: jnp.where(lo_lanes, jnp.sum(x * a, axis=1, keepdims=True),
                                              jnp.sum(x * b, axis=1, keepdims=True))
        state = [s_ref[p] for p in pairs]
        sums = {}

        def reduce_step(ps, u):
            r0, r1 = slice(u, u + 1), slice(u + 1, u + 2)
            for p in ps:
                w0 = tile(w_ref, p)[r0]
                sums[p] = (half_sums(state[p], kn_a[p][r0], kn_b[p][r0]),
                           half_sums(state[p], w0 * kn_a[p][r1], w0 * kn_b[p][r1]))

        mids = {}

        def update_step(ps, u, vb):
            r0, r1 = slice(u, u + 1), slice(u + 1, u + 2)
            for p in ps:
                sa0, sa1 = sums[p]
                fix = sa1 - sa0 * c_b[p][r0] + vb[p][0] * c_k[p][r0]
                s_mid = state[p] * tile(w_ref, p)[r0] - sa0 * tile(b_ref, p)[r0] + vb[p][0] * tile(k_ref, p)[r0]
                state[p] = s_mid * tile(w_ref, p)[r1] - fix * tile(b_ref, p)[r1] + vb[p][1] * tile(k_ref, p)[r1]
                mids[p] = s_mid

        for u in range(0, n, 2):
            vb = [[_dot(jnp.concatenate([on_diag(v_hi[p][r]), on_diag(v_lo[p][r])], axis=1), ones2_ref[...])
                   for r in (slice(u, u + 1), slice(u + 1, u + 2))] for p in pairs]
            reduce_step(pairs, u)
            update_step(pairs, u, vb)
            for p in pairs:
                emit_y(p, u, mids[p])
                emit_y(p, u + 1, state[p])
        for p in pairs:
            s_ref[p] = state[p]
            y_ref[rows8, p * LANES:(p + 1) * LANES] = jnp.concatenate([y_lo[p], y_hi[p]], axis=1)

    n_full = n_tok // SUBLANES

    def full_group(g, carry):
        group(pl.multiple_of(g * SUBLANES, SUBLANES), SUBLANES)
        return carry

    lax.fori_loop(0, n_full, full_group, 0)
    if n_tok % SUBLANES:
        group(n_full * SUBLANES, n_tok % SUBLANES)

    ones = ones2_ref[:LANES, :]
    for p in pairs:
        sl = slice(p * LANES, (p + 1) * LANES)
        y = y_ref[:, sl]
        mean = _dot_ones(y, ones) * (1.0 / DHA)
        yc = y - mean
        var = _dot_ones(yc * yc, ones) * (1.0 / DHA)
        yn = yc * lax.rsqrt(var + LNX_EPS) * g_ref[:, sl] + beta_ref[:, sl]
        bonus = _dot_ones(r_ref[:, sl] * k_ref[:, sl] * rk_ref[:, sl], ones) * v_ref[:, sl]
        y_o[:, sl] = yn + bonus

    @pl.when(c == pl.num_programs(1) - 1)
    def _():
        for p in pairs:
            s_o[2 * p] = s_ref[p][:, :DHA]
            s_o[2 * p + 1] = s_ref[p][:, DHA:]


def _rwkv_scan(seqs, s0, rw, ones_da, tc, n_tok):
    bsz, t, _ = seqs[0].shape
    assert tc == -(-n_tok // SUBLANES) * SUBLANES
    assert n_tok % 2 == 0
    r_k, lnx_g, lnx_b = rw[7:10]
    row = lambda v: v.reshape(1, DA)
    seq_spec = pl.BlockSpec((None, tc, DA), lambda b, c: (b, c, 0))
    st_spec = pl.BlockSpec((None, HA, DHA, DHA), lambda b, c: (b, 0, 0, 0))
    full = lambda shape: pl.BlockSpec(shape, lambda b, c: (0, 0))
    return pl.pallas_call(
        functools.partial(_scan_kernel, n_tok),
        grid=(bsz, t // tc),
        in_specs=[seq_spec] * 6 + [st_spec, full((1, DA)), full((1, DA)), full((1, DA)), full((2 * LANES, LANES))],
        out_specs=[seq_spec, st_spec],
        out_shape=[jax.ShapeDtypeStruct((bsz, t, DA), F32), jax.ShapeDtypeStruct((bsz, HA, DHA, DHA), F32)],
        scratch_shapes=[pltpu.VMEM((N_PAIR, DHA, LANES), F32), pltpu.VMEM((tc, DA), F32)],
        compiler_params=_cparams("parallel", "arbitrary"),
    )(*seqs, s0, row(r_k), row(lnx_g), row(lnx_b), jnp.concatenate([ones_da[:LANES, :LANES]] * 2, axis=0))


def _rope_kernel(qb_ref, kb_ref, vb_ref, qi_ref, ki_ref, c128_ref, s128_ref, c64_ref, s64_ref,
                 q_o, k_o, v_o, kbf_o, vbf_o, qi_o, ki_o, kcat_o):
    c128, s128, c64, s64 = c128_ref[...], s128_ref[...], c64_ref[...], s64_ref[...]
    lane = lax.broadcasted_iota(I32, c64.shape, 1)
    first_half = (lane & (DI - 1)) < DI // 2

    def rope128(x):
        return x * c128 + pltpu.roll(x, DHB // 2, 1) * s128

    def rope64(x):
        partner = jnp.where(first_half, pltpu.roll(x, LANES - DI // 2, 1), pltpu.roll(x, DI // 2, 1))
        return x * c64 + partner * s64

    for h in range(HB):
        sl = slice(h * DHB, (h + 1) * DHB)
        q_o[:, sl] = rope128(qb_ref[:, sl]).astype(BF16)
        kr = rope128(kb_ref[:, sl])
        k_o[h] = kr
        v_o[h] = vb_ref[:, sl]
        kbf_o[:, sl] = kr.astype(BF16)
    vbf_o[...] = vb_ref[...].astype(BF16)
    for h2 in range(HI * DI // LANES):
        sl = slice(h2 * LANES, (h2 + 1) * LANES)
        qi_o[:, sl] = rope64(qi_ref[:, sl])
    ki = rope64(ki_ref[...])[:, :DI]
    ki_o[...] = ki
    kcat_o[...] = _index_key_cat(ki)


def _rope_tables(pos, d):
    inv = ROPE_THETA ** (-jnp.arange(0, d, 2, dtype=F32) / d)
    ang = pos.astype(F32)[:, None] * inv[None, :]
    cos, sin = jnp.cos(ang), jnp.sin(ang)
    reps = LANES // d
    c = jnp.tile(jnp.concatenate([cos, cos], axis=1), (1, reps))
    s = jnp.tile(jnp.concatenate([-sin, sin], axis=1), (1, reps))
    return c, s


def _rope(p, pos, tm):
    t = p.shape[0]
    c128, s128 = _rope_tables(pos, DHB)
    c64, s64 = _rope_tables(pos, DI)
    col = lambda w, c0: pl.BlockSpec((tm, w), lambda i: (i, c0 // w))
    rows = lambda w: pl.BlockSpec((tm, w), lambda i: (i, 0))
    sds = lambda w, dt: jax.ShapeDtypeStruct((t, w), dt)
    heads = pl.BlockSpec((HB, tm, DHB), lambda i: (0, i, 0))
    hsds = jax.ShapeDtypeStruct((HB, t, DHB), F32)
    return pl.pallas_call(
        _rope_kernel,
        grid=(t // tm,),
        in_specs=[col(DB, C_QB), col(DB, C_KB), col(DB, C_VB), col(HI * DI, C_QI), col(LANES, C_KI),
                  rows(LANES), rows(LANES), rows(LANES), rows(LANES)],
        out_specs=[rows(DB), heads, heads, rows(DB), rows(DB), rows(HI * DI), rows(DI), rows(4 * DI)],
        out_shape=[sds(DB, BF16), hsds, hsds, sds(DB, BF16), sds(DB, BF16), sds(HI * DI, F32), sds(DI, F32),
                   sds(4 * DI, BF16)],
        compiler_params=_cparams("parallel"),
    )(p, p, p, p, p, c128, s128, c64, s64)


def _canonical(score):
    return jnp.where(score == 0.0, 0.0, score)


def _ordinal_to_float(u):
    s = u ^ INT_MIN
    b = s ^ ((s >> 31) & 0x7FFFFFFF)
    return lax.bitcast_convert_type(b, F32)


def _index_query_cat(qi):
    qi_hi, qi_lo = _split2(qi)
    out = []
    for h in range(HI):
        sl = slice(h * DI, (h + 1) * DI)
        out.append(jnp.concatenate([qi_hi[:, sl], qi_hi[:, sl], qi_lo[:, sl], qi_lo[:, sl]], axis=1))
    return out


def _index_key_cat(ki):
    k_hi, k_lo = _split2(ki)
    return jnp.concatenate([k_hi, k_lo, k_hi, k_lo], axis=1)


def _count(sc_ref, nc, ck, pred):
    n_acc = 8

    def body(c, accs):
        accs = list(accs)
        base = pl.multiple_of(c * ck, ck)
        for j in range(ck // SUBLANES):
            row0 = base + j * SUBLANES
            accs[j % n_acc] = accs[j % n_acc] + jnp.where(pred(sc_ref[pl.ds(row0, SUBLANES), :], row0), 1, 0)
        return tuple(accs)

    accs = lax.fori_loop(0, nc, body, (jnp.zeros((SUBLANES, LANES), I32),) * n_acc)
    return jnp.sum(functools.reduce(lambda a, b: a + b, accs), axis=0, keepdims=True)


def _select_bias(sc_ref, bias_ref, cut_ref, nc, nc_total, ck, topk, idx_bits):
    sub8 = lax.broadcasted_iota(I32, (SUBLANES, LANES), 0)
    sub128 = lax.broadcasted_iota(I32, (LANES, LANES), 0)

    bits_per_check = 4

    def bit_group(g, carry):
        def refine(carry):
            ans_u, done = carry
            for b in range(bits_per_check):
                cand_u = ans_u | lax.shift_left(jnp.int32(1), 31 - (g * bits_per_check + b))
                cand = _ordinal_to_float(cand_u)
                cnt = _count(sc_ref, nc, ck, lambda blk, _: blk >= cand)
                take = (cnt >= topk) & (done == 0)
                ans_u, done = jnp.where(take, cand_u, ans_u), jnp.where(take & (cnt == topk), 1, done)
            return ans_u, done

        return lax.cond(jnp.min(carry[1]) == 0, refine, lambda carry: carry, carry)

    zeros = jnp.zeros((1, LANES), I32)
    ans_u = lax.fori_loop(0, 32 // bits_per_check, bit_group, (zeros, zeros))[0]
    ans_u = jnp.where((ans_u >= 0) & (ans_u < ORD_NEG_INF), ORD_NEG_INF, ans_u)
    thr = _ordinal_to_float(ans_u)
    n_gt = _count(sc_ref, nc, ck, lambda blk, _: blk > thr)
    n_ge = _count(sc_ref, nc, ck, lambda blk, _: blk >= thr)
    need = topk - n_gt
    tied = ((n_ge - n_gt) > need) & (thr > -jnp.inf)
    cut_ref[...] = jnp.full((1, LANES), 2 ** 30, I32)

    @pl.when(jnp.max(tied.astype(I32)) > 0)
    def _():
        def tie_body(it, ans):
            cand = ans | lax.shift_left(jnp.int32(1), idx_bits - 1 - it)
            g = _count(sc_ref, nc, ck, lambda blk, row0: (blk == thr) & ((sub8 + row0) < cand))
            return jnp.where(g < need, cand, ans)

        cut = lax.fori_loop(0, idx_bits, tie_body, jnp.zeros((1, LANES), I32))
        cut_ref[...] = jnp.where(tied, cut, 2 ** 30)

    cut = cut_ref[...]

    def write(c, carry):
        base = pl.multiple_of(c * ck, ck)
        for u in range(ck // LANES):
            off = base + u * LANES
            blk = sc_ref[pl.ds(off, LANES), :]
            sel = ((blk > thr) | ((blk == thr) & ((sub128 + off) <= cut))) & (blk > -jnp.inf)
            bias_ref[:, pl.ds(off, LANES)] = jnp.where(sel, 0.0, NEG_BIAS).T.astype(bias_ref.dtype)
        return carry

    lax.fori_loop(0, nc, write, 0)

    def fill(c, carry):
        base = pl.multiple_of(c * ck, ck)
        bias_ref[:, pl.ds(base, ck)] = jnp.full((bias_ref.shape[0], ck), NEG_BIAS, bias_ref.dtype)
        return carry

    lax.fori_loop(nc, nc_total, fill, 0)


def _prompt_select_kernel(topk, ck, idx_bits, qi_ref, wi_ref, kcat_ref, bias_o, sc_ref, cut_ref):
    i = pl.program_id(0)
    tq = qi_ref.shape[0]
    t = kcat_ref.shape[0]
    nc = ((i + 1) * tq + ck - 1) // ck
    q_cat = _index_query_cat(qi_ref[...])
    q_pair = [jnp.concatenate(q_cat[2 * j:2 * j + 2], axis=0) for j in range(HI // 2)]
    wi_t = wi_ref[...].T
    kpos = lax.broadcasted_iota(I32, (ck, tq), 0)
    qpos = i * tq + lax.broadcasted_iota(I32, (ck, tq), 1)
    scale = HI ** -0.5 * DI ** -0.5

    def score_chunk(c, carry):
        base = pl.multiple_of(c * ck, ck)
        k_cat = kcat_ref[pl.ds(base, ck), :]
        acc = None
        for h2 in range(HI // 2):
            s2 = _dot_nt(k_cat, q_pair[h2])
            for e in range(2):
                h = 2 * h2 + e
                term = wi_t[h:h + 1, :] * jnp.maximum(s2[:, e * tq:(e + 1) * tq], 0.0)
                acc = term if acc is None else acc + term
        sc = jnp.where(kpos + base <= qpos, acc * scale, -jnp.inf)
        sc_ref[pl.ds(base, ck), :] = _canonical(sc)
        return carry

    lax.fori_loop(0, nc, score_chunk, 0)
    _select_bias(sc_ref, bias_o, cut_ref, nc, t // ck, ck, topk, idx_bits)


def _prompt_select(qi, p, k_cat, ck):
    t = qi.shape[0]
    tq = LANES
    topk = min(TOPK_MAX, t // 4)
    idx_bits = int(np.ceil(np.log2(t))) + 1
    return pl.pallas_call(
        functools.partial(_prompt_select_kernel, topk, ck, idx_bits),
        grid=(t // tq,),
        in_specs=[pl.BlockSpec((tq, HI * DI), lambda i: (i, 0)),
                  pl.BlockSpec((tq, LANES), lambda i: (i, C_WI // LANES)),
                  pl.BlockSpec((t, 4 * DI), lambda i: (0, 0))],
        out_specs=pl.BlockSpec((tq, t), lambda i: (i, 0)),
        out_shape=jax.ShapeDtypeStruct((t, t), BF16),
        scratch_shapes=[pltpu.VMEM((t, tq), F32), pltpu.VMEM((1, LANES), I32)],
        compiler_params=_cparams("parallel"),
    )(qi, p, k_cat)


def _sample_score_kernel(pps, rq, n_new, pt_ref, qit_ref, wif_ref, *refs):
    page_refs, knew_ref, skey_o, skey_new_o = refs[:pps], refs[pps], refs[pps + 1], refs[pps + 2]
    j = pl.program_id(1)
    q_hi, q_lo = _split2(qit_ref[...])
    q_cat = jnp.concatenate([q_hi, q_hi, q_lo, q_lo], axis=1)
    wif = wif_ref[...]

    def scores(kt):
        k_hi, k_lo = _split2(kt)
        at = _dot(q_cat, jnp.concatenate([k_hi, k_lo, k_hi, k_lo], axis=0))
        at = jnp.maximum(at, 0.0) * wif
        sc = at[0:rq]
        for h in range(1, HI):
            sc = sc + at[h * rq:(h + 1) * rq]
        return sc * (HI ** -0.5 * DI ** -0.5)

    sc = scores(jnp.concatenate([r[...] for r in page_refs], axis=1))
    trow = lax.broadcasted_iota(I32, sc.shape, 0)
    skey_o[...] = _canonical(jnp.where(trow < n_new, sc, -jnp.inf))

    @pl.when(j == pl.num_programs(1) - 1)
    def _():
        scn = scores(knew_ref[...])
        tr = lax.broadcasted_iota(I32, scn.shape, 0)
        ln = lax.broadcasted_iota(I32, scn.shape, 1)
        skey_new_o[...] = _canonical(jnp.where((tr < n_new) & (ln <= tr), scn, -jnp.inf))


def _sample_scores(qit, wif, cache_kidx, knew_pad, page_table, n_new, rq, pps):
    bsz, n_pages = page_table.shape
    per_b = lambda b, j, pt: (b, 0, 0)
    page_spec = lambda u: pl.BlockSpec((None, DI, PAGE), lambda b, j, pt: (pt[b, j * pps + u], 0, 0))
    grid_spec = pltpu.PrefetchScalarGridSpec(
        num_scalar_prefetch=1,
        grid=(bsz, n_pages // pps),
        in_specs=[pl.BlockSpec((None, LANES, DI), per_b), pl.BlockSpec((None, LANES, 1), per_b)]
                 + [page_spec(u) for u in range(pps)] + [pl.BlockSpec((None, DI, PAGE), per_b)],
        out_specs=[pl.BlockSpec((None, rq, pps * PAGE), lambda b, j, pt: (b, 0, j)),
                   pl.BlockSpec((None, rq, PAGE), per_b)],
    )
    return pl.pallas_call(
        functools.partial(_sample_score_kernel, pps, rq, n_new),
        grid_spec=grid_spec,
        out_shape=[jax.ShapeDtypeStruct((bsz, rq, n_pages * PAGE), F32), jax.ShapeDtypeStruct((bsz, rq, PAGE), F32)],
        compiler_params=_cparams("parallel", "arbitrary"),
    )(page_table, qit, wif, *([cache_kidx] * pps), knew_pad)


def _sample_select_kernel(topk, ck, idx_bits, sc_past, sc_new, bias_o, sc_ref, cut_ref):
    past = sc_past.shape[1]

    def flip(c, carry):
        off = pl.multiple_of(c * LANES, LANES)
        sc_ref[pl.ds(off, LANES), :] = sc_past[:, pl.ds(off, LANES)].T
        return carry

    lax.fori_loop(0, past // LANES, flip, 0)
    sc_ref[past:, :] = sc_new[...].T
    nc = sc_ref.shape[0] // ck
    _select_bias(sc_ref, bias_o, cut_ref, nc, nc, ck, topk, idx_bits)


def _sample_select(skey_past, skey_new, topk):
    rows, past = skey_past.shape
    n = past + skey_new.shape[1]
    tr = LANES
    idx_bits = int(np.ceil(np.log2(n))) + 1
    return pl.pallas_call(
        functools.partial(_sample_select_kernel, topk, PAGE, idx_bits),
        grid=(rows // tr,),
        in_specs=[pl.BlockSpec((tr, past), lambda i: (i, 0)), pl.BlockSpec((tr, n - past), lambda i: (i, 0))],
        out_specs=pl.BlockSpec((tr, n), lambda i: (i, 0)),
        out_shape=jax.ShapeDtypeStruct((rows, n), F32),
        scratch_shapes=[pltpu.VMEM((n, tr), F32), pltpu.VMEM((1, LANES), I32)],
        compiler_params=_cparams("parallel"),
    )(skey_past, skey_new)


def _prompt_attn_kernel(ck, q_ref, k_ref, v_ref, bias_ref, o_ref, m_ref, l_ref, acc_ref):
    i = pl.program_id(0)
    tq = q_ref.shape[0]
    nc = ((i + 1) * tq + ck - 1) // ck
    m_ref[...] = jnp.full(m_ref.shape, NEG_BIAS, F32)
    l_ref[...] = jnp.zeros(l_ref.shape, F32)
    acc_ref[...] = jnp.zeros(acc_ref.shape, F32)
    scale = DHB ** -0.5

    def chunk(c, carry):
        base = pl.multiple_of(c * ck, ck)
        bias = bias_ref[:, pl.ds(base, ck)].astype(F32)
        for h in range(HB):
            sl = slice(h * DHB, (h + 1) * DHB)
            s = _dot_nt(q_ref[:, sl], k_ref[pl.ds(base, ck), sl]) * scale + bias
            m_old = m_ref[h]
            m_new = jnp.maximum(m_old, jnp.max(s, axis=1, keepdims=True))
            alpha = jnp.exp(m_old - m_new)
            p = jnp.exp(s - m_new)
            l_ref[h] = alpha * l_ref[h] + jnp.sum(p, axis=1, keepdims=True)
            acc_ref[h] = alpha * acc_ref[h] + _dot(p.astype(BF16), v_ref[pl.ds(base, ck), sl])
            m_ref[h] = m_new
        return carry

    lax.fori_loop(0, nc, chunk, 0)
    for h in range(HB):
        o_ref[:, h * DHB:(h + 1) * DHB] = acc_ref[h] / l_ref[h]


def _prompt_attn(q_bf, k_bf, v_bf, bias, tq, ck):
    t = q_bf.shape[0]
    resident = lambda: pl.BlockSpec((t, DB), lambda i: (0, 0), pipeline_mode=pl.Buffered(1))
    return pl.pallas_call(
        functools.partial(_prompt_attn_kernel, ck),
        grid=(t // tq,),
        in_specs=[pl.BlockSpec((tq, DB), lambda i: (i, 0)), resident(), resident(),
                  pl.BlockSpec((tq, t), lambda i: (i, 0))],
        out_specs=pl.BlockSpec((tq, DB), lambda i: (i, 0)),
        out_shape=jax.ShapeDtypeStruct((t, DB), F32),
        scratch_shapes=[pltpu.VMEM((HB, tq, 1), F32), pltpu.VMEM((HB, tq, 1), F32), pltpu.VMEM((HB, tq, DHB), F32)],
        compiler_params=_cparams("parallel"),
    )(q_bf, k_bf, v_bf, bias)


def _sample_attn_kernel(pps, rq, pt_ref, qbd_ref, *refs):
    kpages, vpages = refs[:pps], refs[pps:2 * pps]
    knew_ref, vnew_ref, bias_ref, bias_new_ref, o_ref, m_ref, l_ref, acc_ref = refs[2 * pps:]
    j = pl.program_id(1)
    nr = HB * rq

    @pl.when(j == 0)
    def _():
        m_ref[...] = jnp.full(m_ref.shape, NEG_BIAS, F32)
        l_ref[...] = jnp.zeros(l_ref.shape, F32)
        acc_ref[...] = jnp.zeros(acc_ref.shape, F32)

    def accumulate(kp, vp, bias):
        st = _dot(kp, qbd_ref[...])
        s = st.T[:nr] * (DHB ** -0.5) + jnp.concatenate([bias] * HB, axis=0)
        m_old = m_ref[...]
        m_new = jnp.maximum(m_old, jnp.max(s, axis=1, keepdims=True))
        alpha = jnp.exp(m_old - m_new)
        p = jnp.exp(s - m_new)
        l_ref[...] = alpha * l_ref[...] + jnp.sum(p, axis=1, keepdims=True)
        acc_ref[...] = alpha * acc_ref[...] + _dot(p.astype(BF16), vp)
        m_ref[...] = m_new

    gather = lambda pages: jnp.concatenate(
        [jnp.concatenate([r[h].astype(BF16) for h in range(HB)], axis=1) for r in pages], axis=0)
    accumulate(gather(kpages), gather(vpages), bias_ref[...])

    @pl.when(j == pl.num_programs(1) - 1)
    def _():
        accumulate(knew_ref[...].astype(BF16), vnew_ref[...].astype(BF16), bias_new_ref[...])
        for h in range(HB):
            rs = slice(h * rq, (h + 1) * rq)
            cs = slice(h * DHB, (h + 1) * DHB)
            o_ref[:, cs] = acc_ref[rs, cs] / l_ref[rs, :]


def _sample_attn(qbd, cache_k, cache_v, knew_pad, vnew_pad, bias, page_table, rq, pps):
    bsz, n_pages = page_table.shape
    page_spec = lambda u: pl.BlockSpec((None, HB, PAGE, DHB), lambda b, j, pt: (pt[b, j * pps + u], 0, 0, 0))
    per_b = lambda b, j, pt: (b, 0, 0)
    grid_spec = pltpu.PrefetchScalarGridSpec(
        num_scalar_prefetch=1,
        grid=(bsz, n_pages // pps),
        in_specs=[pl.BlockSpec((None, DB, LANES), per_b)]
                 + [page_spec(u) for u in range(pps)] * 2
                 + [pl.BlockSpec((None, PAGE, DB), per_b), pl.BlockSpec((None, PAGE, DB), per_b),
                    pl.BlockSpec((None, rq, pps * PAGE), lambda b, j, pt: (b, 0, j)),
                    pl.BlockSpec((None, rq, PAGE), lambda b, j, pt: (b, 0, n_pages))],
        out_specs=pl.BlockSpec((None, rq, DB), per_b),
        scratch_shapes=[pltpu.VMEM((HB * rq, 1), F32), pltpu.VMEM((HB * rq, 1), F32), pltpu.VMEM((HB * rq, DB), F32)],
    )
    return pl.pallas_call(
        functools.partial(_sample_attn_kernel, pps, rq),
        grid_spec=grid_spec,
        out_shape=jax.ShapeDtypeStruct((bsz, rq, DB), F32),
        compiler_params=_cparams("parallel", "arbitrary"),
    )(page_table, qbd, *([cache_k] * pps), *([cache_v] * pps), knew_pad, vnew_pad, bias, bias)


def _cross_kernel(q_ref, mk_ref, mv_ref, o_ref):
    scale = DHC ** -0.5
    for h in range(HC):
        sl = slice(h * DHC, (h + 1) * DHC)
        s = _dot_nt(q_ref[:, sl].astype(BF16), mk_ref[h].astype(BF16)) * scale
        m = jnp.max(s, axis=1, keepdims=True)
        p = jnp.exp(s - m)
        l = jnp.sum(p, axis=1, keepdims=True)
        o_ref[:, sl] = _dot(p.astype(BF16), mv_ref[h].astype(BF16)) / l


def _cross_attn(q_arr, q_col, mk, mv, tm):
    bsz, t, _ = q_arr.shape
    return pl.pallas_call(
        _cross_kernel,
        grid=(bsz, t // tm),
        in_specs=[pl.BlockSpec((None, tm, DC), lambda b, i: (b, i, q_col // DC)),
                  pl.BlockSpec((None, HC, N_MEM, DHC), lambda b, i: (b, 0, 0, 0)),
                  pl.BlockSpec((None, HC, N_MEM, DHC), lambda b, i: (b, 0, 0, 0))],
        out_specs=pl.BlockSpec((None, tm, DC), lambda b, i: (b, i, 0)),
        out_shape=jax.ShapeDtypeStruct((bsz, t, DC), F32),
        compiler_params=_cparams("parallel", "parallel"),
    )(q_arr, mk, mv)


def _merge_kernel(x_ref, ya_ref, za_ref, yb_ref, zb_ref, yc_ref, zc_ref, ga_ref, gb_ref, gc_ref,
                  bga_ref, bgb_ref, bgc_ref, wa_ref, wb_ref, wc_ref, wo_ref, fg_ref, o_ref):
    ba = _dot((ya_ref[...] * _silu(za_ref[...])).astype(BF16), wa_ref[...])
    bb = _dot((yb_ref[...] * _silu(zb_ref[...])).astype(BF16), wb_ref[...])
    bc = _dot((yc_ref[...] * _silu(zc_ref[...])).astype(BF16), wc_ref[...])
    mix = (_sigmoid(ga_ref[...] + bga_ref[...]) * ba + _sigmoid(gb_ref[...] + bgb_ref[...]) * bb
           + _sigmoid(gc_ref[...] + bgc_ref[...]) * bc)
    y = x_ref[...] + _dot(mix.astype(BF16), wo_ref[...])
    ms = jnp.mean(y * y, axis=-1, keepdims=True)
    o_ref[...] = y * lax.rsqrt(ms + EPS) * fg_ref[...]


def _merge(x2d, p, ya, yb, yc, b_gate, wa, wb, wc, wo, fg, tm):
    t, d = x2d.shape
    rows = lambda w: pl.BlockSpec((tm, w), lambda i: (i, 0))
    col = lambda w, c0: pl.BlockSpec((tm, w), lambda i: (i, c0 // w))
    const = lambda shape, c=0: pl.BlockSpec(shape, lambda i: (0, c), pipeline_mode=pl.Buffered(1))
    bg = b_gate.reshape(1, 3 * d)
    return pl.pallas_call(
        _merge_kernel,
        grid=(t // tm,),
        in_specs=[rows(d), rows(DA), col(DA, C_ZA), rows(DB), col(DB, C_ZB), rows(DC), col(DC, C_ZC),
                  col(d, C_GATES), col(d, C_GATES + d), col(d, C_GATES + 2 * d),
                  const((1, d), 0), const((1, d), 1), const((1, d), 2),
                  const((DA, d)), const((DB, d)), const((DC, d)), const((d, d)), const((1, d))],
        out_specs=rows(d),
        out_shape=jax.ShapeDtypeStruct((t, d), F32),
        compiler_params=_cparams("parallel"),
    )(x2d, ya, p, yb, p, yc, p, p, p, p, bg, bg, bg, wa, wb, wc, wo, fg.reshape(1, d))


def _block_ones(n, seg):
    idx = np.arange(n) // seg
    return jnp.asarray(idx[:, None] == idx[None, :], dtype=BF16)


def _shift_row(p_row):
    return jnp.concatenate([p_row[:3 * DA], p_row[C_WDAD:C_WDAD + 2 * LORA]])


def kernel(x_prompt, x_sample, mem_prompt, cache_k, cache_v, cache_kidx, cache_mem_k, cache_mem_v, state_wkv, state_shift, page_table, norm_g, w_in, rwkv_mu, rwkv_w0, rwkv_w_up, rwkv_a0, rwkv_a_up, rwkv_k_k, rwkv_k_a, rwkv_r_k, rwkv_lnx_g, rwkv_lnx_b, mem_norm_g, w_mem_kv, b_gate, w_br_a, w_br_b, w_br_c, w_out, final_norm_g):
    depth = w_in.shape[0]
    assert depth == 1 and x_prompt.shape[0] == 1
    t = x_prompt.shape[1]
    bs, ts, d = x_sample.shape
    n_pages = page_table.shape[1]
    past = n_pages * PAGE
    rq = 8
    assert ts <= rq and (ts & (ts - 1)) == 0

    rw = (rwkv_mu[0], rwkv_w0[0], rwkv_w_up[0], rwkv_a0[0], rwkv_a_up[0], rwkv_k_k[0], rwkv_k_a[0],
          rwkv_r_k[0].reshape(DA), rwkv_lnx_g[0], rwkv_lnx_b[0])
    ones_da = _block_ones(DA, DHA)
    w_p = _permute_w_in(w_in[0])
    wa, wb, wc, wo = (w[0].astype(BF16) for w in (w_br_a, w_br_b, w_br_c, w_out))

    xp2 = x_prompt.reshape(t, d)
    xs2 = x_sample.reshape(bs * ts, d)
    tm_p = min(1024, t)
    p_p = _proj(xp2, norm_g[0], w_p, tm_p, 1024)
    p_s = _proj(xs2, norm_g[0], w_p, bs * ts, 1024)
    mkv = _proj(mem_prompt.reshape(N_MEM, d), mem_norm_g[0], w_mem_kv[0].astype(BF16), N_MEM, 2 * DC)
    mk_p = mkv[:, :DC].reshape(1, N_MEM, HC, DHC)
    mv_p = mkv[:, DC:].reshape(1, N_MEM, HC, DHC)
    head_major = lambda a: jnp.transpose(a, (0, 2, 1, 3))

    seq_p = _rwkv_prep(p_p, jnp.zeros((1, SHIFT_W), F32), rw, ones_da, min(256, t), True)
    ya_p, wkv_p = _rwkv_scan([a[None] for a in seq_p], jnp.zeros((1, HA, DHA, DHA), F32), rw, ones_da,
                             min(256, t), min(256, t))
    pinit_s = jnp.repeat(state_shift[0], ts, axis=0)
    seq_s = _rwkv_prep(p_s, pinit_s, rw, ones_da, bs * ts, False, ts)
    pad_t = lambda a: jnp.pad(a.reshape(bs, ts, -1), ((0, 0), (0, rq - ts), (0, 0)))
    ya_s, wkv_s = _rwkv_scan([pad_t(a) for a in seq_s], state_wkv[0], rw, ones_da, rq, ts)
    ya_s = ya_s[:, :ts].reshape(bs * ts, DA)

    tokens_major = lambda a: jnp.transpose(a, (1, 0, 2))
    q_p, k_p, v_p, kbf_p, vbf_p, qi_p, ki_p, kcat_p = _rope(p_p, jnp.arange(t), min(512, t))
    pos_s = jnp.tile(past + jnp.arange(ts), bs)
    q_s, k_s, v_s, _, _, qi_s, ki_s, _ = _rope(p_s, pos_s, bs * ts)
    k_s, v_s = tokens_major(k_s), tokens_major(v_s)

    bias_p = _prompt_select(qi_p, p_p, kcat_p, 512)
    yb_p = _prompt_attn(q_p, kbf_p, vbf_p, bias_p, min(256, t), 1024)

    topk_s = min(TOPK_MAX, (past + ts) // 4)
    pad_rows = lambda a, n: jnp.pad(a, ((0, 0), (0, n - a.shape[1]), (0, 0)))
    pps = 8 if n_pages % 8 == 0 else 1
    qit = pad_rows(qi_s.reshape(bs, ts, HI * DI), rq).reshape(bs, rq, HI, DI).transpose(0, 2, 1, 3)
    qit = pad_rows(qit.reshape(bs, HI * rq, DI), LANES)
    wif = pad_rows(p_s[:, C_WI:C_WI + HI].reshape(bs, ts, HI), rq).transpose(0, 2, 1).reshape(bs, HI * rq, 1)
    wif = pad_rows(wif, LANES)
    kinew = jnp.swapaxes(pad_rows(ki_s.reshape(bs, ts, DI), PAGE), 1, 2)
    skey_past, skey_new = _sample_scores(qit, wif, jnp.swapaxes(cache_kidx[0], 1, 2), kinew, page_table, ts,
                                         rq, pps)
    n_rows = -(-bs * rq // LANES) * LANES
    lane_rows = lambda a: jnp.pad(a.reshape(bs * rq, -1), ((0, n_rows - bs * rq), (0, 0)), constant_values=-jnp.inf)
    bias_s = _sample_select(lane_rows(skey_past), lane_rows(skey_new), topk_s)
    bias_s = bias_s[:bs * rq].reshape(bs, rq, past + PAGE)
    knew = pad_rows(k_s.reshape(bs, ts, DB), PAGE)
    vnew = pad_rows(v_s.reshape(bs, ts, DB), PAGE)
    qh = pad_rows(q_s.reshape(bs, ts, DB), rq).reshape(bs, rq, HB, DHB).transpose(0, 2, 3, 1)
    head_eq = jnp.eye(HB, dtype=bool)[None, :, None, :, None]
    qbd = jnp.where(head_eq, qh[:, :, :, None, :], jnp.zeros((), BF16)).reshape(bs, DB, HB * rq)
    qbd = jnp.pad(qbd, ((0, 0), (0, 0), (0, LANES - HB * rq)))
    yb_s = _sample_attn(qbd, head_major(cache_k[0]), head_major(cache_v[0]), knew, vnew, bias_s, page_table,
                        rq, pps)
    yb_s = yb_s[:, :ts].reshape(bs * ts, DB)

    yc_p = _cross_attn(p_p[None], C_QC, head_major(mk_p), head_major(mv_p), min(512, t))[0]
    qc8 = pad_rows(p_s[:, C_QC:C_QC + DC].reshape(bs, ts, DC), rq)
    yc_s = _cross_attn(qc8, 0, head_major(cache_mem_k[0]), head_major(cache_mem_v[0]), rq)
    yc_s = yc_s[:, :ts].reshape(bs * ts, DC)

    y_p = _merge(xp2, p_p, ya_p[0], yb_p, yc_p, b_gate[0], wa, wb, wc, wo, final_norm_g, min(256, t))
    y_s = _merge(xs2, p_s, ya_s, yb_s, yc_s, b_gate[0], wa, wb, wc, wo, final_norm_g, bs * ts)

    p_s3 = p_s.reshape(bs, ts, N_P)
    shift_s = jnp.concatenate([p_s3[:, -1, :3 * DA], p_s3[:, -1, C_WDAD:C_WDAD + 2 * LORA]], axis=-1)
    return (y_p.reshape(1, t, d), y_s.reshape(bs, ts, d),
            tokens_major(k_p).reshape(1, 1, t, HB, DHB), tokens_major(v_p).reshape(1, 1, t, HB, DHB),
            ki_p.reshape(1, 1, t, DI),
            mk_p.reshape(1, 1, N_MEM, HC, DHC), mv_p.reshape(1, 1, N_MEM, HC, DHC),
            wkv_p.reshape(1, 1, HA, DHA, DHA), _shift_row(p_p[t - 1]).reshape(1, 1, SHIFT_W),
            k_s.reshape(1, bs, ts, HB, DHB), v_s.reshape(1, bs, ts, HB, DHB), ki_s.reshape(1, bs, ts, DI),
            wkv_s.reshape(1, bs, HA, DHA, DHA), shift_s.reshape(1, bs, SHIFT_W))
```

```python
import functools

import numpy as np
import jax
import jax.numpy as jnp
from jax import lax
from jax.experimental import pallas as pl
from jax.experimental.pallas import tpu as pltpu

F32, BF16, I32 = jnp.float32, jnp.bfloat16, jnp.int32

D_MODEL = 2048
HA, DHA = 12, 64
DA = HA * DHA
LORA = 64
LNX_EPS = 64e-5
HB, DHB = 6, 128
DB = HB * DHB
HI, DI = 8, 64
TOPK_MAX = 256
N_MEM = 256
HC, DHC = 4, 128
DC = HC * DHC
ROPE_THETA = 10000.0
EPS = 1e-6
PAGE = 128
SHIFT_W = 3 * DA + 2 * LORA
SPLIT_SIZES = (SHIFT_W, DA, DB, DB, DB, HI * DI, HI, DI, DB, DC, DC, 3 * D_MODEL)

LANES = 128
SUBLANES = 8
N_PAIR = HA // 2

C_RKV, C_ZA, C_QB, C_KB, C_VB, C_ZB = 0, 2304, 3072, 3840, 4608, 5376
C_QI, C_QC, C_ZC, C_WDAD, C_KI, C_WI, C_GATES = 6144, 6656, 7168, 7680, 7808, 7936, 8192
N_P = C_GATES + 3 * D_MODEL

NEG_BIAS = -1e30
ORD_NEG_INF = 0x007FFFFF
INT_MIN = -2147483648
VMEM_LIMIT = 56 * 1024 * 1024


def _cparams(*sem):
    return pltpu.CompilerParams(dimension_semantics=sem, vmem_limit_bytes=VMEM_LIMIT)


def _dot(a, b):
    return jnp.dot(a, b, preferred_element_type=F32)


def _dot_nt(a, b):
    return lax.dot_general(a, b, (((1,), (1,)), ((), ())), preferred_element_type=F32)


def _split2(x):
    hi = x.astype(BF16)
    lo = (x - hi.astype(F32)).astype(BF16)
    return hi, lo


def _dot_ones(x, ones_bf):
    hi = x.astype(BF16)
    r1 = x - hi.astype(F32)
    mid = r1.astype(BF16)
    lo = (r1 - mid.astype(F32)).astype(BF16)
    return _dot(hi, ones_bf) + _dot(mid, ones_bf) + _dot(lo, ones_bf)


def _dot3(a, b):
    a_hi, a_lo = _split2(a)
    b_hi, b_lo = _split2(b)
    return _dot(a_hi, b_hi) + _dot(a_hi, b_lo) + _dot(a_lo, b_hi)


def _sigmoid(x):
    return 1.0 / (1.0 + jnp.exp(-x))


def _silu(x):
    return x * _sigmoid(x)


def _proj_kernel(x_ref, g_ref, w_ref, o_ref, xn_ref):
    @pl.when(pl.program_id(1) == 0)
    def _():
        x = x_ref[...]
        ms = jnp.mean(x * x, axis=-1, keepdims=True)
        xn_ref[...] = (x * lax.rsqrt(ms + EPS) * g_ref[...]).astype(BF16)

    o_ref[...] = _dot(xn_ref[...], w_ref[...])


def _proj(x2d, g, w_bf, tm, tn):
    m, d = x2d.shape
    n = w_bf.shape[1]
    return pl.pallas_call(
        _proj_kernel,
        grid=(m // tm, n // tn),
        in_specs=[pl.BlockSpec((tm, d), lambda i, j: (i, 0)),
                  pl.BlockSpec((1, d), lambda i, j: (0, 0)),
                  pl.BlockSpec((d, tn), lambda i, j: (0, j))],
        out_specs=pl.BlockSpec((tm, tn), lambda i, j: (i, j)),
        out_shape=jax.ShapeDtypeStruct((m, n), F32),
        scratch_shapes=[pltpu.VMEM((tm, d), BF16)],
        compiler_params=_cparams("parallel", "arbitrary"),
    )(x2d, g.reshape(1, d), w_bf)


def _permute_w_in(w):
    o = [int(v) for v in np.cumsum((0,) + SPLIT_SIZES)]
    za0, qb0, kb0, vb0, qi0, wi0, ki0, zb0, qc0, zc0, g0 = o[1:12]
    z = lambda n: jnp.zeros((w.shape[0], n), w.dtype)
    parts = [w[:, 0:3 * DA], w[:, za0:za0 + DA], w[:, qb0:qb0 + DB], w[:, kb0:kb0 + DB], w[:, vb0:vb0 + DB],
             w[:, zb0:zb0 + DB], w[:, qi0:qi0 + HI * DI], w[:, qc0:qc0 + DC], w[:, zc0:zc0 + DC],
             w[:, 3 * DA:SHIFT_W], w[:, ki0:ki0 + DI], z(LANES - DI), w[:, wi0:wi0 + HI], z(LANES - HI), z(LANES),
             w[:, g0:g0 + 3 * D_MODEL]]
    out = jnp.concatenate(parts, axis=1).astype(BF16)
    assert out.shape[1] == N_P
    return out


def _prep_kernel(carry_mode, period, rkv_ref, wdad_ref, pinit_ref, mu_ref, w0_ref, wup_ref, a0_ref, aup_ref,
                 kk_ref, ka_ref, ones_ref, r_o, w_o, k_o, v_o, kn_o, b_o, carry_ref):
    tm = rkv_ref.shape[0]
    row = lax.broadcasted_iota(I32, (tm, 1), 0)
    cur = jnp.concatenate([rkv_ref[...], wdad_ref[...]], axis=1)
    rolled = pltpu.roll(cur, 1, 0)
    if carry_mode:
        @pl.when(pl.program_id(0) == 0)
        def _():
            carry_ref[...] = pinit_ref[...]
        prev = jnp.where(row == 0, carry_ref[...], rolled)
        carry_ref[...] = cur[tm - 1:tm, :]
    else:
        prev = jnp.where((row & (period - 1)) == 0, pinit_ref[...], rolled)
    m = cur + mu_ref[...] * (prev - cur)
    r = m[:, 0:DA]
    k = m[:, DA:2 * DA]
    v = m[:, 2 * DA:3 * DA]
    wd = m[:, 3 * DA:3 * DA + LORA]
    ad = m[:, 3 * DA + LORA:SHIFT_W]
    lw = w0_ref[...] + _dot3(jnp.tanh(wd), wup_ref[...])
    z = -lw
    softplus = jnp.maximum(z, 0.0) + jnp.log(1.0 + jnp.exp(-jnp.abs(z)))
    logw = -softplus - 0.5
    decay = jnp.exp(-jnp.exp(logw))
    a = _sigmoid(a0_ref[...] + _dot3(ad, aup_ref[...]))
    kk = k * kk_ref[...]
    ss = _dot_ones(kk * kk, ones_ref[...])
    kn = kk * lax.rsqrt(ss + 1e-12)
    r_o[...] = r
    w_o[...] = decay
    k_o[...] = k * (1.0 + (a - 1.0) * ka_ref[...])
    v_o[...] = v
    kn_o[...] = kn
    b_o[...] = kn * a


def _rwkv_prep(p, pinit, rw, ones_da, tm, carry_mode, period=1):
    t = p.shape[0]
    mu, w0, w_up, a0, a_up, k_k, k_a = rw[:7]
    row = lambda v: v.reshape(1, -1)
    full = lambda shape: pl.BlockSpec(shape, lambda i: (0, 0))
    pin_spec = full((1, SHIFT_W)) if carry_mode else pl.BlockSpec((tm, SHIFT_W), lambda i: (i, 0))
    out = jax.ShapeDtypeStruct((t, DA), F32)
    return pl.pallas_call(
        functools.partial(_prep_kernel, carry_mode, period),
        grid=(t // tm,),
        in_specs=[pl.BlockSpec((tm, 3 * DA), lambda i: (i, 0)),
                  pl.BlockSpec((tm, LANES), lambda i: (i, C_WDAD // LANES)),
                  pin_spec, full((1, SHIFT_W)), full((1, DA)), full((LORA, DA)), full((1, DA)), full((LORA, DA)),
                  full((1, DA)), full((1, DA)), full((DA, DA))],
        out_specs=[pl.BlockSpec((tm, DA), lambda i: (i, 0))] * 6,
        out_shape=[out] * 6,
        scratch_shapes=[pltpu.VMEM((1, SHIFT_W), F32)],
        compiler_params=_cparams("arbitrary"),
    )(p, p, pinit, row(mu), row(w0), w_up, row(a0), a_up, row(k_k), row(k_a), ones_da)


def _seg_sum_bcast(x, ones2_ref):
    hi = x.astype(BF16)
    lo = (x - hi.astype(F32)).astype(BF16)
    return _dot(jnp.concatenate([hi, lo], axis=1), ones2_ref[...])


def _scan_kernel(n_tok, r_ref, w_ref, k_ref, v_ref, kn_ref, b_ref, s0_ref, rk_ref, g_ref, beta_ref,
                 ones2_ref, y_o, s_o, s_ref, y_ref):
    c = pl.program_id(1)

    @pl.when(c == 0)
    def _():
        for p in range(N_PAIR):
            s_ref[p] = jnp.concatenate([s0_ref[2 * p], s0_ref[2 * p + 1]], axis=1)

    lane = lax.broadcasted_iota(I32, (DHA, LANES), 1)
    sub = lax.broadcasted_iota(I32, (DHA, LANES), 0)
    diag = (lane & (DHA - 1)) == sub
    diag_bf = jnp.where(diag, 1.0, 0.0).astype(BF16)
    lo_lanes = lane < DHA
    lo_row = lax.broadcasted_iota(I32, (SUBLANES, LANES), 1) < DHA
    pairs = range(N_PAIR)

    def on_diag(row):
        packed = jnp.broadcast_to(row, (2 * SUBLANES, LANES)).astype(BF16)
        return jnp.concatenate([packed] * (DHA // (2 * SUBLANES)), axis=0) * diag_bf

    def group(base, n):
        rows8 = pl.ds(base, SUBLANES)
        tile = lambda ref, p: ref[rows8, p * LANES:(p + 1) * LANES]
        v_hi = [tile(v_ref, p).astype(BF16).astype(F32) for p in pairs]
        v_lo = [tile(v_ref, p) - v_hi[p] for p in pairs]
        kn_a = [jnp.where(lo_row, tile(kn_ref, p), 0.0) for p in pairs]
        kn_b = [jnp.where(lo_row, 0.0, tile(kn_ref, p)) for p in pairs]
        kn_next = [pltpu.roll(tile(kn_ref, p), SUBLANES - 1, 0) for p in pairs]
        c_b = [_seg_sum_bcast(tile(b_ref, p) * kn_next[p], ones2_ref) for p in pairs]
        c_k = [_seg_sum_bcast(tile(k_ref, p) * kn_next[p], ones2_ref) for p in pairs]
        y_rows = [[] for _ in pairs]
        half_sums = lambda x, a, b: jnp.where(lo_lanes, jnp.sum(x * a, axis=1, keepdims=True),
                                              jnp.sum(x * b, axis=1, keepdims=True))
        for u in range(0, n, 2):
            r0, r1 = slice(u, u + 1), slice(u + 1, u + 2)
            vb = [[_dot(jnp.concatenate([on_diag(v_hi[p][r]), on_diag(v_lo[p][r])], axis=1), ones2_ref[...])
                   for r in (r0, r1)] for p in pairs]
            sa0, sa1 = [], []
            for p in pairs:
                s = s_ref[p]
                w0 = tile(w_ref, p)[r0]
                sa0.append(half_sums(s, kn_a[p][r0], kn_b[p][r0]))
                sa1.append(half_sums(s, w0 * kn_a[p][r1], w0 * kn_b[p][r1]))
            yb = []
            for p in pairs:
                fix = sa1[p] - sa0[p] * c_b[p][r0] + vb[p][0] * c_k[p][r0]
                s_mid = (s_ref[p] * tile(w_ref, p)[r0] - sa0[p] * tile(b_ref, p)[r0]
                         + vb[p][0] * tile(k_ref, p)[r0])
                s_new = s_mid * tile(w_ref, p)[r1] - fix * tile(b_ref, p)[r1] + vb[p][1] * tile(k_ref, p)[r1]
                s_ref[p] = s_new
                yb.append((_dot((s_mid * tile(r_ref, p)[r0]).astype(BF16), ones2_ref[:LANES, :]),
                           _dot((s_new * tile(r_ref, p)[r1]).astype(BF16), ones2_ref[:LANES, :])))
            for p in pairs:
                for y in yb[p]:
                    y_rows[p].append(jnp.sum(jnp.where(diag, y, 0.0), axis=0, keepdims=True))
        for p in pairs:
            rows = y_rows[p] + [jnp.zeros((1, LANES), F32)] * (SUBLANES - n)
            y_ref[rows8, p * LANES:(p + 1) * LANES] = jnp.concatenate(rows, axis=0)

    n_full = n_tok // SUBLANES

    def full_group(g, carry):
        group(pl.multiple_of(g * SUBLANES, SUBLANES), SUBLANES)
        return carry

    lax.fori_loop(0, n_full, full_group, 0, unroll=4 if n_full % 4 == 0 else 1)
    if n_tok % SUBLANES:
        group(n_full * SUBLANES, n_tok % SUBLANES)

    ones = ones2_ref[:LANES, :]
    for p in pairs:
        sl = slice(p * LANES, (p + 1) * LANES)
        y = y_ref[:, sl]
        mean = _dot_ones(y, ones) * (1.0 / DHA)
        yc = y - mean
        var = _dot_ones(yc * yc, ones) * (1.0 / DHA)
        yn = yc * lax.rsqrt(var + LNX_EPS) * g_ref[:, sl] + beta_ref[:, sl]
        bonus = _dot_ones(r_ref[:, sl] * k_ref[:, sl] * rk_ref[:, sl], ones) * v_ref[:, sl]
        y_o[:, sl] = yn + bonus

    @pl.when(c == pl.num_programs(1) - 1)
    def _():
        for p in pairs:
            s_o[2 * p] = s_ref[p][:, :DHA]
            s_o[2 * p + 1] = s_ref[p][:, DHA:]


def _rwkv_scan(seqs, s0, rw, ones_da, tc, n_tok):
    bsz, t, _ = seqs[0].shape
    assert tc == -(-n_tok // SUBLANES) * SUBLANES
    assert n_tok % 2 == 0
    r_k, lnx_g, lnx_b = rw[7:10]
    row = lambda v: v.reshape(1, DA)
    seq_spec = pl.BlockSpec((None, tc, DA), lambda b, c: (b, c, 0))
    st_spec = pl.BlockSpec((None, HA, DHA, DHA), lambda b, c: (b, 0, 0, 0))
    full = lambda shape: pl.BlockSpec(shape, lambda b, c: (0, 0))
    return pl.pallas_call(
        functools.partial(_scan_kernel, n_tok),
        grid=(bsz, t // tc),
        in_specs=[seq_spec] * 6 + [st_spec, full((1, DA)), full((1, DA)), full((1, DA)), full((2 * LANES, LANES))],
        out_specs=[seq_spec, st_spec],
        out_shape=[jax.ShapeDtypeStruct((bsz, t, DA), F32), jax.ShapeDtypeStruct((bsz, HA, DHA, DHA), F32)],
        scratch_shapes=[pltpu.VMEM((N_PAIR, DHA, LANES), F32), pltpu.VMEM((tc, DA), F32)],
        compiler_params=_cparams("parallel", "arbitrary"),
    )(*seqs, s0, row(r_k), row(lnx_g), row(lnx_b), jnp.concatenate([ones_da[:LANES, :LANES]] * 2, axis=0))


def _rope_kernel(qb_ref, kb_ref, vb_ref, qi_ref, ki_ref, c128_ref, s128_ref, c64_ref, s64_ref,
                 q_o, k_o, v_o, kbf_o, vbf_o, qi_o, ki_o, kcat_o):
    c128, s128, c64, s64 = c128_ref[...], s128_ref[...], c64_ref[...], s64_ref[...]
    lane = lax.broadcasted_iota(I32, c64.shape, 1)
    first_half = (lane & (DI - 1)) < DI // 2

    def rope128(x):
        return x * c128 + pltpu.roll(x, DHB // 2, 1) * s128

    def rope64(x):
        partner = jnp.where(first_half, pltpu.roll(x, LANES - DI // 2, 1), pltpu.roll(x, DI // 2, 1))
        return x * c64 + partner * s64

    for h in range(HB):
        sl = slice(h * DHB, (h + 1) * DHB)
        q_o[:, sl] = rope128(qb_ref[:, sl]).astype(BF16)
        kr = rope128(kb_ref[:, sl])
        k_o[h] = kr
        v_o[h] = vb_ref[:, sl]
        kbf_o[:, sl] = kr.astype(BF16)
    vbf_o[...] = vb_ref[...].astype(BF16)
    for h2 in range(HI * DI // LANES):
        sl = slice(h2 * LANES, (h2 + 1) * LANES)
        qi_o[:, sl] = rope64(qi_ref[:, sl])
    ki = rope64(ki_ref[...])[:, :DI]
    ki_o[...] = ki
    kcat_o[...] = _index_key_cat(ki)


def _rope_tables(pos, d):
    inv = ROPE_THETA ** (-jnp.arange(0, d, 2, dtype=F32) / d)
    ang = pos.astype(F32)[:, None] * inv[None, :]
    cos, sin = jnp.cos(ang), jnp.sin(ang)
    reps = LANES // d
    c = jnp.tile(jnp.concatenate([cos, cos], axis=1), (1, reps))
    s = jnp.tile(jnp.concatenate([-sin, sin], axis=1), (1, reps))
    return c, s


def _rope(p, pos, tm):
    t = p.shape[0]
    c128, s128 = _rope_tables(pos, DHB)
    c64, s64 = _rope_tables(pos, DI)
    col = lambda w, c0: pl.BlockSpec((tm, w), lambda i: (i, c0 // w))
    rows = lambda w: pl.BlockSpec((tm, w), lambda i: (i, 0))
    sds = lambda w, dt: jax.ShapeDtypeStruct((t, w), dt)
    heads = pl.BlockSpec((HB, tm, DHB), lambda i: (0, i, 0))
    hsds = jax.ShapeDtypeStruct((HB, t, DHB), F32)
    return pl.pallas_call(
        _rope_kernel,
        grid=(t // tm,),
        in_specs=[col(DB, C_QB), col(DB, C_KB), col(DB, C_VB), col(HI * DI, C_QI), col(LANES, C_KI),
                  rows(LANES), rows(LANES), rows(LANES), rows(LANES)],
        out_specs=[rows(DB), heads, heads, rows(DB), rows(DB), rows(HI * DI), rows(DI), rows(4 * DI)],
        out_shape=[sds(DB, BF16), hsds, hsds, sds(DB, BF16), sds(DB, BF16), sds(HI * DI, F32), sds(DI, F32),
                   sds(4 * DI, BF16)],
        compiler_params=_cparams("parallel"),
    )(p, p, p, p, p, c128, s128, c64, s64)


def _canonical(score):
    return jnp.where(score == 0.0, 0.0, score)


def _ordinal_to_float(u):
    s = u ^ INT_MIN
    b = s ^ ((s >> 31) & 0x7FFFFFFF)
    return lax.bitcast_convert_type(b, F32)


def _index_query_cat(qi):
    qi_hi, qi_lo = _split2(qi)
    out = []
    for h in range(HI):
        sl = slice(h * DI, (h + 1) * DI)
        out.append(jnp.concatenate([qi_hi[:, sl], qi_hi[:, sl], qi_lo[:, sl], qi_lo[:, sl]], axis=1))
    return out


def _index_key_cat(ki):
    k_hi, k_lo = _split2(ki)
    return jnp.concatenate([k_hi, k_lo, k_hi, k_lo], axis=1)


def _count(sc_ref, nc, ck, pred):
    n_acc = 8

    def body(c, accs):
        accs = list(accs)
        base = pl.multiple_of(c * ck, ck)
        for j in range(ck // SUBLANES):
            row0 = base + j * SUBLANES
            accs[j % n_acc] = accs[j % n_acc] + jnp.where(pred(sc_ref[pl.ds(row0, SUBLANES), :], row0), 1, 0)
        return tuple(accs)

    accs = lax.fori_loop(0, nc, body, (jnp.zeros((SUBLANES, LANES), I32),) * n_acc)
    return jnp.sum(functools.reduce(lambda a, b: a + b, accs), axis=0, keepdims=True)


def _select_bias(sc_ref, bias_ref, cut_ref, nc, nc_total, ck, topk, idx_bits):
    sub8 = lax.broadcasted_iota(I32, (SUBLANES, LANES), 0)
    sub128 = lax.broadcasted_iota(I32, (LANES, LANES), 0)

    bits_per_check = 4

    def bit_group(g, carry):
        def refine(carry):
            ans_u, done = carry
            for b in range(bits_per_check):
                cand_u = ans_u | lax.shift_left(jnp.int32(1), 31 - (g * bits_per_check + b))
                cand = _ordinal_to_float(cand_u)
                cnt = _count(sc_ref, nc, ck, lambda blk, _: blk >= cand)
                take = (cnt >= topk) & (done == 0)
                ans_u, done = jnp.where(take, cand_u, ans_u), jnp.where(take & (cnt == topk), 1, done)
            return ans_u, done

        return lax.cond(jnp.min(carry[1]) == 0, refine, lambda carry: carry, carry)

    zeros = jnp.zeros((1, LANES), I32)
    ans_u = lax.fori_loop(0, 32 // bits_per_check, bit_group, (zeros, zeros))[0]
    ans_u = jnp.where((ans_u >= 0) & (ans_u < ORD_NEG_INF), ORD_NEG_INF, ans_u)
    thr = _ordinal_to_float(ans_u)
    n_gt = _count(sc_ref, nc, ck, lambda blk, _: blk > thr)
    n_ge = _count(sc_ref, nc, ck, lambda blk, _: blk >= thr)
    need = topk - n_gt
    tied = ((n_ge - n_gt) > need) & (thr > -jnp.inf)
    cut_ref[...] = jnp.full((1, LANES), 2 ** 30, I32)

    @pl.when(jnp.max(tied.astype(I32)) > 0)
    def _():
        def tie_body(it, ans):
            cand = ans | lax.shift_left(jnp.int32(1), idx_bits - 1 - it)
            g = _count(sc_ref, nc, ck, lambda blk, row0: (blk == thr) & ((sub8 + row0) < cand))
            return jnp.where(g < need, cand, ans)

        cut = lax.fori_loop(0, idx_bits, tie_body, jnp.zeros((1, LANES), I32))
        cut_ref[...] = jnp.where(tied, cut, 2 ** 30)

    cut = cut_ref[...]

    def write(c, carry):
        base = pl.multiple_of(c * ck, ck)
        for u in range(ck // LANES):
            off = base + u * LANES
            blk = sc_ref[pl.ds(off, LANES), :]
            sel = ((blk > thr) | ((blk == thr) & ((sub128 + off) <= cut))) & (blk > -jnp.inf)
            bias_ref[:, pl.ds(off, LANES)] = jnp.where(sel, 0.0, NEG_BIAS).T.astype(bias_ref.dtype)
        return carry

    lax.fori_loop(0, nc, write, 0)

    def fill(c, carry):
        base = pl.multiple_of(c * ck, ck)
        bias_ref[:, pl.ds(base, ck)] = jnp.full((bias_ref.shape[0], ck), NEG_BIAS, bias_ref.dtype)
        return carry

    lax.fori_loop(nc, nc_total, fill, 0)


def _prompt_select_kernel(topk, ck, idx_bits, qi_ref, wi_ref, kcat_ref, bias_o, sc_ref, cut_ref):
    i = pl.program_id(0)
    tq = qi_ref.shape[0]
    t = kcat_ref.shape[0]
    nc = ((i + 1) * tq + ck - 1) // ck
    q_cat = _index_query_cat(qi_ref[...])
    q_pair = [jnp.concatenate(q_cat[2 * j:2 * j + 2], axis=0) for j in range(HI // 2)]
    wi_t = wi_ref[...].T
    kpos = lax.broadcasted_iota(I32, (ck, tq), 0)
    qpos = i * tq + lax.broadcasted_iota(I32, (ck, tq), 1)
    scale = HI ** -0.5 * DI ** -0.5

    def score_chunk(c, carry):
        base = pl.multiple_of(c * ck, ck)
        k_cat = kcat_ref[pl.ds(base, ck), :]
        acc = None
        for h2 in range(HI // 2):
            s2 = _dot_nt(k_cat, q_pair[h2])
            for e in range(2):
                h = 2 * h2 + e
                term = wi_t[h:h + 1, :] * jnp.maximum(s2[:, e * tq:(e + 1) * tq], 0.0)
                acc = term if acc is None else acc + term
        sc = jnp.where(kpos + base <= qpos, acc * scale, -jnp.inf)
        sc_ref[pl.ds(base, ck), :] = _canonical(sc)
        return carry

    lax.fori_loop(0, nc, score_chunk, 0)
    _select_bias(sc_ref, bias_o, cut_ref, nc, t // ck, ck, topk, idx_bits)


def _prompt_select(qi, p, k_cat, ck):
    t = qi.shape[0]
    tq = LANES
    topk = min(TOPK_MAX, t // 4)
    idx_bits = int(np.ceil(np.log2(t))) + 1
    return pl.pallas_call(
        functools.partial(_prompt_select_kernel, topk, ck, idx_bits),
        grid=(t // tq,),
        in_specs=[pl.BlockSpec((tq, HI * DI), lambda i: (i, 0)),
                  pl.BlockSpec((tq, LANES), lambda i: (i, C_WI // LANES)),
                  pl.BlockSpec((t, 4 * DI), lambda i: (0, 0))],
        out_specs=pl.BlockSpec((tq, t), lambda i: (i, 0)),
        out_shape=jax.ShapeDtypeStruct((t, t), BF16),
        scratch_shapes=[pltpu.VMEM((t, tq), F32), pltpu.VMEM((1, LANES), I32)],
        compiler_params=_cparams("parallel"),
    )(qi, p, k_cat)


def _sample_score_kernel(pps, rq, n_new, pt_ref, qit_ref, wif_ref, *refs):
    page_refs, knew_ref, skey_o, skey_new_o = refs[:pps], refs[pps], refs[pps + 1], refs[pps + 2]
    j = pl.program_id(1)
    q_hi, q_lo = _split2(qit_ref[...])
    q_cat = jnp.concatenate([q_hi, q_hi, q_lo, q_lo], axis=1)
    wif = wif_ref[...]

    def scores(kt):
        k_hi, k_lo = _split2(kt)
        at = _dot(q_cat, jnp.concatenate([k_hi, k_lo, k_hi, k_lo], axis=0))
        at = jnp.maximum(at, 0.0) * wif
        sc = at[0:rq]
        for h in range(1, HI):
            sc = sc + at[h * rq:(h + 1) * rq]
        return sc * (HI ** -0.5 * DI ** -0.5)

    sc = scores(jnp.concatenate([r[...] for r in page_refs], axis=1))
    trow = lax.broadcasted_iota(I32, sc.shape, 0)
    skey_o[...] = _canonical(jnp.where(trow < n_new, sc, -jnp.inf))

    @pl.when(j == pl.num_programs(1) - 1)
    def _():
        scn = scores(knew_ref[...])
        tr = lax.broadcasted_iota(I32, scn.shape, 0)
        ln = lax.broadcasted_iota(I32, scn.shape, 1)
        skey_new_o[...] = _canonical(jnp.where((tr < n_new) & (ln <= tr), scn, -jnp.inf))


def _sample_scores(qit, wif, cache_kidx, knew_pad, page_table, n_new, rq, pps):
    bsz, n_pages = page_table.shape
    per_b = lambda b, j, pt: (b, 0, 0)
    page_spec = lambda u: pl.BlockSpec((None, DI, PAGE), lambda b, j, pt: (pt[b, j * pps + u], 0, 0))
    grid_spec = pltpu.PrefetchScalarGridSpec(
        num_scalar_prefetch=1,
        grid=(bsz, n_pages // pps),
        in_specs=[pl.BlockSpec((None, LANES, DI), per_b), pl.BlockSpec((None, LANES, 1), per_b)]
                 + [page_spec(u) for u in range(pps)] + [pl.BlockSpec((None, DI, PAGE), per_b)],
        out_specs=[pl.BlockSpec((None, rq, pps * PAGE), lambda b, j, pt: (b, 0, j)),
                   pl.BlockSpec((None, rq, PAGE), per_b)],
    )
    return pl.pallas_call(
        functools.partial(_sample_score_kernel, pps, rq, n_new),
        grid_spec=grid_spec,
        out_shape=[jax.ShapeDtypeStruct((bsz, rq, n_pages * PAGE), F32), jax.ShapeDtypeStruct((bsz, rq, PAGE), F32)],
        compiler_params=_cparams("parallel", "arbitrary"),
    )(page_table, qit, wif, *([cache_kidx] * pps), knew_pad)


def _sample_select_kernel(topk, ck, idx_bits, sc_past, sc_new, bias_o, sc_ref, cut_ref):
    past = sc_past.shape[1]

    def flip(c, carry):
        off = pl.multiple_of(c * LANES, LANES)
        sc_ref[pl.ds(off, LANES), :] = sc_past[:, pl.ds(off, LANES)].T
        return carry

    lax.fori_loop(0, past // LANES, flip, 0)
    sc_ref[past:, :] = sc_new[...].T
    nc = sc_ref.shape[0] // ck
    _select_bias(sc_ref, bias_o, cut_ref, nc, nc, ck, topk, idx_bits)


def _sample_select(skey_past, skey_new, topk):
    rows, past = skey_past.shape
    n = past + skey_new.shape[1]
    tr = LANES
    idx_bits = int(np.ceil(np.log2(n))) + 1
    return pl.pallas_call(
        functools.partial(_sample_select_kernel, topk, PAGE, idx_bits),
        grid=(rows // tr,),
        in_specs=[pl.BlockSpec((tr, past), lambda i: (i, 0)), pl.BlockSpec((tr, n - past), lambda i: (i, 0))],
        out_specs=pl.BlockSpec((tr, n), lambda i: (i, 0)),
        out_shape=jax.ShapeDtypeStruct((rows, n), F32),
        scratch_shapes=[pltpu.VMEM((n, tr), F32), pltpu.VMEM((1, LANES), I32)],
        compiler_params=_cparams("parallel"),
    )(skey_past, skey_new)


def _prompt_attn_kernel(ck, q_ref, k_ref, v_ref, bias_ref, o_ref, m_ref, l_ref, acc_ref):
    i = pl.program_id(0)
    tq = q_ref.shape[0]
    nc = ((i + 1) * tq + ck - 1) // ck
    m_ref[...] = jnp.full(m_ref.shape, NEG_BIAS, F32)
    l_ref[...] = jnp.zeros(l_ref.shape, F32)
    acc_ref[...] = jnp.zeros(acc_ref.shape, F32)
    scale = DHB ** -0.5

    def chunk(c, carry):
        base = pl.multiple_of(c * ck, ck)
        bias = bias_ref[:, pl.ds(base, ck)].astype(F32)
        for h in range(HB):
            sl = slice(h * DHB, (h + 1) * DHB)
            s = _dot_nt(q_ref[:, sl], k_ref[pl.ds(base, ck), sl]) * scale + bias
            m_old = m_ref[h]
            m_new = jnp.maximum(m_old, jnp.max(s, axis=1, keepdims=True))
            alpha = jnp.exp(m_old - m_new)
            p = jnp.exp(s - m_new)
            l_ref[h] = alpha * l_ref[h] + jnp.sum(p, axis=1, keepdims=True)
            acc_ref[h] = alpha * acc_ref[h] + _dot(p.astype(BF16), v_ref[pl.ds(base, ck), sl])
            m_ref[h] = m_new
        return carry

    lax.fori_loop(0, nc, chunk, 0)
    for h in range(HB):
        o_ref[:, h * DHB:(h + 1) * DHB] = acc_ref[h] / l_ref[h]


def _prompt_attn(q_bf, k_bf, v_bf, bias, tq, ck):
    t = q_bf.shape[0]
    resident = lambda: pl.BlockSpec((t, DB), lambda i: (0, 0), pipeline_mode=pl.Buffered(1))
    return pl.pallas_call(
        functools.partial(_prompt_attn_kernel, ck),
        grid=(t // tq,),
        in_specs=[pl.BlockSpec((tq, DB), lambda i: (i, 0)), resident(), resident(),
                  pl.BlockSpec((tq, t), lambda i: (i, 0))],
        out_specs=pl.BlockSpec((tq, DB), lambda i: (i, 0)),
        out_shape=jax.ShapeDtypeStruct((t, DB), F32),
        scratch_shapes=[pltpu.VMEM((HB, tq, 1), F32), pltpu.VMEM((HB, tq, 1), F32), pltpu.VMEM((HB, tq, DHB), F32)],
        compiler_params=_cparams("parallel"),
    )(q_bf, k_bf, v_bf, bias)


def _sample_attn_kernel(pps, rq, pt_ref, qbd_ref, *refs):
    kpages, vpages = refs[:pps], refs[pps:2 * pps]
    knew_ref, vnew_ref, bias_ref, bias_new_ref, o_ref, m_ref, l_ref, acc_ref = refs[2 * pps:]
    j = pl.program_id(1)
    nr = HB * rq

    @pl.when(j == 0)
    def _():
        m_ref[...] = jnp.full(m_ref.shape, NEG_BIAS, F32)
        l_ref[...] = jnp.zeros(l_ref.shape, F32)
        acc_ref[...] = jnp.zeros(acc_ref.shape, F32)

    def accumulate(kp, vp, bias):
        st = _dot(kp, qbd_ref[...])
        s = st.T[:nr] * (DHB ** -0.5) + jnp.concatenate([bias] * HB, axis=0)
        m_old = m_ref[...]
        m_new = jnp.maximum(m_old, jnp.max(s, axis=1, keepdims=True))
        alpha = jnp.exp(m_old - m_new)
        p = jnp.exp(s - m_new)
        l_ref[...] = alpha * l_ref[...] + jnp.sum(p, axis=1, keepdims=True)
        acc_ref[...] = alpha * acc_ref[...] + _dot(p.astype(BF16), vp)
        m_ref[...] = m_new

    gather = lambda pages: jnp.concatenate(
        [jnp.concatenate([r[h].astype(BF16) for h in range(HB)], axis=1) for r in pages], axis=0)
    accumulate(gather(kpages), gather(vpages), bias_ref[...])

    @pl.when(j == pl.num_programs(1) - 1)
    def _():
        accumulate(knew_ref[...].astype(BF16), vnew_ref[...].astype(BF16), bias_new_ref[...])
        for h in range(HB):
            rs = slice(h * rq, (h + 1) * rq)
            cs = slice(h * DHB, (h + 1) * DHB)
            o_ref[:, cs] = acc_ref[rs, cs] / l_ref[rs, :]


def _sample_attn(qbd, cache_k, cache_v, knew_pad, vnew_pad, bias, page_table, rq, pps):
    bsz, n_pages = page_table.shape
    page_spec = lambda u: pl.BlockSpec((None, HB, PAGE, DHB), lambda b, j, pt: (pt[b, j * pps + u], 0, 0, 0))
    per_b = lambda b, j, pt: (b, 0, 0)
    grid_spec = pltpu.PrefetchScalarGridSpec(
        num_scalar_prefetch=1,
        grid=(bsz, n_pages // pps),
        in_specs=[pl.BlockSpec((None, DB, LANES), per_b)]
                 + [page_spec(u) for u in range(pps)] * 2
                 + [pl.BlockSpec((None, PAGE, DB), per_b), pl.BlockSpec((None, PAGE, DB), per_b),
                    pl.BlockSpec((None, rq, pps * PAGE), lambda b, j, pt: (b, 0, j)),
                    pl.BlockSpec((None, rq, PAGE), lambda b, j, pt: (b, 0, n_pages))],
        out_specs=pl.BlockSpec((None, rq, DB), per_b),
        scratch_shapes=[pltpu.VMEM((HB * rq, 1), F32), pltpu.VMEM((HB * rq, 1), F32), pltpu.VMEM((HB * rq, DB), F32)],
    )
    return pl.pallas_call(
        functools.partial(_sample_attn_kernel, pps, rq),
        grid_spec=grid_spec,
        out_shape=jax.ShapeDtypeStruct((bsz, rq, DB), F32),
        compiler_params=_cparams("parallel", "arbitrary"),
    )(page_table, qbd, *([cache_k] * pps), *([cache_v] * pps), knew_pad, vnew_pad, bias, bias)


def _cross_kernel(q_ref, mk_ref, mv_ref, o_ref):
    scale = DHC ** -0.5
    for h in range(HC):
        sl = slice(h * DHC, (h + 1) * DHC)
        s = _dot_nt(q_ref[:, sl].astype(BF16), mk_ref[h].astype(BF16)) * scale
        m = jnp.max(s, axis=1, keepdims=True)
        p = jnp.exp(s - m)
        l = jnp.sum(p, axis=1, keepdims=True)
        o_ref[:, sl] = _dot(p.astype(BF16), mv_ref[h].astype(BF16)) / l


def _cross_attn(q_arr, q_col, mk, mv, tm):
    bsz, t, _ = q_arr.shape
    return pl.pallas_call(
        _cross_kernel,
        grid=(bsz, t // tm),
        in_specs=[pl.BlockSpec((None, tm, DC), lambda b, i: (b, i, q_col // DC)),
                  pl.BlockSpec((None, HC, N_MEM, DHC), lambda b, i: (b, 0, 0, 0)),
                  pl.BlockSpec((None, HC, N_MEM, DHC), lambda b, i: (b, 0, 0, 0))],
        out_specs=pl.BlockSpec((None, tm, DC), lambda b, i: (b, i, 0)),
        out_shape=jax.ShapeDtypeStruct((bsz, t, DC), F32),
        compiler_params=_cparams("parallel", "parallel"),
    )(q_arr, mk, mv)


def _merge_kernel(x_ref, ya_ref, za_ref, yb_ref, zb_ref, yc_ref, zc_ref, ga_ref, gb_ref, gc_ref,
                  bga_ref, bgb_ref, bgc_ref, wa_ref, wb_ref, wc_ref, wo_ref, fg_ref, o_ref):
    ba = _dot((ya_ref[...] * _silu(za_ref[...])).astype(BF16), wa_ref[...])
    bb = _dot((yb_ref[...] * _silu(zb_ref[...])).astype(BF16), wb_ref[...])
    bc = _dot((yc_ref[...] * _silu(zc_ref[...])).astype(BF16), wc_ref[...])
    mix = (_sigmoid(ga_ref[...] + bga_ref[...]) * ba + _sigmoid(gb_ref[...] + bgb_ref[...]) * bb
           + _sigmoid(gc_ref[...] + bgc_ref[...]) * bc)
    y = x_ref[...] + _dot(mix.astype(BF16), wo_ref[...])
    ms = jnp.mean(y * y, axis=-1, keepdims=True)
    o_ref[...] = y * lax.rsqrt(ms + EPS) * fg_ref[...]


def _merge(x2d, p, ya, yb, yc, b_gate, wa, wb, wc, wo, fg, tm):
    t, d = x2d.shape
    rows = lambda w: pl.BlockSpec((tm, w), lambda i: (i, 0))
    col = lambda w, c0: pl.BlockSpec((tm, w), lambda i: (i, c0 // w))
    const = lambda shape, c=0: pl.BlockSpec(shape, lambda i: (0, c), pipeline_mode=pl.Buffered(1))
    bg = b_gate.reshape(1, 3 * d)
    return pl.pallas_call(
        _merge_kernel,
        grid=(t // tm,),
        in_specs=[rows(d), rows(DA), col(DA, C_ZA), rows(DB), col(DB, C_ZB), rows(DC), col(DC, C_ZC),
                  col(d, C_GATES), col(d, C_GATES + d), col(d, C_GATES + 2 * d),
                  const((1, d), 0), const((1, d), 1), const((1, d), 2),
                  const((DA, d)), const((DB, d)), const((DC, d)), const((d, d)), const((1, d))],
        out_specs=rows(d),
        out_shape=jax.ShapeDtypeStruct((t, d), F32),
        compiler_params=_cparams("parallel"),
    )(x2d, ya, p, yb, p, yc, p, p, p, p, bg, bg, bg, wa, wb, wc, wo, fg.reshape(1, d))


def _block_ones(n, seg):
    idx = np.arange(n) // seg
    return jnp.asarray(idx[:, None] == idx[None, :], dtype=BF16)


def _shift_row(p_row):
    return jnp.concatenate([p_row[:3 * DA], p_row[C_WDAD:C_WDAD + 2 * LORA]])


def kernel(x_prompt, x_sample, mem_prompt, cache_k, cache_v, cache_kidx, cache_mem_k, cache_mem_v, state_wkv, state_shift, page_table, norm_g, w_in, rwkv_mu, rwkv_w0, rwkv_w_up, rwkv_a0, rwkv_a_up, rwkv_k_k, rwkv_k_a, rwkv_r_k, rwkv_lnx_g, rwkv_lnx_b, mem_norm_g, w_mem_kv, b_gate, w_br_a, w_br_b, w_br_c, w_out, final_norm_g):
    depth = w_in.shape[0]
    assert depth == 1 and x_prompt.shape[0] == 1
    t = x_prompt.shape[1]
    bs, ts, d = x_sample.shape
    n_pages = page_table.shape[1]
    past = n_pages * PAGE
    rq = 8
    assert ts <= rq and (ts & (ts - 1)) == 0

    rw = (rwkv_mu[0], rwkv_w0[0], rwkv_w_up[0], rwkv_a0[0], rwkv_a_up[0], rwkv_k_k[0], rwkv_k_a[0],
          rwkv_r_k[0].reshape(DA), rwkv_lnx_g[0], rwkv_lnx_b[0])
    ones_da = _block_ones(DA, DHA)
    w_p = _permute_w_in(w_in[0])
    wa, wb, wc, wo = (w[0].astype(BF16) for w in (w_br_a, w_br_b, w_br_c, w_out))

    xp2 = x_prompt.reshape(t, d)
    xs2 = x_sample.reshape(bs * ts, d)
    tm_p = min(1024, t)
    p_p = _proj(xp2, norm_g[0], w_p, tm_p, 1024)
    p_s = _proj(xs2, norm_g[0], w_p, bs * ts, 1024)
    mkv = _proj(mem_prompt.reshape(N_MEM, d), mem_norm_g[0], w_mem_kv[0].astype(BF16), N_MEM, 2 * DC)
    mk_p = mkv[:, :DC].reshape(1, N_MEM, HC, DHC)
    mv_p = mkv[:, DC:].reshape(1, N_MEM, HC, DHC)
    head_major = lambda a: jnp.transpose(a, (0, 2, 1, 3))

    seq_p = _rwkv_prep(p_p, jnp.zeros((1, SHIFT_W), F32), rw, ones_da, min(256, t), True)
    ya_p, wkv_p = _rwkv_scan([a[None] for a in seq_p], jnp.zeros((1, HA, DHA, DHA), F32), rw, ones_da,
                             min(256, t), min(256, t))
    pinit_s = jnp.repeat(state_shift[0], ts, axis=0)
    seq_s = _rwkv_prep(p_s, pinit_s, rw, ones_da, bs * ts, False, ts)
    pad_t = lambda a: jnp.pad(a.reshape(bs, ts, -1), ((0, 0), (0, rq - ts), (0, 0)))
    ya_s, wkv_s = _rwkv_scan([pad_t(a) for a in seq_s], state_wkv[0], rw, ones_da, rq, ts)
    ya_s = ya_s[:, :ts].reshape(bs * ts, DA)

    tokens_major = lambda a: jnp.transpose(a, (1, 0, 2))
    q_p, k_p, v_p, kbf_p, vbf_p, qi_p, ki_p, kcat_p = _rope(p_p, jnp.arange(t), min(512, t))
    pos_s = jnp.tile(past + jnp.arange(ts), bs)
    q_s, k_s, v_s, _, _, qi_s, ki_s, _ = _rope(p_s, pos_s, bs * ts)
    k_s, v_s = tokens_major(k_s), tokens_major(v_s)

    bias_p = _prompt_select(qi_p, p_p, kcat_p, 512)
    yb_p = _prompt_attn(q_p, kbf_p, vbf_p, bias_p, min(256, t), 1024)

    topk_s = min(TOPK_MAX, (past + ts) // 4)
    pad_rows = lambda a, n: jnp.pad(a, ((0, 0), (0, n - a.shape[1]), (0, 0)))
    pps = 8 if n_pages % 8 == 0 else 1
    qit = pad_rows(qi_s.reshape(bs, ts, HI * DI), rq).reshape(bs, rq, HI, DI).transpose(0, 2, 1, 3)
    qit = pad_rows(qit.reshape(bs, HI * rq, DI), LANES)
    wif = pad_rows(p_s[:, C_WI:C_WI + HI].reshape(bs, ts, HI), rq).transpose(0, 2, 1).reshape(bs, HI * rq, 1)
    wif = pad_rows(wif, LANES)
    kinew = jnp.swapaxes(pad_rows(ki_s.reshape(bs, ts, DI), PAGE), 1, 2)
    skey_past, skey_new = _sample_scores(qit, wif, jnp.swapaxes(cache_kidx[0], 1, 2), kinew, page_table, ts,
                                         rq, pps)
    n_rows = -(-bs * rq // LANES) * LANES
    lane_rows = lambda a: jnp.pad(a.reshape(bs * rq, -1), ((0, n_rows - bs * rq), (0, 0)), constant_values=-jnp.inf)
    bias_s = _sample_select(lane_rows(skey_past), lane_rows(skey_new), topk_s)
    bias_s = bias_s[:bs * rq].reshape(bs, rq, past + PAGE)
    knew = pad_rows(k_s.reshape(bs, ts, DB), PAGE)
    vnew = pad_rows(v_s.reshape(bs, ts, DB), PAGE)
    qh = pad_rows(q_s.reshape(bs, ts, DB), rq).reshape(bs, rq, HB, DHB).transpose(0, 2, 3, 1)
    head_eq = jnp.eye(HB, dtype=bool)[None, :, None, :, None]
    qbd = jnp.where(head_eq, qh[:, :, :, None, :], jnp.zeros((), BF16)).reshape(bs, DB, HB * rq)
    qbd = jnp.pad(qbd, ((0, 0), (0, 0), (0, LANES - HB * rq)))
    yb_s = _sample_attn(qbd, head_major(cache_k[0]), head_major(cache_v[0]), knew, vnew, bias_s, page_table,
                        rq, pps)
    yb_s = yb_s[:, :ts].reshape(bs * ts, DB)

    yc_p = _cross_attn(p_p[None], C_QC, head_major(mk_p), head_major(mv_p), min(512, t))[0]
    qc8 = pad_rows(p_s[:, C_QC:C_QC + DC].reshape(bs, ts, DC), rq)
    yc_s = _cross_attn(qc8, 0, head_major(cache_mem_k[0]), head_major(cache_mem_v[0]), rq)
    yc_s = yc_s[:, :ts].reshape(bs * ts, DC)

    y_p = _merge(xp2, p_p, ya_p[0], yb_p, yc_p, b_gate[0], wa, wb, wc, wo, final_norm_g, min(256, t))
    y_s = _merge(xs2, p_s, ya_s, yb_s, yc_s, b_gate[0], wa, wb, wc, wo, final_norm_g, bs * ts)

    p_s3 = p_s.reshape(bs, ts, N_P)
    shift_s = jnp.concatenate([p_s3[:, -1, :3 * DA], p_s3[:, -1, C_WDAD:C_WDAD + 2 * LORA]], axis=-1)
    return (y_p.reshape(1, t, d), y_s.reshape(bs, ts, d),
            tokens_major(k_p).reshape(1, 1, t, HB, DHB), tokens_major(v_p).reshape(1, 1, t, HB, DHB),
            ki_p.reshape(1, 1, t, DI),
            mk_p.reshape(1, 1, N_MEM, HC, DHC), mv_p.reshape(1, 1, N_MEM, HC, DHC),
            wkv_p.reshape(1, 1, HA, DHA, DHA), _shift_row(p_p[t - 1]).reshape(1, 1, SHIFT_W),
            k_s.reshape(1, bs, ts, HB, DHB), v_s.reshape(1, bs, ts, HB, DHB), ki_s.reshape(1, bs, ts, DI),
            wkv_s.reshape(1, bs, HA, DHA, DHA), shift_s.reshape(1, bs, SHIFT_W))
```
